```python
import math
import jax, jax.numpy as jnp
from jax import lax
import numpy as np

D_MODEL = 1024
BATCH = 2
SEQ = 8192
DEPTH = 4

A_HEADS = 6
A_HEAD_DIM = 64
A_CONFIGS = ((128, 1), (512, 4), (2048, 16))
B_HEADS = 4
B_QK_DIM = 64
B_V_DIM = 2 * B_QK_DIM
C_HEADS = 6
C_NOPE_DIM = 64
C_ROPE_DIM = 32
C_V_DIM = 64
C_Q_RANK = 256
C_KV_RANK = 128
ROPE_BASE = 10000.0
N_BUCKETS = 32
MAX_DISTANCE = 2048
BIAS_HEADS = A_HEADS + B_HEADS
PEER_HEADS = 8
PEER_KEYS = 128
PEER_EXPERTS = PEER_KEYS * PEER_KEYS
PEER_KEY_DIM = 128
PEER_TOPK = 16
PEER_CHUNK = 128
Q_BLOCK = 128
RMS_EPS = 1e-6
NEG_INF = -1e30

A_COLS = 3 * A_HEADS * A_HEAD_DIM
B_QK_COLS = B_HEADS * 2 * B_QK_DIM
B_V_COLS = B_HEADS * B_V_DIM
C_COLS = C_Q_RANK + C_KV_RANK + C_ROPE_DIM
GATE_COLS = 3 * D_MODEL
IN_COLS = A_COLS + 2 * B_QK_COLS + B_V_COLS + C_COLS + GATE_COLS
A_WIDTH = A_HEADS * A_HEAD_DIM
B_WIDTH = B_HEADS * B_V_DIM
C_WIDTH = C_HEADS * C_V_DIM

kernel_name = "hybrid_gated_dilated_diff_mla_peer"


def rmsnorm(x, g):
    xf = x.astype(jnp.float32)
    y = xf * lax.rsqrt(jnp.mean(xf * xf, axis=-1, keepdims=True) + RMS_EPS)
    return (y * g.astype(jnp.float32)).astype(x.dtype)


def t5_bucket(dist):
    n = jnp.maximum(dist, 0)
    max_exact = N_BUCKETS // 2
    nf = jnp.maximum(n, max_exact).astype(jnp.float32)
    large = max_exact + (jnp.log(nf / max_exact) / math.log(MAX_DISTANCE / max_exact)
                         * (N_BUCKETS - max_exact)).astype(jnp.int32)
    large = jnp.minimum(large, N_BUCKETS - 1)
    return jnp.where(n < max_exact, n, large)


def rope_tables(seq_len):
    half = C_ROPE_DIM // 2
    inv = ROPE_BASE ** (-jnp.arange(half, dtype=jnp.float32) / half)
    ang = jnp.arange(seq_len, dtype=jnp.float32)[:, None] * inv[None, :]
    return jnp.cos(ang), jnp.sin(ang)


def apply_rope(x, cos, sin):
    x1, x2 = jnp.split(x, 2, axis=-1)
    c = cos.astype(x.dtype)
    s = sin.astype(x.dtype)
    return jnp.concatenate([x1 * c - x2 * s, x1 * s + x2 * c], axis=-1)


def strided_window_attention(q, k, v, bias_table, window, dilation):
    bsz, seq, heads, dh = q.shape
    w = window // dilation
    span = w * dilation
    sp = -(-seq // span) * span
    pad = sp - seq
    nb = sp // span

    def to_blocks(t):
        t = jnp.pad(t, ((0, 0), (0, pad), (0, 0), (0, 0)))
        t = t.reshape(bsz, sp // dilation, dilation, heads, dh)
        t = t.transpose(0, 2, 1, 3, 4)
        return t.reshape(bsz, dilation, nb, w, heads, dh)

    def with_prev(t):
        prev = jnp.pad(t[:, :, :-1], ((0, 0), (0, 0), (1, 0), (0, 0), (0, 0), (0, 0)))
        return jnp.concatenate([prev, t], axis=3)

    qb = to_blocks(q)
    kk = with_prev(to_blocks(k))
    vv = with_prev(to_blocks(v))
    logits = jnp.einsum('brnqhd,brnkhd->brnhqk', qb, kk).astype(jnp.float32) / math.sqrt(dh)
    qi = jnp.arange(w)[:, None]
    ki = jnp.arange(2 * w)[None, :]
    rel = w + qi - ki
    band = (rel >= 0) & (rel <= w)
    bias = bias_table[t5_bucket(rel * dilation)].astype(jnp.float32)
    logits = logits + bias.transpose(2, 0, 1)
    first = (jnp.arange(nb)[:, None, None] == 0) & (ki[None] < w)
    valid = band[None] & ~first
    logits = jnp.where(valid[None, None, :, None], logits, NEG_INF)
    lse = jax.nn.logsumexp(logits, axis=-1)
    p = jnp.exp(logits - lse[..., None]).astype(v.dtype)
    out = jnp.einsum('brnhqk,brnkhd->brnqhd', p, vv)
    out = out.reshape(bsz, dilation, sp // dilation, heads, dh).transpose(0, 2, 1, 3, 4)
    out = out.reshape(bsz, sp, heads, dh)[:, :seq]
    lse = lse.transpose(0, 1, 2, 4, 3).reshape(bsz, dilation, sp // dilation, heads)
    lse = lse.transpose(0, 2, 1, 3).reshape(bsz, sp, heads)[:, :seq]
    return out, lse


def dilated_attention(q, k, v, bias_table):
    outs, lses = [], []
    for window, dilation in A_CONFIGS:
        o, l = strided_window_attention(q, k, v, bias_table, window, dilation)
        outs.append(o)
        lses.append(l)
    wts = jax.nn.softmax(jnp.stack(lses, 0), axis=0)
    return jnp.einsum('gbsh,gbshd->bshd', wts.astype(q.dtype), jnp.stack(outs, 0))


def to_qblocks(t):
    bsz, heads, seq, dd = t.shape
    return t.reshape(bsz, heads, seq // Q_BLOCK, Q_BLOCK, dd).transpose(2, 0, 1, 3, 4)


def from_qblocks(out):
    nb, bsz, heads, qb, dd = out.shape
    return out.transpose(1, 0, 3, 2, 4).reshape(bsz, nb * qb, heads, dd)


def diff_attention(q1, q2, k1, k2, v, bias_table, lam, lam_init, subln_g):
    bsz, heads, seq, d = q1.shape
    nb = seq // Q_BLOCK
    scale = 1.0 / math.sqrt(d)
    kpos = jnp.arange(seq)

    def body(args):
        i, qa, qb = args
        qpos = i * Q_BLOCK + jnp.arange(Q_BLOCK)
        dist = qpos[:, None] - kpos[None, :]
        causal = dist >= 0
        bias = bias_table[t5_bucket(dist)].astype(jnp.float32).transpose(2, 0, 1)

        def probs(qx, kx):
            s = jnp.einsum('bhqd,bhkd->bhqk', qx, kx).astype(jnp.float32) * scale + bias
            return jax.nn.softmax(jnp.where(causal, s, NEG_INF), axis=-1)

        a = probs(qa, k1) - lam * probs(qb, k2)
        return jnp.einsum('bhqk,bhkd->bhqd', a.astype(v.dtype), v)

    out = lax.map(body, (jnp.arange(nb), to_qblocks(q1), to_qblocks(q2)))
    out = from_qblocks(out)
    out = rmsnorm(out, subln_g) * (1.0 - lam_init)
    return out.reshape(bsz, seq, heads * 2 * d)


def mla_attention(q_nope, q_rope, k_nope, k_rope, v):
    bsz, heads, seq, _ = q_nope.shape
    nb = seq // Q_BLOCK
    scale = 1.0 / math.sqrt(C_NOPE_DIM + C_ROPE_DIM)
    kpos = jnp.arange(seq)

    def body(args):
        i, qn, qr = args
        qpos = i * Q_BLOCK + jnp.arange(Q_BLOCK)
        causal = (qpos[:, None] - kpos[None, :]) >= 0
        s = (jnp.einsum('bhqd,bhkd->bhqk', qn, k_nope)
             + jnp.einsum('bhqd,bkd->bhqk', qr, k_rope)).astype(jnp.float32) * scale
        p = jax.nn.softmax(jnp.where(causal, s, NEG_INF), axis=-1)
        return jnp.einsum('bhqk,bhkd->bhqd', p.astype(v.dtype), v)

    out = lax.map(body, (jnp.arange(nb), to_qblocks(q_nope), to_qblocks(q_rope)))
    return from_qblocks(out).reshape(bsz, seq, heads * C_V_DIM)


def peer_ffn(h, w_q, sub_keys, u, v_e):
    bsz, seq, d = h.shape
    q = (h @ w_q).reshape(bsz, seq, PEER_HEADS, 2, PEER_KEY_DIM // 2)
    scores = jnp.einsum('bshpd,hpkd->bshpk', q, sub_keys).astype(jnp.float32)
    top_s, top_i = lax.top_k(scores, PEER_TOPK)
    cand = top_s[..., 0, :, None] + top_s[..., 1, None, :]
    best_s, best_c = lax.top_k(cand.reshape(bsz, seq, PEER_HEADS, PEER_TOPK * PEER_TOPK), PEER_TOPK)
    i1 = jnp.take_along_axis(top_i[..., 0, :], best_c // PEER_TOPK, axis=-1)
    i2 = jnp.take_along_axis(top_i[..., 1, :], best_c % PEER_TOPK, axis=-1)
    idx = i1 * PEER_KEYS + i2
    gate = jax.nn.softmax(best_s, axis=-1).astype(h.dtype)
    n_chunks = (bsz * seq) // PEER_CHUNK
    idx = idx.reshape(n_chunks, PEER_CHUNK, PEER_HEADS * PEER_TOPK)
    gate = gate.reshape(n_chunks, PEER_CHUNK, PEER_HEADS * PEER_TOPK)
    hc = h.reshape(n_chunks, PEER_CHUNK, d)

    def body(args):
        ic, gc, xc = args
        act = jax.nn.gelu(jnp.einsum('tkd,td->tk', u[ic], xc), approximate=False)
        return jnp.einsum('tk,tkd->td', gc * act, v_e[ic])

    out = lax.map(body, (idx, gate, hc))
    return out.reshape(bsz, seq, d)


def split_in_proj(proj):
    sizes = [A_COLS, B_QK_COLS, B_QK_COLS, B_V_COLS, C_Q_RANK, C_KV_RANK, C_ROPE_DIM]
    bounds, acc = [], 0
    for s in sizes:
        acc += s
        bounds.append(acc)
    return jnp.split(proj, bounds, axis=-1)


def setup_inputs(seed: int = 0) -> dict:
    key = jax.random.key(seed)
    ks = jax.random.split(key, 24)
    f32 = jnp.float32

    def nrm(k, shape, scale):
        return jax.random.normal(k, shape, f32) * scale

    def gain(k, shape):
        return 1.0 + 0.02 * jax.random.normal(k, shape, f32)

    L, D = DEPTH, D_MODEL
    return {
        "x": nrm(ks[0], (BATCH, SEQ, D), 1.0),
        "rel_bias": nrm(ks[1], (N_BUCKETS, BIAS_HEADS), 0.5),
        "w_in": nrm(ks[2], (L, D, IN_COLS), D ** -0.5),
        "g_mix": gain(ks[3], (L, D)),
        "w_uq": nrm(ks[4], (L, C_Q_RANK, C_HEADS * (C_NOPE_DIM + C_ROPE_DIM)), C_Q_RANK ** -0.5),
        "g_cq": gain(ks[5], (L, C_Q_RANK)),
        "w_ukv": nrm(ks[6], (L, C_KV_RANK, C_HEADS * (C_NOPE_DIM + C_V_DIM)), C_KV_RANK ** -0.5),
        "g_ckv": gain(ks[7], (L, C_KV_RANK)),
        "lam_q1": nrm(ks[8], (L, B_QK_DIM), 0.1),
        "lam_k1": nrm(ks[9], (L, B_QK_DIM), 0.1),
        "lam_q2": nrm(ks[10], (L, B_QK_DIM), 0.1),
        "lam_k2": nrm(ks[11], (L, B_QK_DIM), 0.1),
        "g_subln": gain(ks[12], (L, B_V_DIM)),
        "w_branch_a": nrm(ks[13], (L, A_WIDTH, D), A_WIDTH ** -0.5),
        "w_branch_b": nrm(ks[14], (L, B_WIDTH, D), B_WIDTH ** -0.5),
        "w_branch_c": nrm(ks[15], (L, C_WIDTH, D), C_WIDTH ** -0.5),
        "w_out": nrm(ks[16], (L, D, D), D ** -0.5),
        "g_ffn": gain(ks[17], (L, D)),
        "w_peer_q": nrm(ks[18], (L, D, PEER_HEADS * PEER_KEY_DIM), D ** -0.5),
        "peer_sub_keys": nrm(ks[19], (L, PEER_HEADS, 2, PEER_KEYS, PEER_KEY_DIM // 2), (PEER_KEY_DIM // 2) ** -0.5),
        "peer_u": nrm(ks[20], (L, PEER_EXPERTS, D), D ** -0.5),
        "peer_v": nrm(ks[21], (L, PEER_EXPERTS, D), (PEER_HEADS * PEER_TOPK) ** -0.5),
        "g_final": gain(ks[22], (D,)),
    }


def reference(x, rel_bias, w_in, g_mix, w_uq, g_cq, w_ukv, g_ckv, lam_q1, lam_k1, lam_q2, lam_k2,
              g_subln, w_branch_a, w_branch_b, w_branch_c, w_out, g_ffn, w_peer_q, peer_sub_keys,
              peer_u, peer_v, g_final):
    bsz, seq, d = x.shape
    cos, sin = rope_tables(seq)
    bias_a = rel_bias[:, :A_HEADS]
    bias_b = rel_bias[:, A_HEADS:]
    for l in range(DEPTH):
        h = rmsnorm(x, g_mix[l])
        proj = h @ w_in[l]
        a_qkv, b_q, b_k, b_v, c_q, c_kv, c_kr, gates = split_in_proj(proj)

        a = a_qkv.reshape(bsz, seq, 3, A_HEADS, A_HEAD_DIM)
        y_a = dilated_attention(a[:, :, 0], a[:, :, 1], a[:, :, 2], bias_a).reshape(bsz, seq, A_WIDTH)

        bq = b_q.reshape(bsz, seq, B_HEADS, 2, B_QK_DIM).transpose(3, 0, 2, 1, 4)
        bk = b_k.reshape(bsz, seq, B_HEADS, 2, B_QK_DIM).transpose(3, 0, 2, 1, 4)
        bv = b_v.reshape(bsz, seq, B_HEADS, B_V_DIM).transpose(0, 2, 1, 3)
        lam_init = 0.8 - 0.6 * math.exp(-0.3 * l)
        lam = (jnp.exp(jnp.sum(lam_q1[l].astype(jnp.float32) * lam_k1[l].astype(jnp.float32)))
               - jnp.exp(jnp.sum(lam_q2[l].astype(jnp.float32) * lam_k2[l].astype(jnp.float32)))
               + lam_init)
        y_b = diff_attention(bq[0], bq[1], bk[0], bk[1], bv, bias_b, lam, lam_init, g_subln[l])

        cq = (rmsnorm(c_q, g_cq[l]) @ w_uq[l]).reshape(bsz, seq, C_HEADS, C_NOPE_DIM + C_ROPE_DIM)
        q_nope, q_rope = cq[..., :C_NOPE_DIM], cq[..., C_NOPE_DIM:]
        q_rope = apply_rope(q_rope, cos[:, None, :], sin[:, None, :])
        ckv = (rmsnorm(c_kv, g_ckv[l]) @ w_ukv[l]).reshape(bsz, seq, C_HEADS, C_NOPE_DIM + C_V_DIM)
        k_nope, v_c = ckv[..., :C_NOPE_DIM], ckv[..., C_NOPE_DIM:]
        k_rope = apply_rope(c_kr, cos, sin)
        y_c = mla_attention(q_nope.transpose(0, 2, 1, 3), q_rope.transpose(0, 2, 1, 3),
                            k_nope.transpose(0, 2, 1, 3), k_rope, v_c.transpose(0, 2, 1, 3))

        g = jax.nn.sigmoid(gates.reshape(bsz, seq, 3, d))
        mix = (g[:, :, 0] * (y_a @ w_branch_a[l])
               + g[:, :, 1] * (y_b @ w_branch_b[l])
               + g[:, :, 2] * (y_c @ w_branch_c[l]))
        x = x + mix @ w_out[l]

        h2 = rmsnorm(x, g_ffn[l])
        x = x + peer_ffn(h2, w_peer_q[l], peer_sub_keys[l], peer_u[l], peer_v[l])
    return rmsnorm(x, g_final)
```

```python
import functools
import math

import numpy as np
import jax
import jax.numpy as jnp
from jax import lax
from jax.experimental import pallas as pl
from jax.experimental.pallas import tpu as pltpu

D_MODEL = 1024
DEPTH = 4
A_HEADS = 6
A_HEAD_DIM = 64
A_CONFIGS = ((128, 1), (512, 4), (2048, 16))
B_HEADS = 4
B_QK_DIM = 64
B_V_DIM = 128
C_HEADS = 6
C_NOPE_DIM = 64
C_ROPE_DIM = 32
C_V_DIM = 64
C_Q_RANK = 256
C_KV_RANK = 128
ROPE_BASE = 10000.0
N_BUCKETS = 32
MAX_DISTANCE = 2048
PEER_HEADS = 8
PEER_KEYS = 128
PEER_EXPERTS = PEER_KEYS * PEER_KEYS
PEER_KEY_DIM = 128
PEER_TOPK = 16
RMS_EPS = 1e-6
NEG_INF = -1e30

A_COLS = 3 * A_HEADS * A_HEAD_DIM
B_QK_COLS = B_HEADS * 2 * B_QK_DIM
B_V_COLS = B_HEADS * B_V_DIM

LANES = 128
SUBLANES = 8
VMEM_LIMIT = 56 * 1024 * 1024

F32 = jnp.float32
BF16 = jnp.bfloat16

TOKEN_TILE = 512
PROJ_COL_TILE = 512
A_BLOCK = 128
A_BLOCKS_PER_STEP = 4
FLASH_TILE = 256
PEER_TOKEN_TILE = 512
PEER_EXPERT_TILE = 1024


def _nt_dot(a, b):
    return lax.dot_general(a, b, (((1,), (1,)), ((), ())), preferred_element_type=F32)


def _gelu(x):
    return 0.5 * x * (1.0 + lax.erf(x * math.sqrt(0.5)))


def _rms(x, g):
    return x * lax.rsqrt(jnp.mean(x * x, axis=-1, keepdims=True) + RMS_EPS) * g


def _norm_matmul_kernel(x_ref, g_ref, w_ref, o_ref, h_scr):
    @pl.when(pl.program_id(1) == 0)
    def _():
        h_scr[...] = _rms(x_ref[...], g_ref[...]).astype(h_scr.dtype)

    o_ref[...] = jnp.dot(h_scr[...], w_ref[...], preferred_element_type=F32).astype(o_ref.dtype)


def norm_matmul(x, g, w, out_dtype):
    t, k = x.shape
    n = w.shape[1]
    tm, tn = min(TOKEN_TILE, t), PROJ_COL_TILE
    return pl.pallas_call(
        _norm_matmul_kernel,
        grid=(t // tm, n // tn),
        in_specs=[pl.BlockSpec((tm, k), lambda i, j: (i, 0)),
                  pl.BlockSpec((1, k), lambda i, j: (0, 0)),
                  pl.BlockSpec((k, tn), lambda i, j: (0, j))],
        out_specs=pl.BlockSpec((tm, tn), lambda i, j: (i, j)),
        out_shape=jax.ShapeDtypeStruct((t, n), out_dtype),
        scratch_shapes=[pltpu.VMEM((tm, k), BF16)],
        compiler_params=pltpu.CompilerParams(dimension_semantics=("parallel", "arbitrary")),
        name="norm_matmul",
    )(x, g.reshape(1, k), w)


A_Q_WIDTH = A_HEADS * LANES
A_PAIRS = A_HEADS // 2
A_KV_WIDTH = 2 * A_PAIRS * LANES
A_OUT_WIDTH = A_PAIRS * LANES


def _local_attn_kernel(q_ref, kvc_ref, kvp_ref, bias_ref, o_ref, lse_ref, kv_scr, *, nblk):
    w = A_BLOCK
    first_class_block = pl.program_id(2) == 0
    kv_scr[0:w, :] = kvp_ref[0]
    kv_scr[w:, :] = kvc_ref[0]
    lane = lax.broadcasted_iota(jnp.int32, (w, LANES), 1)
    col = lax.broadcasted_iota(jnp.int32, (w, 2 * w), 1)
    no_prev = jnp.where(col < w, jnp.where(first_class_block, NEG_INF, 0.0), 0.0)
    for i in range(nblk):
        rows = slice(i * w, (i + 1) * w)
        for p in range(A_PAIRS):
            k_blk = kv_scr[i * w:(i + 2) * w, p * LANES:(p + 1) * LANES]
            v_blk = kv_scr[i * w:(i + 2) * w, (A_PAIRS + p) * LANES:(A_PAIRS + p + 1) * LANES]
            outs, lses = [], []
            for e in range(2):
                hd = 2 * p + e
                q = q_ref[0, rows, hd * LANES:(hd + 1) * LANES]
                s = _nt_dot(q, k_blk) + bias_ref[hd]
                if i == 0:
                    s = s + no_prev
                m = jnp.max(s, axis=-1, keepdims=True)
                pr = jnp.exp(s - m)
                l = jnp.sum(pr, axis=-1, keepdims=True)
                o = jnp.dot(pr.astype(BF16), v_blk, preferred_element_type=F32)
                outs.append(o / l)
                lses.append(m + jnp.log(l))
            o_ref[0, rows, p * LANES:(p + 1) * LANES] = jnp.where(lane < A_HEAD_DIM, outs[0], outs[1])
            lse_ref[0, rows, p * LANES:(p + 1) * LANES] = jnp.where(lane < A_HEAD_DIM, lses[0], lses[1])


def local_attn(qkv, bias, dilation):
    bsz, seq, width = qkv.shape
    d = dilation
    w = A_BLOCK
    rows = seq // d
    nblk = min(A_BLOCKS_PER_STEP, rows // w)
    assert rows % (w * nblk) == 0 and width == A_Q_WIDTH + A_KV_WIDTH and A_Q_WIDTH == A_KV_WIDTH
    view = qkv.reshape(bsz, rows, d * width)
    grid = (bsz, d, rows // (w * nblk))
    out_shape = jax.ShapeDtypeStruct((bsz, rows, d * A_OUT_WIDTH), F32)
    o, lse = pl.pallas_call(
        functools.partial(_local_attn_kernel, nblk=nblk),
        grid=grid,
        in_specs=[pl.BlockSpec((1, w * nblk, A_Q_WIDTH), lambda b, r, n: (b, n, 2 * r)),
                  pl.BlockSpec((1, w * nblk, A_KV_WIDTH), lambda b, r, n: (b, n, 2 * r + 1)),
                  pl.BlockSpec((1, w, A_KV_WIDTH), lambda b, r, n: (b, jnp.maximum(n * nblk - 1, 0), 2 * r + 1)),
                  pl.BlockSpec((A_HEADS, w, 2 * w), lambda b, r, n: (0, 0, 0))],
        out_specs=[pl.BlockSpec((1, w * nblk, A_OUT_WIDTH), lambda b, r, n: (b, n, r)),
                   pl.BlockSpec((1, w * nblk, A_OUT_WIDTH), lambda b, r, n: (b, n, r))],
        out_shape=[out_shape, out_shape],
        scratch_shapes=[pltpu.VMEM((w * (nblk + 1), A_KV_WIDTH), BF16)],
        compiler_params=pltpu.CompilerParams(dimension_semantics=("parallel", "parallel", "arbitrary")),
        name=f"local_attn_d{d}",
    )(view, view, view, bias)
    return o.reshape(bsz, seq, A_OUT_WIDTH), lse.reshape(bsz, seq, A_OUT_WIDTH)


def _flash_pair_kernel(*refs, tq, n_near, shared_k, mode):
    if shared_k:
        qa_ref, qb_ref, ka_ref, v_ref, bias_ref = refs[:5]
        kb_ref = ka_ref
        rest = refs[5:]
    else:
        qa_ref, qb_ref, ka_ref, kb_ref, v_ref, bias_ref = refs[:6]
        rest = refs[6:]
    if mode == "diff":
        prm_ref, gsub_ref, o_ref, m_scr, l_scr, acc_scr = rest
    else:
        o_ref, m_scr, l_scr, acc_scr = rest

    qi = pl.program_id(2)
    q = jnp.concatenate([qa_ref[0], qb_ref[0]], axis=0)
    m_scr[...] = jnp.full(m_scr.shape, NEG_INF, F32)
    l_scr[...] = jnp.zeros(l_scr.shape, F32)
    acc_scr[...] = jnp.zeros(acc_scr.shape, F32)

    def step(kt, bias_tile):
        koff = pl.multiple_of(kt * tq, tq)
        ka = ka_ref[0, pl.ds(koff, tq), :]
        if shared_k:
            s = _nt_dot(q, ka)
        else:
            kb = kb_ref[0, pl.ds(koff, tq), :]
            s = jnp.concatenate([_nt_dot(q[:tq], ka), _nt_dot(q[tq:], kb)], axis=0)
        if bias_tile is not None:
            s = s + jnp.concatenate([bias_tile, bias_tile], axis=0)
        m_prev = m_scr[...]
        m_new = jnp.maximum(m_prev, jnp.max(s, axis=-1, keepdims=True))
        alpha = jnp.exp(m_prev - m_new)
        pr = jnp.exp(s - m_new)
        l_scr[...] = alpha * l_scr[...] + jnp.sum(pr, axis=-1, keepdims=True)
        pv = jnp.dot(pr.astype(BF16), v_ref[0, pl.ds(koff, tq), :], preferred_element_type=F32)
        acc_scr[...] = alpha * acc_scr[...] + pv
        m_scr[...] = m_new

    n_far = jnp.maximum(qi - (n_near - 1), 0)

    def far_body(kt, carry):
        step(kt, None)
        return carry

    def near_body(kt, carry):
        step(kt, bias_ref[0, qi - kt])
        return carry

    lax.fori_loop(0, n_far, far_body, 0)
    lax.fori_loop(n_far, qi + 1, near_body, 0)

    o = acc_scr[...] / l_scr[...]
    oa, ob = o[:tq], o[tq:]
    if mode == "diff":
        prm = prm_ref[...]
        lam_init = prm[4:5, 0:1]
        lam = (jnp.exp(jnp.sum(prm[0:1] * prm[1:2], axis=-1, keepdims=True))
               - jnp.exp(jnp.sum(prm[2:3] * prm[3:4], axis=-1, keepdims=True)) + lam_init)
        d = oa - lam * ob
        o_ref[0] = (_rms(d, gsub_ref[...]) * (1.0 - lam_init)).astype(o_ref.dtype)
    else:
        lane = lax.broadcasted_iota(jnp.int32, oa.shape, 1)
        o_ref[0] = jnp.where(lane < C_V_DIM, oa, ob).astype(o_ref.dtype)


def _flash_pair_call(mode, q_arr, k_arr, v_arr, bias, groups, col_maps, out_cols, extra):
    bsz, seq, _ = q_arr.shape
    tq = min(FLASH_TILE, seq)
    n_near = bias.shape[1]
    qa_c, qb_c, ka_c, kb_c, v_c = col_maps
    shared_k = kb_c is None
    q_spec = lambda cm: pl.BlockSpec((1, tq, LANES), lambda b, g, i: (b, i, cm(g)))
    kv_spec = lambda cm: pl.BlockSpec((1, seq, LANES), lambda b, g, i: (b, 0, cm(g)))
    in_specs = [q_spec(qa_c), q_spec(qb_c), kv_spec(ka_c)]
    args = [q_arr, q_arr, k_arr]
    if not shared_k:
        in_specs.append(kv_spec(kb_c))
        args.append(k_arr)
    in_specs.append(kv_spec(v_c))
    args.append(v_arr)
    bias_group = (lambda g: g) if bias.shape[0] > 1 else (lambda g: 0)
    in_specs.append(pl.BlockSpec((1, n_near, tq, tq), lambda b, g, i: (bias_group(g), 0, 0, 0)))
    args.append(bias)
    for e in extra:
        in_specs.append(pl.BlockSpec(e.shape, lambda b, g, i: (0, 0)))
        args.append(e)
    return pl.pallas_call(
        functools.partial(_flash_pair_kernel, tq=tq, n_near=n_near, shared_k=shared_k, mode=mode),
        grid=(bsz, groups, seq // tq),
        in_specs=in_specs,
        out_specs=pl.BlockSpec((1, tq, LANES), lambda b, g, i: (b, i, g)),
        out_shape=jax.ShapeDtypeStruct((bsz, seq, out_cols), BF16),
        scratch_shapes=[pltpu.VMEM((2 * tq, 1), F32), pltpu.VMEM((2 * tq, 1), F32),
                        pltpu.VMEM((2 * tq, LANES), F32)],
        compiler_params=pltpu.CompilerParams(dimension_semantics=("parallel", "parallel", "arbitrary"),
                                             vmem_limit_bytes=VMEM_LIMIT),
        name=f"flash_pair_{mode}",
    )(*args)


B_Q_BLOCKS = 2 * B_HEADS
B_WIDTH_PADDED = (B_Q_BLOCKS + 2 * B_HEADS) * LANES


def diff_attn(qkv, bias, lam_prm, g_subln):
    return _flash_pair_call(
        "diff", qkv, qkv, qkv, bias, B_HEADS,
        (lambda g: 2 * g, lambda g: 2 * g + 1, lambda g: B_Q_BLOCKS + g, None,
         lambda g: B_Q_BLOCKS + B_HEADS + g),
        B_HEADS * B_V_DIM, [lam_prm, g_subln.reshape(1, B_V_DIM)])


def mla_attn(q, k, v, mask):
    return _flash_pair_call(
        "mla", q, k, v, mask, C_HEADS // 2,
        (lambda g: 2 * g, lambda g: 2 * g + 1, lambda g: 2 * g, lambda g: 2 * g + 1, lambda g: g),
        C_HEADS * C_V_DIM, [])


C_QK_WIDTH = C_HEADS * LANES
C_V_WIDTH = C_HEADS * C_V_DIM
C_LATENT_WIDTH = C_Q_RANK + C_KV_RANK + LANES
ROPE_SHIFT = LANES - C_ROPE_DIM


def _mla_prep_kernel(c_ref, gq_ref, gkv_ref, wq_ref, wk_ref, wv_ref, cos_ref, sin_ref, q_ref, k_ref, v_ref):
    c = c_ref[...]
    cos = cos_ref[...]
    sin = sin_ref[...]

    def rope(blk):
        return blk * cos + pltpu.roll(blk, ROPE_SHIFT, 1) * sin

    cq = _rms(c[:, :C_Q_RANK], gq_ref[...]).astype(BF16)
    ckv = _rms(c[:, C_Q_RANK:C_Q_RANK + C_KV_RANK], gkv_ref[...]).astype(BF16)
    q_all = jnp.dot(cq, wq_ref[...], preferred_element_type=F32)
    k_all = jnp.dot(ckv, wk_ref[...], preferred_element_type=F32)
    k_rope = rope(c[:, C_Q_RANK + C_KV_RANK:])
    scale = 1.0 / math.sqrt(C_NOPE_DIM + C_ROPE_DIM)
    for h in range(C_HEADS):
        cols = slice(h * LANES, (h + 1) * LANES)
        q_ref[:, cols] = (rope(q_all[:, cols]) * scale).astype(q_ref.dtype)
        k_ref[:, cols] = (k_all[:, cols] + k_rope).astype(k_ref.dtype)
    v_ref[...] = jnp.dot(ckv, wv_ref[...], preferred_element_type=F32).astype(v_ref.dtype)


def mla_prep(c_lat, c_col_block, g_cq, g_ckv, w_q, w_k, w_v, cos_t, sin_t, seq):
    t = c_lat.shape[0]
    tm = min(TOKEN_TILE, seq)
    per_seq = seq // tm
    full = lambda a: pl.BlockSpec(a.shape, lambda i: (0, 0))
    g_cq = g_cq.reshape(1, -1)
    g_ckv = g_ckv.reshape(1, -1)
    return pl.pallas_call(
        _mla_prep_kernel,
        grid=(t // tm,),
        in_specs=[pl.BlockSpec((tm, C_LATENT_WIDTH), lambda i: (i, c_col_block)),
                  full(g_cq), full(g_ckv), full(w_q), full(w_k), full(w_v),
                  pl.BlockSpec((tm, LANES), lambda i: (i % per_seq, 0)),
                  pl.BlockSpec((tm, LANES), lambda i: (i % per_seq, 0))],
        out_specs=[pl.BlockSpec((tm, C_QK_WIDTH), lambda i: (i, 0)),
                   pl.BlockSpec((tm, C_QK_WIDTH), lambda i: (i, 0)),
                   pl.BlockSpec((tm, C_V_WIDTH), lambda i: (i, 0))],
        out_shape=[jax.ShapeDtypeStruct((t, C_QK_WIDTH), BF16), jax.ShapeDtypeStruct((t, C_QK_WIDTH), BF16),
                   jax.ShapeDtypeStruct((t, C_V_WIDTH), BF16)],
        compiler_params=pltpu.CompilerParams(dimension_semantics=("parallel",)),
        name="mla_prep",
    )(c_lat, g_cq, g_ckv, w_q, w_k, w_v, cos_t, sin_t)


def _merge_proj_kernel(x_ref, ga_ref, gb_ref, gc_ref, o1_ref, o2_ref, o3_ref, l1_ref, l2_ref, l3_ref,
                       yb_ref, yc_ref, wa_ref, wb_ref, wc_ref, wo_ref, gffn_ref, wpq_ref,
                       xo_ref, h2_ref, qp_ref):
    l1, l2, l3 = l1_ref[...], l2_ref[...], l3_ref[...]
    m = jnp.maximum(jnp.maximum(l1, l2), l3)
    e1, e2, e3 = jnp.exp(l1 - m), jnp.exp(l2 - m), jnp.exp(l3 - m)
    den = e1 + e2 + e3
    ya = (e1 / den) * o1_ref[...] + (e2 / den) * o2_ref[...] + (e3 / den) * o3_ref[...]
    pa = jnp.dot(ya.astype(BF16), wa_ref[...], preferred_element_type=F32)
    pb = jnp.dot(yb_ref[...], wb_ref[...], preferred_element_type=F32)
    pc = jnp.dot(yc_ref[...], wc_ref[...], preferred_element_type=F32)
    mix = (jax.nn.sigmoid(ga_ref[...]) * pa + jax.nn.sigmoid(gb_ref[...]) * pb
           + jax.nn.sigmoid(gc_ref[...]) * pc)
    x_new = x_ref[...] + jnp.dot(mix.astype(BF16), wo_ref[...], preferred_element_type=F32)
    xo_ref[...] = x_new
    h2 = _rms(x_new, gffn_ref[...]).astype(BF16)
    h2_ref[...] = h2
    qp_ref[...] = jnp.dot(h2, wpq_ref[...], preferred_element_type=F32)


def merge_proj(x, gc_arr, a_outs, a_lses, y_b, y_c, w_a, w_b, w_c, w_o, g_ffn, w_pq):
    t, d = x.shape
    tm = min(TOKEN_TILE, t)
    row = lambda width, col=0: pl.BlockSpec((tm, width), lambda i: (i, col))
    full = lambda a: pl.BlockSpec(a.shape, lambda i: (0, 0))
    g_ffn = g_ffn.reshape(1, d)
    in_specs = ([row(d)] + [row(d, c) for c in range(3)]
                + [row(A_OUT_WIDTH)] * 6 + [row(y_b.shape[1]), row(y_c.shape[1])]
                + [full(w_a), full(w_b), full(w_c), full(w_o), full(g_ffn), full(w_pq)])
    return pl.pallas_call(
        _merge_proj_kernel,
        grid=(t // tm,),
        in_specs=in_specs,
        out_specs=[row(d), row(d), row(d)],
        out_shape=[jax.ShapeDtypeStruct((t, d), F32), jax.ShapeDtypeStruct((t, d), BF16),
                   jax.ShapeDtypeStruct((t, d), F32)],
        compiler_params=pltpu.CompilerParams(dimension_semantics=("parallel",), vmem_limit_bytes=VMEM_LIMIT),
        name="merge_proj",
    )(x, gc_arr, gc_arr, gc_arr, *a_outs, *a_lses, y_b, y_c, w_a, w_b, w_c, w_o, g_ffn, w_pq)


def _oddeven_merge_sort_pairs(n):
    pairs = []
    p = 1
    while p < n:
        k = p
        while k >= 1:
            for j in range(k % p, n - k, 2 * k):
                for i in range(min(k, n - j - k)):
                    if (i + j) // (2 * p) == (i + j + k) // (2 * p):
                        pairs.append((i + j, i + j + k))
            k //= 2
        p *= 2
    return pairs


def _bitonic_merge_pairs(n):
    pairs = []
    stride = n // 2
    while stride >= 1:
        pairs += [(i, i + stride) for i in range(n) if not i & stride]
        stride //= 2
    return pairs


def _compare_exchange(vals, pairs):
    for i, j in pairs:
        a, b = vals[i], vals[j]
        if b is None:
            continue
        if a is None:
            vals[i], vals[j] = b, None
        else:
            vals[i], vals[j] = jnp.maximum(a, b), jnp.minimum(a, b)
    return vals


_SORT16 = _oddeven_merge_sort_pairs(PEER_TOPK)
_MERGE16 = _bitonic_merge_pairs(PEER_TOPK)
_CAND_PAIRS = [(a, b) for a in range(PEER_TOPK) for b in range(PEER_TOPK) if (a + 1) * (b + 1) <= PEER_TOPK]
_SORT64 = _oddeven_merge_sort_pairs(64)


def _top16_sorted(scores_t):
    slabs = [scores_t[v * SUBLANES:(v + 1) * SUBLANES, :] for v in range(PEER_KEYS // SUBLANES)]
    slabs = _compare_exchange(slabs, _SORT16)
    shift = SUBLANES // 2
    while shift >= 1:
        other = [pltpu.roll(s, shift, 0) for s in slabs]
        slabs = [jnp.maximum(slabs[r], other[PEER_TOPK - 1 - r]) for r in range(PEER_TOPK)]
        slabs = _compare_exchange(slabs, _MERGE16)
        shift //= 2
    return slabs


def _peer_select_kernel(qp_ref, keys_ref, s1_ref, s2_ref, c1_ref, e2_ref, thr_ref):
    tm = qp_ref.shape[0]
    sub = lax.broadcasted_iota(jnp.int32, (SUBLANES, tm), 0)
    tops = [[None] * PEER_TOPK for _ in range(2)]
    for h in range(PEER_HEADS):
        qh = qp_ref[:, h * LANES:(h + 1) * LANES].astype(BF16)
        for p in range(2):
            s_t = _nt_dot(keys_ref[2 * h + p], qh)
            (s1_ref if p == 0 else s2_ref)[h] = s_t
            top = _top16_sorted(s_t)
            for r in range(PEER_TOPK):
                tops[p][r] = top[r] if h == 0 else jnp.where(sub == h, top[r], tops[p][r])
    cand = [tops[0][a] + tops[1][b] for a, b in _CAND_PAIRS] + [None] * (64 - len(_CAND_PAIRS))
    best = _compare_exchange(cand, _SORT64)[:PEER_TOPK]
    mx = best[0]
    z = functools.reduce(lambda u, w_: u + w_, [jnp.exp(b - mx) for b in best])
    thr_ref[...] = best[PEER_TOPK - 1]
    m1, m2 = tops[0][0], tops[1][0]
    for h in range(PEER_HEADS):
        c1_ref[h] = jnp.exp(s1_ref[h] - m1[h:h + 1, :]) / z[h:h + 1, :]
        e2_ref[h] = jnp.exp(s2_ref[h] - m2[h:h + 1, :])


def peer_select(qp, keys_padded):
    t = qp.shape[0]
    tm = min(PEER_TOKEN_TILE, t)
    big = jax.ShapeDtypeStruct((PEER_HEADS, PEER_KEYS, t), F32)
    big_spec = pl.BlockSpec((PEER_HEADS, PEER_KEYS, tm), lambda i: (0, 0, i))
    return pl.pallas_call(
        _peer_select_kernel,
        grid=(t // tm,),
        in_specs=[pl.BlockSpec((tm, PEER_HEADS * PEER_KEY_DIM), lambda i: (i, 0)),
                  pl.BlockSpec(keys_padded.shape, lambda i: (0, 0, 0))],
        out_specs=[big_spec] * 4 + [pl.BlockSpec((PEER_HEADS, tm), lambda i: (0, i))],
        out_shape=[big] * 4 + [jax.ShapeDtypeStruct((PEER_HEADS, t), F32)],
        compiler_params=pltpu.CompilerParams(dimension_semantics=("parallel",), vmem_limit_bytes=VMEM_LIMIT),
        name="peer_select",
    )(qp, keys_padded)


def _peer_dense_kernel(h_ref, u_ref, vt_ref, s1_ref, s2_ref, c1_ref, e2_ref, thr_ref, x_ref, g_ref,
                       o_ref, acc_scr, act_scr, wa_scr, *, final_norm):
    j = pl.program_id(1)
    tm = h_ref.shape[0]
    ec = u_ref.shape[0]
    rows_per_tile = ec // PEER_KEYS

    @pl.when(j == 0)
    def _():
        acc_scr[...] = jnp.zeros(acc_scr.shape, F32)

    act_scr[...] = _gelu(_nt_dot(u_ref[...], h_ref[...]))

    assert rows_per_tile == SUBLANES
    key_rows = pl.ds(pl.multiple_of(j * rows_per_tile, SUBLANES), SUBLANES)

    def build(tc, carry):
        cols = pl.ds(pl.multiple_of(tc * LANES, LANES), LANES)
        thr = thr_ref[:, cols]
        s1_grp = [s1_ref[h, key_rows, cols] for h in range(PEER_HEADS)]
        c1_grp = [c1_ref[h, key_rows, cols] for h in range(PEER_HEADS)]
        for ii in range(rows_per_tile):
            rows = slice(ii * PEER_KEYS, (ii + 1) * PEER_KEYS)
            w = jnp.zeros((PEER_KEYS, LANES), F32)
            for h in range(PEER_HEADS):
                pair_score = s2_ref[h, :, cols] + s1_grp[h][ii:ii + 1, :]
                gate = e2_ref[h, :, cols] * c1_grp[h][ii:ii + 1, :]
                w = w + jnp.where(pair_score >= thr[h:h + 1, :], gate, 0.0)
            wa_scr[rows, cols] = (w * act_scr[rows, cols]).astype(BF16)
        return carry

    lax.fori_loop(0, tm // LANES, build, 0)
    acc_scr[...] += jnp.dot(vt_ref[...], wa_scr[...], preferred_element_type=F32)

    @pl.when(j == pl.num_programs(1) - 1)
    def _():
        y = x_ref[...] + acc_scr[...].T
        if final_norm:
            y = _rms(y, g_ref[...])
        o_ref[...] = y


def peer_dense(h2, u_bf, vt_bf, s1, s2, c1, e2, thr, x, g_final, final_norm):
    t, d = x.shape
    n_exp = u_bf.shape[0]
    tm = min(PEER_TOKEN_TILE, t)
    ec = PEER_EXPERT_TILE
    sel_spec = pl.BlockSpec((PEER_HEADS, PEER_KEYS, tm), lambda i, j: (0, 0, i))
    return pl.pallas_call(
        functools.partial(_peer_dense_kernel, final_norm=final_norm),
        grid=(t // tm, n_exp // ec),
        in_specs=[pl.BlockSpec((tm, d), lambda i, j: (i, 0)),
                  pl.BlockSpec((ec, d), lambda i, j: (j, 0)),
                  pl.BlockSpec((d, ec), lambda i, j: (0, j)),
                  sel_spec, sel_spec, sel_spec, sel_spec,
                  pl.BlockSpec((PEER_HEADS, tm), lambda i, j: (0, i)),
                  pl.BlockSpec((tm, d), lambda i, j: (i, 0)),
                  pl.BlockSpec((1, d), lambda i, j: (0, 0))],
        out_specs=pl.BlockSpec((tm, d), lambda i, j: (i, 0)),
        out_shape=jax.ShapeDtypeStruct((t, d), F32),
        scratch_shapes=[pltpu.VMEM((d, tm), F32), pltpu.VMEM((ec, tm), F32), pltpu.VMEM((ec, tm), BF16)],
        compiler_params=pltpu.CompilerParams(dimension_semantics=("parallel", "arbitrary"),
                                             vmem_limit_bytes=VMEM_LIMIT),
        name="peer_dense",
    )(h2, u_bf, vt_bf, s1, s2, c1, e2, thr, x, g_final.reshape(1, d))


def _t5_bucket_table(max_dist):
    n = np.arange(max_dist + 1)
    max_exact = N_BUCKETS // 2
    nf = np.maximum(n, max_exact).astype(np.float32)
    large = max_exact + (np.log(nf / np.float32(max_exact)) / np.float32(math.log(MAX_DISTANCE / max_exact))
                         * np.float32(N_BUCKETS - max_exact)).astype(np.int32)
    large = np.minimum(large, N_BUCKETS - 1)
    return np.where(n < max_exact, n, large).astype(np.int32)


def _banded_bias(bias_heads, dilation):
    w = A_BLOCK
    rel = w + np.arange(w)[:, None] - np.arange(2 * w)[None, :]
    band = (rel >= 0) & (rel <= w)
    bucket = _t5_bucket_table(w * dilation)[np.clip(rel, 0, w) * dilation]
    bias = jnp.transpose(bias_heads[bucket], (2, 0, 1)).astype(F32)
    return jnp.where(band[None], bias, NEG_INF)


def _causal_bias_tiles(bias_heads, tile):
    n_near = -(-(MAX_DISTANCE - 1 + tile) // tile)
    dist = (np.arange(n_near)[:, None, None] * tile + np.arange(tile)[None, :, None]
            - np.arange(tile)[None, None, :])
    bucket = _t5_bucket_table(n_near * tile)[np.clip(dist, 0, None)]
    far = bias_heads[N_BUCKETS - 1]
    bias = jnp.transpose((bias_heads - far[None, :])[bucket], (3, 0, 1, 2)).astype(F32)
    return jnp.where((dist >= 0)[None], bias, NEG_INF)


def _causal_mask_tile(tile):
    dist = np.arange(tile)[:, None] - np.arange(tile)[None, :]
    return jnp.asarray(np.where(dist >= 0, 0.0, NEG_INF).astype(np.float32))[None, None]


def _rope_tables(seq):
    half = C_ROPE_DIM // 2
    inv = ROPE_BASE ** (-jnp.arange(half, dtype=F32) / half)
    ang = jnp.arange(seq, dtype=F32)[:, None] * inv[None, :]
    cos, sin = jnp.cos(ang), jnp.sin(ang)
    ones = jnp.ones((seq, C_NOPE_DIM), F32)
    zeros_n = jnp.zeros((seq, C_NOPE_DIM), F32)
    spare = jnp.zeros((seq, LANES - C_NOPE_DIM - C_ROPE_DIM), F32)
    cos_t = jnp.concatenate([ones, cos, cos, spare], axis=1)
    sin_t = jnp.concatenate([zeros_n, -sin, sin, spare], axis=1)
    return cos_t, sin_t


def _swap_halves(w):
    half = w.shape[-1] // 2
    return jnp.concatenate([w[..., half:], w[..., :half]], axis=-1)


def _prep_in_proj(w_in):
    d = w_in.shape[0]
    bounds = np.cumsum([A_COLS, B_QK_COLS, B_QK_COLS, B_V_COLS, C_Q_RANK, C_KV_RANK, C_ROPE_DIM])
    wa, wbq, wbk, wbv, wcq, wckv, wckr, wg = jnp.split(w_in, bounds, axis=1)
    zeros = lambda n: jnp.zeros((d, n), w_in.dtype)
    wa = wa.reshape(d, 3, A_HEADS, A_HEAD_DIM)
    qa = wa[:, 0] * (1.0 / math.sqrt(A_HEAD_DIM))
    q_blocks = []
    for h in range(A_HEADS):
        pad = zeros(A_HEAD_DIM)
        q_blocks += [qa[:, h], pad] if h % 2 == 0 else [pad, qa[:, h]]
    w_a = jnp.concatenate(q_blocks + [wa[:, 1].reshape(d, -1), wa[:, 2].reshape(d, -1)], axis=1)
    qb = wbq.reshape(d, B_HEADS, 2, B_QK_DIM) * (1.0 / math.sqrt(B_QK_DIM))
    q_blocks = []
    for h in range(B_HEADS):
        pad = zeros(B_QK_DIM)
        q_blocks += [qb[:, h, 0], pad, pad, qb[:, h, 1]]
    w_b = jnp.concatenate(q_blocks + [wbk, wbv], axis=1)
    w_gc = jnp.concatenate([wg, wcq, wckv, zeros(C_NOPE_DIM), wckr, _swap_halves(wckr)], axis=1)
    return w_a.astype(BF16), w_b.astype(BF16), w_gc.astype(BF16)


def _prep_mla_weights(w_uq, w_ukv):
    rq = w_uq.shape[0]
    wq = w_uq.reshape(rq, C_HEADS, C_NOPE_DIM + C_ROPE_DIM)
    q_rope = wq[..., C_NOPE_DIM:]
    wq = jnp.concatenate([wq, _swap_halves(q_rope)], axis=-1).reshape(rq, C_QK_WIDTH)
    rkv = w_ukv.shape[0]
    wkv = w_ukv.reshape(rkv, C_HEADS, C_NOPE_DIM + C_V_DIM)
    wk = jnp.concatenate([wkv[..., :C_NOPE_DIM], jnp.zeros((rkv, C_HEADS, LANES - C_NOPE_DIM), w_ukv.dtype)],
                         axis=-1).reshape(rkv, C_QK_WIDTH)
    wv = wkv[..., C_NOPE_DIM:].reshape(rkv, C_V_WIDTH)
    return wq.astype(BF16), wk.astype(BF16), wv.astype(BF16)


def _prep_peer_keys(sub_keys):
    half = PEER_KEY_DIM // 2
    z = jnp.zeros(sub_keys.shape[:1] + sub_keys.shape[2:], sub_keys.dtype)
    first = jnp.concatenate([sub_keys[:, 0], z], axis=-1)
    second = jnp.concatenate([z, sub_keys[:, 1]], axis=-1)
    assert first.shape[-1] == LANES and half * 2 == LANES
    return jnp.stack([first, second], axis=1).reshape(2 * PEER_HEADS, PEER_KEYS, LANES).astype(BF16)


def kernel(x, rel_bias, w_in, g_mix, w_uq, g_cq, w_ukv, g_ckv, lam_q1, lam_k1, lam_q2, lam_k2, g_subln,
           w_branch_a, w_branch_b, w_branch_c, w_out, g_ffn, w_peer_q, peer_sub_keys, peer_u, peer_v, g_final):
    bsz, seq, d = x.shape
    t = bsz * seq
    depth = w_in.shape[0]
    bias_a = rel_bias[:, :A_HEADS]
    bias_b = rel_bias[:, A_HEADS:]
    a_biases = [_banded_bias(bias_a, dil) for _, dil in A_CONFIGS]
    tq = min(FLASH_TILE, seq)
    b_bias = _causal_bias_tiles(bias_b, tq)
    c_mask = _causal_mask_tile(tq)
    cos_t, sin_t = _rope_tables(seq)
    gate_blocks = 3 * d // C_LATENT_WIDTH

    xf = x.reshape(t, d)
    for l in range(depth):
        w_a, w_b, w_gc = _prep_in_proj(w_in[l])
        qkv_a = norm_matmul(xf, g_mix[l], w_a, BF16).reshape(bsz, seq, -1)
        qkv_b = norm_matmul(xf, g_mix[l], w_b, BF16).reshape(bsz, seq, -1)
        gc = norm_matmul(xf, g_mix[l], w_gc, F32)

        a_outs, a_lses = [], []
        for (_, dil), bias in zip(A_CONFIGS, a_biases):
            o, lse = local_attn(qkv_a, bias, dil)
            a_outs.append(o.reshape(t, -1))
            a_lses.append(lse.reshape(t, -1))

        lam_init = 0.8 - 0.6 * math.exp(-0.3 * l)
        pad = lambda v: jnp.pad(v.astype(F32), (0, LANES - v.shape[0]))
        lam_prm = jnp.stack([pad(lam_q1[l]), pad(lam_k1[l]), pad(lam_q2[l]), pad(lam_k2[l]),
                             jnp.full((LANES,), lam_init, F32)]
                            + [jnp.zeros((LANES,), F32)] * (SUBLANES - 5))
        y_b = diff_attn(qkv_b, b_bias, lam_prm, g_subln[l]).reshape(t, -1)

        wq, wk, wv = _prep_mla_weights(w_uq[l], w_ukv[l])
        q_c, k_c, v_c = mla_prep(gc, gate_blocks, g_cq[l], g_ckv[l], wq, wk, wv, cos_t, sin_t, seq)
        y_c = mla_attn(q_c.reshape(bsz, seq, -1), k_c.reshape(bsz, seq, -1), v_c.reshape(bsz, seq, -1),
                       c_mask).reshape(t, -1)

        xf, h2, qp = merge_proj(xf, gc, a_outs, a_lses, y_b, y_c,
                                w_branch_a[l].astype(BF16), w_branch_b[l].astype(BF16),
                                w_branch_c[l].astype(BF16), w_out[l].astype(BF16), g_ffn[l],
                                w_peer_q[l].astype(BF16))

        s1, s2, c1, e2, thr = peer_select(qp, _prep_peer_keys(peer_sub_keys[l]))
        xf = peer_dense(h2, peer_u[l].astype(BF16), peer_v[l].T.astype(BF16), s1, s2, c1, e2, thr, xf,
                        g_final, final_norm=(l == depth - 1))
    return xf.reshape(bsz, seq, d)
```

```python
import functools
import math

import numpy as np
import jax
import jax.numpy as jnp
from jax import lax
from jax.experimental import pallas as pl
from jax.experimental.pallas import tpu as pltpu

D_MODEL = 1024
DEPTH = 4
A_HEADS = 6
A_HEAD_DIM = 64
A_CONFIGS = ((128, 1), (512, 4), (2048, 16))
B_HEADS = 4
B_QK_DIM = 64
B_V_DIM = 128
C_HEADS = 6
C_NOPE_DIM = 64
C_ROPE_DIM = 32
C_V_DIM = 64
C_Q_RANK = 256
C_KV_RANK = 128
ROPE_BASE = 10000.0
N_BUCKETS = 32
MAX_DISTANCE = 2048
PEER_HEADS = 8
PEER_KEYS = 128
PEER_EXPERTS = PEER_KEYS * PEER_KEYS
PEER_KEY_DIM = 128
PEER_TOPK = 16
RMS_EPS = 1e-6
NEG_INF = -1e30
LOG2E = math.log2(math.e)

A_COLS = 3 * A_HEADS * A_HEAD_DIM
B_QK_COLS = B_HEADS * 2 * B_QK_DIM
B_V_COLS = B_HEADS * B_V_DIM

LANES = 128
SUBLANES = 8
VMEM_LIMIT = 56 * 1024 * 1024

F32 = jnp.float32
BF16 = jnp.bfloat16

TOKEN_TILE = 512
PROJ_COL_TILE = 512
A_BLOCK = 128
A_BLOCKS_PER_STEP = 4
FLASH_TILE = 256
FLASH_ROW_BLOCK = 128
PEER_TOKEN_TILE = 512
PEER_EXPERT_TILE = 1024


def _nt_dot(a, b):
    return lax.dot_general(a, b, (((1,), (1,)), ((), ())), preferred_element_type=F32)


def _gelu(x):
    return 0.5 * x * (1.0 + lax.erf(x * math.sqrt(0.5)))


def _rms(x, g):
    return x * lax.rsqrt(jnp.mean(x * x, axis=-1, keepdims=True) + RMS_EPS) * g


def _norm_matmul_kernel(x_ref, g_ref, w_ref, o_ref, h_scr):
    @pl.when(pl.program_id(1) == 0)
    def _():
        h_scr[...] = _rms(x_ref[...], g_ref[...]).astype(h_scr.dtype)

    o_ref[...] = jnp.dot(h_scr[...], w_ref[...], preferred_element_type=F32).astype(o_ref.dtype)


def norm_matmul(x, g, w, out_dtype):
    t, k = x.shape
    n = w.shape[1]
    tm, tn = min(TOKEN_TILE, t), PROJ_COL_TILE
    return pl.pallas_call(
        _norm_matmul_kernel,
        grid=(t // tm, n // tn),
        in_specs=[pl.BlockSpec((tm, k), lambda i, j: (i, 0)),
                  pl.BlockSpec((1, k), lambda i, j: (0, 0)),
                  pl.BlockSpec((k, tn), lambda i, j: (0, j))],
        out_specs=pl.BlockSpec((tm, tn), lambda i, j: (i, j)),
        out_shape=jax.ShapeDtypeStruct((t, n), out_dtype),
        scratch_shapes=[pltpu.VMEM((tm, k), BF16)],
        compiler_params=pltpu.CompilerParams(dimension_semantics=("parallel", "arbitrary")),
        name="norm_matmul",
    )(x, g.reshape(1, k), w)


A_Q_WIDTH = A_HEADS * LANES
A_PAIRS = A_HEADS // 2
A_KV_WIDTH = 2 * A_PAIRS * LANES
A_OUT_WIDTH = A_PAIRS * LANES


def _local_attn_kernel(q_ref, kvc_ref, kvp_ref, bias_ref, o_ref, lse_ref, kv_scr, *, nblk):
    w = A_BLOCK
    first_class_block = pl.program_id(2) == 0
    kv_scr[0:w, :] = kvp_ref[0]
    kv_scr[w:, :] = kvc_ref[0]
    lane = lax.broadcasted_iota(jnp.int32, (w, LANES), 1)
    col = lax.broadcasted_iota(jnp.int32, (w, 2 * w), 1)
    no_prev = jnp.where(col < w, jnp.where(first_class_block, NEG_INF, 0.0), 0.0)
    for i in range(nblk):
        rows = slice(i * w, (i + 1) * w)
        for p in range(A_PAIRS):
            k_blk = kv_scr[i * w:(i + 2) * w, p * LANES:(p + 1) * LANES]
            v_blk = kv_scr[i * w:(i + 2) * w, (A_PAIRS + p) * LANES:(A_PAIRS + p + 1) * LANES]
            outs, lses = [], []
            for e in range(2):
                hd = 2 * p + e
                q = q_ref[0, rows, hd * LANES:(hd + 1) * LANES]
                s = _nt_dot(q, k_blk) + bias_ref[hd]
                if i == 0:
                    s = s + no_prev
                m = jnp.max(s, axis=-1, keepdims=True)
                pr = jnp.exp(s - m)
                l = jnp.sum(pr, axis=-1, keepdims=True)
                o = jnp.dot(pr.astype(BF16), v_blk, preferred_element_type=F32)
                outs.append(o / l)
                lses.append(m + jnp.log(l))
            o_ref[0, rows, p * LANES:(p + 1) * LANES] = jnp.where(lane < A_HEAD_DIM, outs[0], outs[1])
            lse_ref[0, rows, p * LANES:(p + 1) * LANES] = jnp.where(lane < A_HEAD_DIM, lses[0], lses[1])


def local_attn(qkv, bias, dilation):
    bsz, seq, width = qkv.shape
    d = dilation
    w = A_BLOCK
    rows = seq // d
    nblk = min(A_BLOCKS_PER_STEP, rows // w)
    assert rows % (w * nblk) == 0 and width == A_Q_WIDTH + A_KV_WIDTH and A_Q_WIDTH == A_KV_WIDTH
    view = qkv.reshape(bsz, rows, d * width)
    grid = (bsz, d, rows // (w * nblk))
    out_shape = jax.ShapeDtypeStruct((bsz, rows, d * A_OUT_WIDTH), F32)
    o, lse = pl.pallas_call(
        functools.partial(_local_attn_kernel, nblk=nblk),
        grid=grid,
        in_specs=[pl.BlockSpec((1, w * nblk, A_Q_WIDTH), lambda b, r, n: (b, n, 2 * r)),
                  pl.BlockSpec((1, w * nblk, A_KV_WIDTH), lambda b, r, n: (b, n, 2 * r + 1)),
                  pl.BlockSpec((1, w, A_KV_WIDTH), lambda b, r, n: (b, jnp.maximum(n * nblk - 1, 0), 2 * r + 1)),
                  pl.BlockSpec((A_HEADS, w, 2 * w), lambda b, r, n: (0, 0, 0))],
        out_specs=[pl.BlockSpec((1, w * nblk, A_OUT_WIDTH), lambda b, r, n: (b, n, r)),
                   pl.BlockSpec((1, w * nblk, A_OUT_WIDTH), lambda b, r, n: (b, n, r))],
        out_shape=[out_shape, out_shape],
        scratch_shapes=[pltpu.VMEM((w * (nblk + 1), A_KV_WIDTH), BF16)],
        compiler_params=pltpu.CompilerParams(dimension_semantics=("parallel", "parallel", "arbitrary")),
        name=f"local_attn_d{d}",
    )(view, view, view, bias)
    return o.reshape(bsz, seq, A_OUT_WIDTH), lse.reshape(bsz, seq, A_OUT_WIDTH)


def _flash_pair_kernel(*refs, tq, n_near, shared_k, mode):
    if shared_k:
        qa_ref, qb_ref, ka_ref, v_ref, bias_ref = refs[:5]
        kb_ref = ka_ref
        rest = refs[5:]
    else:
        qa_ref, qb_ref, ka_ref, kb_ref, v_ref, bias_ref = refs[:6]
        rest = refs[6:]
    if mode == "diff":
        prm_ref, gsub_ref, o_ref, m_scr, alpha_scr, acc_scr, s0_scr, s1_scr, p_scr = rest
    else:
        o_ref, m_scr, alpha_scr, acc_scr, s0_scr, s1_scr, p_scr = rest

    qi = pl.program_id(2)
    rb = FLASH_ROW_BLOCK
    m_scr[...] = jnp.full(m_scr.shape, NEG_INF, F32)
    acc_scr[...] = jnp.zeros(acc_scr.shape, F32)
    ones = jnp.ones((tq, LANES), BF16)

    def scores(kt, s_scr):
        koff = pl.multiple_of(jnp.minimum(kt, qi) * tq, tq)
        ka = ka_ref[0, pl.ds(koff, tq), :]
        if shared_k:
            s_scr[...] = _nt_dot(jnp.concatenate([qa_ref[0], qb_ref[0]], axis=0), ka)
        else:
            s_scr[0:tq, :] = _nt_dot(qa_ref[0], ka)
            s_scr[tq:, :] = _nt_dot(qb_ref[0], kb_ref[0, pl.ds(koff, tq), :])

    def accumulate(kt, s_scr, near):
        koff = pl.multiple_of(jnp.minimum(kt, qi) * tq, tq)
        if near:
            behind = qi - kt
            bias_idx = jnp.where(behind < 0, n_near + 1, jnp.minimum(behind, n_near))
        for r in range(2 * tq // rb):
            rows = slice(r * rb, (r + 1) * rb)
            s = s_scr[rows, :]
            if near:
                b0 = (r * rb) % tq
                s = s + bias_ref[0, bias_idx, b0:b0 + rb, :]
            m_prev = m_scr[rows, :]
            m_new = jnp.maximum(m_prev, jnp.max(s, axis=-1, keepdims=True))
            alpha_scr[rows, :] = jnp.exp2(m_prev - m_new)
            m_scr[rows, :] = m_new
            p_scr[rows, :] = jnp.exp2(s - jnp.concatenate([m_new] * (tq // LANES), axis=1)).astype(BF16)
        v_ones = jnp.concatenate([v_ref[0, pl.ds(koff, tq), :], ones], axis=1)
        pv = jnp.dot(p_scr[...], v_ones, preferred_element_type=F32)
        alpha = alpha_scr[...]
        acc_scr[...] = jnp.concatenate([alpha, alpha], axis=1) * acc_scr[...] + pv

    far_pairs = jnp.maximum(qi - (n_near - 1), 0) // 2

    def pair(i, near):
        kt = 2 * i
        scores(kt + 1, s1_scr)
        accumulate(kt, s0_scr, near)
        scores(kt + 2, s0_scr)
        accumulate(kt + 1, s1_scr, near)

    def far_body(i, carry):
        pair(i, False)
        return carry

    def near_body(i, carry):
        pair(i, True)
        return carry

    scores(0, s0_scr)
    lax.fori_loop(0, far_pairs, far_body, 0)
    lax.fori_loop(far_pairs, (qi + 2) // 2, near_body, 0)

    o = acc_scr[:, :LANES] / acc_scr[:, LANES:]
    oa, ob = o[:tq], o[tq:]
    if mode == "diff":
        prm = prm_ref[...]
        lam_init = prm[4:5, 0:1]
        lam = (jnp.exp(jnp.sum(prm[0:1] * prm[1:2], axis=-1, keepdims=True))
               - jnp.exp(jnp.sum(prm[2:3] * prm[3:4], axis=-1, keepdims=True)) + lam_init)
        d = oa - lam * ob
        o_ref[0] = (_rms(d, gsub_ref[...]) * (1.0 - lam_init)).astype(o_ref.dtype)
    else:
        lane = lax.broadcasted_iota(jnp.int32, oa.shape, 1)
        o_ref[0] = jnp.where(lane < C_V_DIM, oa, ob).astype(o_ref.dtype)


def _flash_pair_call(mode, q_arr, k_arr, v_arr, bias, groups, col_maps, out_cols, extra):
    bsz, seq, _ = q_arr.shape
    tq = min(FLASH_TILE, seq)
    n_near = bias.shape[1] - 2
    qa_c, qb_c, ka_c, kb_c, v_c = col_maps
    shared_k = kb_c is None
    q_spec = lambda cm: pl.BlockSpec((1, tq, LANES), lambda b, g, i: (b, i, cm(g)))
    kv_spec = lambda cm: pl.BlockSpec((1, seq, LANES), lambda b, g, i: (b, 0, cm(g)))
    in_specs = [q_spec(qa_c), q_spec(qb_c), kv_spec(ka_c)]
    args = [q_arr, q_arr, k_arr]
    if not shared_k:
        in_specs.append(kv_spec(kb_c))
        args.append(k_arr)
    in_specs.append(kv_spec(v_c))
    args.append(v_arr)
    bias_group = (lambda g: g) if bias.shape[0] > 1 else (lambda g: 0)
    in_specs.append(pl.BlockSpec((1, n_near + 2, tq, tq), lambda b, g, i: (bias_group(g), 0, 0, 0)))
    args.append(bias)
    for e in extra:
        in_specs.append(pl.BlockSpec(e.shape, lambda b, g, i: (0, 0)))
        args.append(e)
    return pl.pallas_call(
        functools.partial(_flash_pair_kernel, tq=tq, n_near=n_near, shared_k=shared_k, mode=mode),
        grid=(bsz, groups, seq // tq),
        in_specs=in_specs,
        out_specs=pl.BlockSpec((1, tq, LANES), lambda b, g, i: (b, i, g)),
        out_shape=jax.ShapeDtypeStruct((bsz, seq, out_cols), BF16),
        scratch_shapes=[pltpu.VMEM((2 * tq, LANES), F32), pltpu.VMEM((2 * tq, LANES), F32),
                        pltpu.VMEM((2 * tq, 2 * LANES), F32), pltpu.VMEM((2 * tq, tq), F32),
                        pltpu.VMEM((2 * tq, tq), F32), pltpu.VMEM((2 * tq, tq), BF16)],
        compiler_params=pltpu.CompilerParams(dimension_semantics=("parallel", "parallel", "arbitrary"),
                                             vmem_limit_bytes=VMEM_LIMIT),
        name=f"flash_pair_{mode}",
    )(*args)


B_Q_BLOCKS = 2 * B_HEADS
B_WIDTH_PADDED = (B_Q_BLOCKS + 2 * B_HEADS) * LANES


def diff_attn(qkv, bias, lam_prm, g_subln):
    return _flash_pair_call(
        "diff", qkv, qkv, qkv, bias, B_HEADS,
        (lambda g: 2 * g, lambda g: 2 * g + 1, lambda g: B_Q_BLOCKS + g, None,
         lambda g: B_Q_BLOCKS + B_HEADS + g),
        B_HEADS * B_V_DIM, [lam_prm, g_subln.reshape(1, B_V_DIM)])


def mla_attn(q, k, v, mask):
    return _flash_pair_call(
        "mla", q, k, v, mask, C_HEADS // 2,
        (lambda g: 2 * g, lambda g: 2 * g + 1, lambda g: 2 * g, lambda g: 2 * g + 1, lambda g: g),
        C_HEADS * C_V_DIM, [])


C_QK_WIDTH = C_HEADS * LANES
C_V_WIDTH = C_HEADS * C_V_DIM
C_LATENT_WIDTH = C_Q_RANK + C_KV_RANK + LANES
ROPE_SHIFT = LANES - C_ROPE_DIM


def _mla_prep_kernel(c_ref, gq_ref, gkv_ref, wq_ref, wk_ref, wv_ref, cos_ref, sin_ref, q_ref, k_ref, v_ref):
    c = c_ref[...]
    cos = cos_ref[...]
    sin = sin_ref[...]

    def rope(blk):
        return blk * cos + pltpu.roll(blk, ROPE_SHIFT, 1) * sin

    cq = _rms(c[:, :C_Q_RANK], gq_ref[...]).astype(BF16)
    ckv = _rms(c[:, C_Q_RANK:C_Q_RANK + C_KV_RANK], gkv_ref[...]).astype(BF16)
    q_all = jnp.dot(cq, wq_ref[...], preferred_element_type=F32)
    k_all = jnp.dot(ckv, wk_ref[...], preferred_element_type=F32)
    k_rope = rope(c[:, C_Q_RANK + C_KV_RANK:])
    scale = LOG2E / math.sqrt(C_NOPE_DIM + C_ROPE_DIM)
    for h in range(C_HEADS):
        cols = slice(h * LANES, (h + 1) * LANES)
        q_ref[:, cols] = (rope(q_all[:, cols]) * scale).astype(q_ref.dtype)
        k_ref[:, cols] = (k_all[:, cols] + k_rope).astype(k_ref.dtype)
    v_ref[...] = jnp.dot(ckv, wv_ref[...], preferred_element_type=F32).astype(v_ref.dtype)


def mla_prep(c_lat, c_col_block, g_cq, g_ckv, w_q, w_k, w_v, cos_t, sin_t, seq):
    t = c_lat.shape[0]
    tm = min(TOKEN_TILE, seq)
    per_seq = seq // tm
    full = lambda a: pl.BlockSpec(a.shape, lambda i: (0, 0))
    g_cq = g_cq.reshape(1, -1)
    g_ckv = g_ckv.reshape(1, -1)
    return pl.pallas_call(
        _mla_prep_kernel,
        grid=(t // tm,),
        in_specs=[pl.BlockSpec((tm, C_LATENT_WIDTH), lambda i: (i, c_col_block)),
                  full(g_cq), full(g_ckv), full(w_q), full(w_k), full(w_v),
                  pl.BlockSpec((tm, LANES), lambda i: (i % per_seq, 0)),
                  pl.BlockSpec((tm, LANES), lambda i: (i % per_seq, 0))],
        out_specs=[pl.BlockSpec((tm, C_QK_WIDTH), lambda i: (i, 0)),
                   pl.BlockSpec((tm, C_QK_WIDTH), lambda i: (i, 0)),
                   pl.BlockSpec((tm, C_V_WIDTH), lambda i: (i, 0))],
        out_shape=[jax.ShapeDtypeStruct((t, C_QK_WIDTH), BF16), jax.ShapeDtypeStruct((t, C_QK_WIDTH), BF16),
                   jax.ShapeDtypeStruct((t, C_V_WIDTH), BF16)],
        compiler_params=pltpu.CompilerParams(dimension_semantics=("parallel",)),
        name="mla_prep",
    )(c_lat, g_cq, g_ckv, w_q, w_k, w_v, cos_t, sin_t)


def _merge_proj_kernel(x_ref, ga_ref, gb_ref, gc_ref, o1_ref, o2_ref, o3_ref, l1_ref, l2_ref, l3_ref,
                       yb_ref, yc_ref, wa_ref, wb_ref, wc_ref, wo_ref, gffn_ref, wpq_ref,
                       xo_ref, h2_ref, qp_ref):
    l1, l2, l3 = l1_ref[...], l2_ref[...], l3_ref[...]
    m = jnp.maximum(jnp.maximum(l1, l2), l3)
    e1, e2, e3 = jnp.exp(l1 - m), jnp.exp(l2 - m), jnp.exp(l3 - m)
    den = e1 + e2 + e3
    ya = (e1 / den) * o1_ref[...] + (e2 / den) * o2_ref[...] + (e3 / den) * o3_ref[...]
    pa = jnp.dot(ya.astype(BF16), wa_ref[...], preferred_element_type=F32)
    pb = jnp.dot(yb_ref[...], wb_ref[...], preferred_element_type=F32)
    pc = jnp.dot(yc_ref[...], wc_ref[...], preferred_element_type=F32)
    mix = (jax.nn.sigmoid(ga_ref[...]) * pa + jax.nn.sigmoid(gb_ref[...]) * pb
           + jax.nn.sigmoid(gc_ref[...]) * pc)
    x_new = x_ref[...] + jnp.dot(mix.astype(BF16), wo_ref[...], preferred_element_type=F32)
    xo_ref[...] = x_new
    h2 = _rms(x_new, gffn_ref[...]).astype(BF16)
    h2_ref[...] = h2
    qp_ref[...] = jnp.dot(h2, wpq_ref[...], preferred_element_type=F32)


def merge_proj(x, gc_arr, a_outs, a_lses, y_b, y_c, w_a, w_b, w_c, w_o, g_ffn, w_pq):
    t, d = x.shape
    tm = min(TOKEN_TILE, t)
    row = lambda width, col=0: pl.BlockSpec((tm, width), lambda i: (i, col))
    full = lambda a: pl.BlockSpec(a.shape, lambda i: (0, 0))
    g_ffn = g_ffn.reshape(1, d)
    in_specs = ([row(d)] + [row(d, c) for c in range(3)]
                + [row(A_OUT_WIDTH)] * 6 + [row(y_b.shape[1]), row(y_c.shape[1])]
                + [full(w_a), full(w_b), full(w_c), full(w_o), full(g_ffn), full(w_pq)])
    return pl.pallas_call(
        _merge_proj_kernel,
        grid=(t // tm,),
        in_specs=in_specs,
        out_specs=[row(d), row(d), row(d)],
        out_shape=[jax.ShapeDtypeStruct((t, d), F32), jax.ShapeDtypeStruct((t, d), BF16),
                   jax.ShapeDtypeStruct((t, d), F32)],
        compiler_params=pltpu.CompilerParams(dimension_semantics=("parallel",), vmem_limit_bytes=VMEM_LIMIT),
        name="merge_proj",
    )(x, gc_arr, gc_arr, gc_arr, *a_outs, *a_lses, y_b, y_c, w_a, w_b, w_c, w_o, g_ffn, w_pq)


def _oddeven_merge_sort_pairs(n):
    pairs = []
    p = 1
    while p < n:
        k = p
        while k >= 1:
            for j in range(k % p, n - k, 2 * k):
                for i in range(min(k, n - j - k)):
                    if (i + j) // (2 * p) == (i + j + k) // (2 * p):
                        pairs.append((i + j, i + j + k))
            k //= 2
        p *= 2
    return pairs


def _bitonic_merge_pairs(n):
    pairs = []
    stride = n // 2
    while stride >= 1:
        pairs += [(i, i + stride) for i in range(n) if not i & stride]
        stride //= 2
    return pairs


def _compare_exchange(vals, pairs):
    for i, j in pairs:
        a, b = vals[i], vals[j]
        if b is None:
            continue
        if a is None:
            vals[i], vals[j] = b, None
        else:
            vals[i], vals[j] = jnp.maximum(a, b), jnp.minimum(a, b)
    return vals


_SORT16 = _oddeven_merge_sort_pairs(PEER_TOPK)
_MERGE16 = _bitonic_merge_pairs(PEER_TOPK)
_CAND_PAIRS = [(a, b) for a in range(PEER_TOPK) for b in range(PEER_TOPK) if (a + 1) * (b + 1) <= PEER_TOPK]
_SORT64 = _oddeven_merge_sort_pairs(64)


def _top16_sorted(scores_t):
    slabs = [scores_t[v * SUBLANES:(v + 1) * SUBLANES, :] for v in range(PEER_KEYS // SUBLANES)]
    slabs = _compare_exchange(slabs, _SORT16)
    shift = SUBLANES // 2
    while shift >= 1:
        other = [pltpu.roll(s, shift, 0) for s in slabs]
        slabs = [jnp.maximum(slabs[r], other[PEER_TOPK - 1 - r]) for r in range(PEER_TOPK)]
        slabs = _compare_exchange(slabs, _MERGE16)
        shift //= 2
    return slabs


def _peer_select_kernel(qp_ref, keys_ref, s1_ref, s2_ref, c1_ref, e2_ref, thr_ref):
    tm = qp_ref.shape[0]
    sub = lax.broadcasted_iota(jnp.int32, (SUBLANES, tm), 0)
    tops = [[None] * PEER_TOPK for _ in range(2)]
    for h in range(PEER_HEADS):
        qh = qp_ref[:, h * LANES:(h + 1) * LANES].astype(BF16)
        for p in range(2):
            s_t = _nt_dot(keys_ref[2 * h + p], qh)
            if p == 0:
                s1_ref[h] = s_t
            else:
                for c in range(tm // LANES):
                    s2_ref[h, c] = s_t[:, c * LANES:(c + 1) * LANES]
            top = _top16_sorted(s_t)
            for r in range(PEER_TOPK):
                tops[p][r] = top[r] if h == 0 else jnp.where(sub == h, top[r], tops[p][r])
    cand = [tops[0][a] + tops[1][b] for a, b in _CAND_PAIRS] + [None] * (64 - len(_CAND_PAIRS))
    best = _compare_exchange(cand, _SORT64)[:PEER_TOPK]
    mx = best[0]
    z = functools.reduce(lambda u, w_: u + w_, [jnp.exp(b - mx) for b in best])
    thr_ref[...] = best[PEER_TOPK - 1]
    m1, m2 = tops[0][0], tops[1][0]
    for h in range(PEER_HEADS):
        c1_ref[h] = jnp.exp(s1_ref[h] - m1[h:h + 1, :]) / z[h:h + 1, :]
        for c in range(tm // LANES):
            e2_ref[h, c] = jnp.exp(s2_ref[h, c] - m2[h:h + 1, c * LANES:(c + 1) * LANES])


def peer_select(qp, keys_padded):
    t = qp.shape[0]
    tm = min(PEER_TOKEN_TILE, t)
    first = jax.ShapeDtypeStruct((PEER_HEADS, PEER_KEYS, t), F32)
    first_spec = pl.BlockSpec((PEER_HEADS, PEER_KEYS, tm), lambda i: (0, 0, i))
    second = jax.ShapeDtypeStruct((PEER_HEADS, t // LANES, PEER_KEYS, LANES), F32)
    second_spec = pl.BlockSpec((PEER_HEADS, tm // LANES, PEER_KEYS, LANES), lambda i: (0, i, 0, 0))
    return pl.pallas_call(
        _peer_select_kernel,
        grid=(t // tm,),
        in_specs=[pl.BlockSpec((tm, PEER_HEADS * PEER_KEY_DIM), lambda i: (i, 0)),
                  pl.BlockSpec(keys_padded.shape, lambda i: (0, 0, 0))],
        out_specs=[first_spec, second_spec, first_spec, second_spec,
                   pl.BlockSpec((PEER_HEADS, tm), lambda i: (0, i))],
        out_shape=[first, second, first, second, jax.ShapeDtypeStruct((PEER_HEADS, t), F32)],
        compiler_params=pltpu.CompilerParams(dimension_semantics=("parallel",), vmem_limit_bytes=VMEM_LIMIT),
        name="peer_select",
    )(qp, keys_padded)


def _peer_dense_kernel(h_ref, u_ref, vt_ref, s1_ref, s2_ref, c1_ref, e2_ref, thr_ref, x_ref, g_ref,
                       o_ref, acc_scr, act_scr, wa_scr, *, final_norm):
    j = pl.program_id(1)
    tm = h_ref.shape[0]
    ec = u_ref.shape[0]
    rows_per_tile = ec // PEER_KEYS

    @pl.when(j == 0)
    def _():
        acc_scr[...] = jnp.zeros(acc_scr.shape, F32)

    n_chunks = tm // LANES
    act = _gelu(_nt_dot(u_ref[...], h_ref[...]))
    for tc in range(n_chunks):
        act_scr[tc] = act[:, tc * LANES:(tc + 1) * LANES]

    def build(tc, carry):
        cols = pl.ds(pl.multiple_of(tc * LANES, LANES), LANES)
        for ii in range(rows_per_tile):
            rows = slice(ii * PEER_KEYS, (ii + 1) * PEER_KEYS)
            w = jnp.zeros((PEER_KEYS, LANES), F32)
            for h in range(PEER_HEADS):
                pair_score = s2_ref[h, tc] + s1_ref[h, ii:ii + 1, cols]
                gate = e2_ref[h, tc] * c1_ref[h, ii:ii + 1, cols]
                w = w + jnp.where(pair_score >= thr_ref[h:h + 1, cols], gate, 0.0)
            wa_scr[tc, rows, :] = (w * act_scr[tc, rows, :]).astype(BF16)
        return carry

    lax.fori_loop(0, n_chunks, build, 0)
    wa = jnp.concatenate([wa_scr[tc] for tc in range(n_chunks)], axis=1)
    acc_scr[...] += jnp.dot(vt_ref[...], wa, preferred_element_type=F32)

    @pl.when(j == pl.num_programs(1) - 1)
    def _():
        y = x_ref[...] + acc_scr[...].T
        if final_norm:
            y = _rms(y, g_ref[...])
        o_ref[...] = y


def peer_dense(h2, u_bf, vt_bf, s1, s2, c1, e2, thr, x, g_final, final_norm):
    t, d = x.shape
    n_exp = u_bf.shape[0]
    tm = min(PEER_TOKEN_TILE, t)
    ec = PEER_EXPERT_TILE
    sel_spec = pl.BlockSpec((PEER_HEADS, tm // LANES, PEER_KEYS, LANES), lambda i, j: (0, i, 0, 0))
    key_spec = pl.BlockSpec((PEER_HEADS, ec // PEER_KEYS, tm), lambda i, j: (0, j, i))
    return pl.pallas_call(
        functools.partial(_peer_dense_kernel, final_norm=final_norm),
        grid=(t // tm, n_exp // ec),
        in_specs=[pl.BlockSpec((tm, d), lambda i, j: (i, 0)),
                  pl.BlockSpec((ec, d), lambda i, j: (j, 0)),
                  pl.BlockSpec((d, ec), lambda i, j: (0, j)),
                  key_spec, sel_spec, key_spec, sel_spec,
                  pl.BlockSpec((PEER_HEADS, tm), lambda i, j: (0, i)),
                  pl.BlockSpec((tm, d), lambda i, j: (i, 0)),
                  pl.BlockSpec((1, d), lambda i, j: (0, 0))],
        out_specs=pl.BlockSpec((tm, d), lambda i, j: (i, 0)),
        out_shape=jax.ShapeDtypeStruct((t, d), F32),
        scratch_shapes=[pltpu.VMEM((d, tm), F32), pltpu.VMEM((tm // LANES, ec, LANES), F32),
                        pltpu.VMEM((tm // LANES, ec, LANES), BF16)],
        compiler_params=pltpu.CompilerParams(dimension_semantics=("parallel", "arbitrary"),
                                             vmem_limit_bytes=VMEM_LIMIT),
        name="peer_dense",
    )(h2, u_bf, vt_bf, s1, s2, c1, e2, thr, x, g_final.reshape(1, d))


def _t5_bucket_table(max_dist):
    n = np.arange(max_dist + 1)
    max_exact = N_BUCKETS // 2
    nf = np.maximum(n, max_exact).astype(np.float32)
    large = max_exact + (np.log(nf / np.float32(max_exact)) / np.float32(math.log(MAX_DISTANCE / max_exact))
                         * np.float32(N_BUCKETS - max_exact)).astype(np.int32)
    large = np.minimum(large, N_BUCKETS - 1)
    return np.where(n < max_exact, n, large).astype(np.int32)


def _skew(vec, rows):
    h, length = vec.shape
    return jnp.tile(vec, (1, rows))[:, :rows * (length - 1)].reshape(h, rows, length - 1)


def _banded_bias(bias_heads, dilation):
    w = A_BLOCK
    heads = bias_heads.shape[1]
    by_rel = bias_heads[_t5_bucket_table(w * dilation)[np.arange(w + 1) * dilation]].T.astype(F32)
    vec = jnp.concatenate([by_rel, jnp.full((heads, 3 * w - (w + 1)), NEG_INF, F32)], axis=1)
    return jnp.transpose(_skew(vec, 2 * w)[:, :, w:2 * w], (0, 2, 1))


def _causal_bias_tiles(bias_heads, tile):
    n_near = -(-(MAX_DISTANCE - 1 + tile) // tile)
    heads = bias_heads.shape[1]
    far = bias_heads[N_BUCKETS - 1]
    by_dist = (bias_heads - far[None, :])[_t5_bucket_table(n_near * tile - 1)].T.astype(F32)
    vec = jnp.concatenate([by_dist, jnp.full((heads, tile), NEG_INF, F32)], axis=1)
    skew = _skew(vec, tile)[:, :, :n_near * tile].reshape(heads, tile, n_near, tile)
    return _with_filler_tiles(jnp.transpose(skew, (0, 2, 3, 1)) * LOG2E)


def _causal_mask_tile(tile):
    dist = np.arange(tile)[:, None] - np.arange(tile)[None, :]
    return _with_filler_tiles(jnp.asarray(np.where(dist >= 0, 0.0, NEG_INF).astype(np.float32))[None, None])


def _with_filler_tiles(tiles):
    shape = tiles.shape[:1] + (1,) + tiles.shape[2:]
    return jnp.concatenate([tiles, jnp.zeros(shape, F32), jnp.full(shape, NEG_INF, F32)], axis=1)


def _rope_tables(seq):
    half = C_ROPE_DIM // 2
    inv = ROPE_BASE ** (-jnp.arange(half, dtype=F32) / half)
    ang = jnp.arange(seq, dtype=F32)[:, None] * inv[None, :]
    cos, sin = jnp.cos(ang), jnp.sin(ang)
    ones = jnp.ones((seq, C_NOPE_DIM), F32)
    zeros_n = jnp.zeros((seq, C_NOPE_DIM), F32)
    spare = jnp.zeros((seq, LANES - C_NOPE_DIM - C_ROPE_DIM), F32)
    cos_t = jnp.concatenate([ones, cos, cos, spare], axis=1)
    sin_t = jnp.concatenate([zeros_n, -sin, sin, spare], axis=1)
    return cos_t, sin_t


def _swap_halves(w):
    half = w.shape[-1] // 2
    return jnp.concatenate([w[..., half:], w[..., :half]], axis=-1)


def _prep_in_proj(w_in):
    d = w_in.shape[0]
    bounds = np.cumsum([A_COLS, B_QK_COLS, B_QK_COLS, B_V_COLS, C_Q_RANK, C_KV_RANK, C_ROPE_DIM])
    wa, wbq, wbk, wbv, wcq, wckv, wckr, wg = jnp.split(w_in, bounds, axis=1)
    zeros = lambda n: jnp.zeros((d, n), w_in.dtype)
    wa = wa.reshape(d, 3, A_HEADS, A_HEAD_DIM)
    qa = wa[:, 0] * (1.0 / math.sqrt(A_HEAD_DIM))
    q_blocks = []
    for h in range(A_HEADS):
        pad = zeros(A_HEAD_DIM)
        q_blocks += [qa[:, h], pad] if h % 2 == 0 else [pad, qa[:, h]]
    w_a = jnp.concatenate(q_blocks + [wa[:, 1].reshape(d, -1), wa[:, 2].reshape(d, -1)], axis=1)
    qb = wbq.reshape(d, B_HEADS, 2, B_QK_DIM) * (LOG2E / math.sqrt(B_QK_DIM))
    q_blocks = []
    for h in range(B_HEADS):
        pad = zeros(B_QK_DIM)
        q_blocks += [qb[:, h, 0], pad, pad, qb[:, h, 1]]
    w_b = jnp.concatenate(q_blocks + [wbk, wbv], axis=1)
    w_gc = jnp.concatenate([wg, wcq, wckv, zeros(C_NOPE_DIM), wckr, _swap_halves(wckr)], axis=1)
    return w_a.astype(BF16), w_b.astype(BF16), w_gc.astype(BF16)


def _prep_mla_weights(w_uq, w_ukv):
    rq = w_uq.shape[0]
    wq = w_uq.reshape(rq, C_HEADS, C_NOPE_DIM + C_ROPE_DIM)
    q_rope = wq[..., C_NOPE_DIM:]
    wq = jnp.concatenate([wq, _swap_halves(q_rope)], axis=-1).reshape(rq, C_QK_WIDTH)
    rkv = w_ukv.shape[0]
    wkv = w_ukv.reshape(rkv, C_HEADS, C_NOPE_DIM + C_V_DIM)
    wk = jnp.concatenate([wkv[..., :C_NOPE_DIM], jnp.zeros((rkv, C_HEADS, LANES - C_NOPE_DIM), w_ukv.dtype)],
                         axis=-1).reshape(rkv, C_QK_WIDTH)
    wv = wkv[..., C_NOPE_DIM:].reshape(rkv, C_V_WIDTH)
    return wq.astype(BF16), wk.astype(BF16), wv.astype(BF16)


def _prep_peer_keys(sub_keys):
    half = PEER_KEY_DIM // 2
    z = jnp.zeros(sub_keys.shape[:1] + sub_keys.shape[2:], sub_keys.dtype)
    first = jnp.concatenate([sub_keys[:, 0], z], axis=-1)
    second = jnp.concatenate([z, sub_keys[:, 1]], axis=-1)
    assert first.shape[-1] == LANES and half * 2 == LANES
    return jnp.stack([first, second], axis=1).reshape(2 * PEER_HEADS, PEER_KEYS, LANES).astype(BF16)


def kernel(x, rel_bias, w_in, g_mix, w_uq, g_cq, w_ukv, g_ckv, lam_q1, lam_k1, lam_q2, lam_k2, g_subln,
           w_branch_a, w_branch_b, w_branch_c, w_out, g_ffn, w_peer_q, peer_sub_keys, peer_u, peer_v, g_final):
    bsz, seq, d = x.shape
    t = bsz * seq
    depth = w_in.shape[0]
    bias_a = rel_bias[:, :A_HEADS]
    bias_b = rel_bias[:, A_HEADS:]
    a_biases = [_banded_bias(bias_a, dil) for _, dil in A_CONFIGS]
    tq = min(FLASH_TILE, seq)
    b_bias = _causal_bias_tiles(bias_b, tq)
    c_mask = _causal_mask_tile(tq)
    cos_t, sin_t = _rope_tables(seq)
    gate_blocks = 3 * d // C_LATENT_WIDTH

    xf = x.reshape(t, d)
    for l in range(depth):
        w_a, w_b, w_gc = _prep_in_proj(w_in[l])
        qkv_a = norm_matmul(xf, g_mix[l], w_a, BF16).reshape(bsz, seq, -1)
        qkv_b = norm_matmul(xf, g_mix[l], w_b, BF16).reshape(bsz, seq, -1)
        gc = norm_matmul(xf, g_mix[l], w_gc, F32)

        a_outs, a_lses = [], []
        for (_, dil), bias in zip(A_CONFIGS, a_biases):
            o, lse = local_attn(qkv_a, bias, dil)
            a_outs.append(o.reshape(t, -1))
            a_lses.append(lse.reshape(t, -1))

        lam_init = 0.8 - 0.6 * math.exp(-0.3 * l)
        pad = lambda v: jnp.pad(v.astype(F32), (0, LANES - v.shape[0]))
        lam_prm = jnp.stack([pad(lam_q1[l]), pad(lam_k1[l]), pad(lam_q2[l]), pad(lam_k2[l]),
                             jnp.full((LANES,), lam_init, F32)]
                            + [jnp.zeros((LANES,), F32)] * (SUBLANES - 5))
        y_b = diff_attn(qkv_b, b_bias, lam_prm, g_subln[l]).reshape(t, -1)

        wq, wk, wv = _prep_mla_weights(w_uq[l], w_ukv[l])
        q_c, k_c, v_c = mla_prep(gc, gate_blocks, g_cq[l], g_ckv[l], wq, wk, wv, cos_t, sin_t, seq)
        y_c = mla_attn(q_c.reshape(bsz, seq, -1), k_c.reshape(bsz, seq, -1), v_c.reshape(bsz, seq, -1),
                       c_mask).reshape(t, -1)

        xf, h2, qp = merge_proj(xf, gc, a_outs, a_lses, y_b, y_c,
                                w_branch_a[l].astype(BF16), w_branch_b[l].astype(BF16),
                                w_branch_c[l].astype(BF16), w_out[l].astype(BF16), g_ffn[l],
                                w_peer_q[l].astype(BF16))

        s1, s2, c1, e2, thr = peer_select(qp, _prep_peer_keys(peer_sub_keys[l]))
        xf = peer_dense(h2, peer_u[l].astype(BF16), peer_v[l].T.astype(BF16), s1, s2, c1, e2, thr, xf,
                        g_final, final_norm=(l == depth - 1))
    return xf.reshape(bsz, seq, d)
```

```python
import functools
import math

import numpy as np
import jax
import jax.numpy as jnp
from jax import lax
from jax.experimental import pallas as pl
from jax.experimental.pallas import tpu as pltpu

D_MODEL = 1024
DEPTH = 4
A_HEADS = 6
A_HEAD_DIM = 64
A_CONFIGS = ((128, 1), (512, 4), (2048, 16))
B_HEADS = 4
B_QK_DIM = 64
B_V_DIM = 128
C_HEADS = 6
C_NOPE_DIM = 64
C_ROPE_DIM = 32
C_V_DIM = 64
C_Q_RANK = 256
C_KV_RANK = 128
ROPE_BASE = 10000.0
N_BUCKETS = 32
MAX_DISTANCE = 2048
PEER_HEADS = 8
PEER_KEYS = 128
PEER_EXPERTS = PEER_KEYS * PEER_KEYS
PEER_KEY_DIM = 128
PEER_TOPK = 16
RMS_EPS = 1e-6
NEG_INF = -1e30
LOG2E = math.log2(math.e)

A_COLS = 3 * A_HEADS * A_HEAD_DIM
B_QK_COLS = B_HEADS * 2 * B_QK_DIM
B_V_COLS = B_HEADS * B_V_DIM

LANES = 128
SUBLANES = 8
VMEM_LIMIT = 56 * 1024 * 1024

F32 = jnp.float32
BF16 = jnp.bfloat16

TOKEN_TILE = 512
PROJ_TOKEN_TILE = 2048
PROJ_COL_TILE = 512
A_BLOCK = 128
A_BLOCKS_PER_STEP = 4
FLASH_Q_TILE = 512
FLASH_K_TILE = 256
FLASH_ROW_BLOCK = 128
PEER_TOKEN_TILE = 512
PEER_EXPERT_TILE = 1024


def _nt_dot(a, b):
    return lax.dot_general(a, b, (((1,), (1,)), ((), ())), preferred_element_type=F32)


def _gelu(x):
    return 0.5 * x * (1.0 + lax.erf(x * math.sqrt(0.5)))


def _rms(x, g):
    return x * lax.rsqrt(jnp.mean(x * x, axis=-1, keepdims=True) + RMS_EPS) * g


def _norm_matmul_kernel(x_ref, g_ref, w_ref, o_ref, h_scr):
    @pl.when(pl.program_id(1) == 0)
    def _():
        h_scr[...] = _rms(x_ref[...], g_ref[...]).astype(h_scr.dtype)

    o_ref[...] = jnp.dot(h_scr[...], w_ref[...], preferred_element_type=F32).astype(o_ref.dtype)


def norm_matmul(x, g, w, out_dtype):
    t, k = x.shape
    n = w.shape[1]
    tm, tn = min(PROJ_TOKEN_TILE, t), PROJ_COL_TILE
    return pl.pallas_call(
        _norm_matmul_kernel,
        grid=(t // tm, n // tn),
        in_specs=[pl.BlockSpec((tm, k), lambda i, j: (i, 0)),
                  pl.BlockSpec((1, k), lambda i, j: (0, 0)),
                  pl.BlockSpec((k, tn), lambda i, j: (0, j))],
        out_specs=pl.BlockSpec((tm, tn), lambda i, j: (i, j)),
        out_shape=jax.ShapeDtypeStruct((t, n), out_dtype),
        scratch_shapes=[pltpu.VMEM((tm, k), BF16)],
        compiler_params=pltpu.CompilerParams(dimension_semantics=("parallel", "arbitrary"),
                                             vmem_limit_bytes=VMEM_LIMIT),
        name="norm_matmul",
    )(x, g.reshape(1, k), w)


A_Q_WIDTH = A_HEADS * LANES
A_PAIRS = A_HEADS // 2
A_KV_WIDTH = 2 * A_PAIRS * LANES
A_OUT_WIDTH = A_PAIRS * LANES


def _local_attn_kernel(q_ref, kvc_ref, kvp_ref, bias_ref, o_ref, lse_ref, kv_scr, *, nblk):
    w = A_BLOCK
    first_class_block = pl.program_id(2) == 0
    kv_scr[0:w, :] = kvp_ref[0]
    kv_scr[w:, :] = kvc_ref[0]
    lane = lax.broadcasted_iota(jnp.int32, (w, LANES), 1)
    col = lax.broadcasted_iota(jnp.int32, (w, 2 * w), 1)
    no_prev = jnp.where(col < w, jnp.where(first_class_block, NEG_INF, 0.0), 0.0)
    for i in range(nblk):
        rows = slice(i * w, (i + 1) * w)
        for p in range(A_PAIRS):
            k_blk = kv_scr[i * w:(i + 2) * w, p * LANES:(p + 1) * LANES]
            v_blk = kv_scr[i * w:(i + 2) * w, (A_PAIRS + p) * LANES:(A_PAIRS + p + 1) * LANES]
            outs, lses = [], []
            for e in range(2):
                hd = 2 * p + e
                q = q_ref[0, rows, hd * LANES:(hd + 1) * LANES]
                s = _nt_dot(q, k_blk) + bias_ref[hd]
                if i == 0:
                    s = s + no_prev
                m = jnp.max(s, axis=-1, keepdims=True)
                pr = jnp.exp(s - m)
                l = jnp.sum(pr, axis=-1, keepdims=True)
                o = jnp.dot(pr.astype(BF16), v_blk, preferred_element_type=F32)
                outs.append(o / l)
                lses.append(m + jnp.log(l))
            o_ref[0, rows, p * LANES:(p + 1) * LANES] = jnp.where(lane < A_HEAD_DIM, outs[0], outs[1])
            lse_ref[0, rows, p * LANES:(p + 1) * LANES] = jnp.where(lane < A_HEAD_DIM, lses[0], lses[1])


def local_attn(qkv, bias, dilation):
    bsz, seq, width = qkv.shape
    d = dilation
    w = A_BLOCK
    rows = seq // d
    nblk = min(A_BLOCKS_PER_STEP, rows // w)
    assert rows % (w * nblk) == 0 and width == A_Q_WIDTH + A_KV_WIDTH and A_Q_WIDTH == A_KV_WIDTH
    view = qkv.reshape(bsz, rows, d * width)
    grid = (bsz, d, rows // (w * nblk))
    out_shape = jax.ShapeDtypeStruct((bsz, rows, d * A_OUT_WIDTH), F32)
    o, lse = pl.pallas_call(
        functools.partial(_local_attn_kernel, nblk=nblk),
        grid=grid,
        in_specs=[pl.BlockSpec((1, w * nblk, A_Q_WIDTH), lambda b, r, n: (b, n, 2 * r)),
                  pl.BlockSpec((1, w * nblk, A_KV_WIDTH), lambda b, r, n: (b, n, 2 * r + 1)),
                  pl.BlockSpec((1, w, A_KV_WIDTH), lambda b, r, n: (b, jnp.maximum(n * nblk - 1, 0), 2 * r + 1)),
                  pl.BlockSpec((A_HEADS, w, 2 * w), lambda b, r, n: (0, 0, 0))],
        out_specs=[pl.BlockSpec((1, w * nblk, A_OUT_WIDTH), lambda b, r, n: (b, n, r)),
                   pl.BlockSpec((1, w * nblk, A_OUT_WIDTH), lambda b, r, n: (b, n, r))],
        out_shape=[out_shape, out_shape],
        scratch_shapes=[pltpu.VMEM((w * (nblk + 1), A_KV_WIDTH), BF16)],
        compiler_params=pltpu.CompilerParams(dimension_semantics=("parallel", "parallel", "arbitrary")),
        name=f"local_attn_d{d}",
    )(view, view, view, bias)
    return o.reshape(bsz, seq, A_OUT_WIDTH), lse.reshape(bsz, seq, A_OUT_WIDTH)


def _flash_pair_kernel(*refs, tq, tk, n_near, shared_k, mode):
    if shared_k:
        qa_ref, qb_ref, ka_ref, v_ref, bias_ref = refs[:5]
        kb_ref = ka_ref
        rest = refs[5:]
    else:
        qa_ref, qb_ref, ka_ref, kb_ref, v_ref, bias_ref = refs[:6]
        rest = refs[6:]
    if mode == "diff":
        prm_ref, gsub_ref, o_ref, m_scr, alpha_scr, acc_scr, s0_scr, s1_scr, p_scr = rest
    else:
        o_ref, m_scr, alpha_scr, acc_scr, s0_scr, s1_scr, p_scr = rest

    qi = pl.program_id(2)
    rb = FLASH_ROW_BLOCK
    m_scr[...] = jnp.full(m_scr.shape, NEG_INF, F32)
    acc_scr[...] = jnp.zeros(acc_scr.shape, F32)
    ones = jnp.ones((tk, LANES), BF16)
    last_kt = 2 * qi + 1

    def scores(kt, s_scr):
        koff = pl.multiple_of(jnp.minimum(kt, last_kt) * tk, tk)
        ka = ka_ref[0, pl.ds(koff, tk), :]
        if shared_k:
            s_scr[...] = _nt_dot(jnp.concatenate([qa_ref[0], qb_ref[0]], axis=0), ka)
        else:
            s_scr[0:tq, :] = _nt_dot(qa_ref[0], ka)
            s_scr[tq:, :] = _nt_dot(qb_ref[0], kb_ref[0, pl.ds(koff, tk), :])

    def accumulate(kt, s_scr, near):
        koff = pl.multiple_of(kt * tk, tk)
        for r in range(2 * tq // rb):
            rows = slice(r * rb, (r + 1) * rb)
            s = s_scr[rows, :]
            if near:
                b0 = (r * rb) % tq
                s = s + bias_ref[0, last_kt - kt, b0:b0 + rb, :]
            m_prev = m_scr[rows, :]
            m_new = jnp.maximum(m_prev, jnp.max(s, axis=-1, keepdims=True))
            alpha_scr[rows, :] = jnp.exp2(m_prev - m_new)
            m_scr[rows, :] = m_new
            p_scr[rows, :] = jnp.exp2(s - jnp.concatenate([m_new] * (tk // LANES), axis=1)).astype(BF16)
        v_ones = jnp.concatenate([v_ref[0, pl.ds(koff, tk), :], ones], axis=1)
        pv = jnp.dot(p_scr[...], v_ones, preferred_element_type=F32)
        alpha = alpha_scr[...]
        acc_scr[...] = jnp.concatenate([alpha, alpha], axis=1) * acc_scr[...] + pv

    far_pairs = jnp.maximum(qi + 1 - n_near // 2, 0)

    def pair(i, near):
        kt = 2 * i
        scores(kt + 1, s1_scr)
        accumulate(kt, s0_scr, near)
        scores(kt + 2, s0_scr)
        accumulate(kt + 1, s1_scr, near)

    def far_body(i, carry):
        pair(i, False)
        return carry

    def near_body(i, carry):
        pair(i, True)
        return carry

    scores(0, s0_scr)
    lax.fori_loop(0, far_pairs, far_body, 0)
    lax.fori_loop(far_pairs, qi + 1, near_body, 0)

    o = acc_scr[:, :LANES] / acc_scr[:, LANES:]
    oa, ob = o[:tq], o[tq:]
    if mode == "diff":
        prm = prm_ref[...]
        lam_init = prm[4:5, 0:1]
        lam = (jnp.exp(jnp.sum(prm[0:1] * prm[1:2], axis=-1, keepdims=True))
               - jnp.exp(jnp.sum(prm[2:3] * prm[3:4], axis=-1, keepdims=True)) + lam_init)
        d = oa - lam * ob
        o_ref[0] = (_rms(d, gsub_ref[...]) * (1.0 - lam_init)).astype(o_ref.dtype)
    else:
        lane = lax.broadcasted_iota(jnp.int32, oa.shape, 1)
        o_ref[0] = jnp.where(lane < C_V_DIM, oa, ob).astype(o_ref.dtype)


def _flash_pair_call(mode, q_arr, k_arr, v_arr, bias, groups, col_maps, out_cols, extra):
    bsz, seq, _ = q_arr.shape
    tq, tk = FLASH_Q_TILE, FLASH_K_TILE
    assert seq % tq == 0 and tq == 2 * tk
    n_near = bias.shape[1]
    assert n_near % 2 == 0
    qa_c, qb_c, ka_c, kb_c, v_c = col_maps
    shared_k = kb_c is None
    q_spec = lambda cm: pl.BlockSpec((1, tq, LANES), lambda b, g, i: (b, i, cm(g)))
    kv_spec = lambda cm: pl.BlockSpec((1, seq, LANES), lambda b, g, i: (b, 0, cm(g)))
    in_specs = [q_spec(qa_c), q_spec(qb_c), kv_spec(ka_c)]
    args = [q_arr, q_arr, k_arr]
    if not shared_k:
        in_specs.append(kv_spec(kb_c))
        args.append(k_arr)
    in_specs.append(kv_spec(v_c))
    args.append(v_arr)
    bias_group = (lambda g: g) if bias.shape[0] > 1 else (lambda g: 0)
    in_specs.append(pl.BlockSpec((1, n_near, tq, tk), lambda b, g, i: (bias_group(g), 0, 0, 0)))
    args.append(bias)
    for e in extra:
        in_specs.append(pl.BlockSpec(e.shape, lambda b, g, i: (0, 0)))
        args.append(e)
    return pl.pallas_call(
        functools.partial(_flash_pair_kernel, tq=tq, tk=tk, n_near=n_near, shared_k=shared_k, mode=mode),
        grid=(bsz, groups, seq // tq),
        in_specs=in_specs,
        out_specs=pl.BlockSpec((1, tq, LANES), lambda b, g, i: (b, i, g)),
        out_shape=jax.ShapeDtypeStruct((bsz, seq, out_cols), BF16),
        scratch_shapes=[pltpu.VMEM((2 * tq, LANES), F32), pltpu.VMEM((2 * tq, LANES), F32),
                        pltpu.VMEM((2 * tq, 2 * LANES), F32), pltpu.VMEM((2 * tq, tk), F32),
                        pltpu.VMEM((2 * tq, tk), F32), pltpu.VMEM((2 * tq, tk), BF16)],
        compiler_params=pltpu.CompilerParams(dimension_semantics=("parallel", "parallel", "arbitrary"),
                                             vmem_limit_bytes=VMEM_LIMIT),
        name=f"flash_pair_{mode}",
    )(*args)


B_Q_BLOCKS = 2 * B_HEADS
B_WIDTH_PADDED = (B_Q_BLOCKS + 2 * B_HEADS) * LANES


def diff_attn(qkv, bias, lam_prm, g_subln):
    return _flash_pair_call(
        "diff", qkv, qkv, qkv, bias, B_HEADS,
        (lambda g: 2 * g, lambda g: 2 * g + 1, lambda g: B_Q_BLOCKS + g, None,
         lambda g: B_Q_BLOCKS + B_HEADS + g),
        B_HEADS * B_V_DIM, [lam_prm, g_subln.reshape(1, B_V_DIM)])


def mla_attn(q, k, v, mask):
    return _flash_pair_call(
        "mla", q, k, v, mask, C_HEADS // 2,
        (lambda g: 2 * g, lambda g: 2 * g + 1, lambda g: 2 * g, lambda g: 2 * g + 1, lambda g: g),
        C_HEADS * C_V_DIM, [])


C_QK_WIDTH = C_HEADS * LANES
C_V_WIDTH = C_HEADS * C_V_DIM
C_LATENT_WIDTH = C_Q_RANK + C_KV_RANK + LANES
ROPE_SHIFT = LANES - C_ROPE_DIM


def _mla_prep_kernel(c_ref, gq_ref, gkv_ref, wq_ref, wk_ref, wv_ref, cos_ref, sin_ref, q_ref, k_ref, v_ref):
    c = c_ref[...]
    cos = cos_ref[...]
    sin = sin_ref[...]

    def rope(blk):
        return blk * cos + pltpu.roll(blk, ROPE_SHIFT, 1) * sin

    cq = _rms(c[:, :C_Q_RANK], gq_ref[...]).astype(BF16)
    ckv = _rms(c[:, C_Q_RANK:C_Q_RANK + C_KV_RANK], gkv_ref[...]).astype(BF16)
    q_all = jnp.dot(cq, wq_ref[...], preferred_element_type=F32)
    k_all = jnp.dot(ckv, wk_ref[...], preferred_element_type=F32)
    k_rope = rope(c[:, C_Q_RANK + C_KV_RANK:])
    scale = LOG2E / math.sqrt(C_NOPE_DIM + C_ROPE_DIM)
    for h in range(C_HEADS):
        cols = slice(h * LANES, (h + 1) * LANES)
        q_ref[:, cols] = (rope(q_all[:, cols]) * scale).astype(q_ref.dtype)
        k_ref[:, cols] = (k_all[:, cols] + k_rope).astype(k_ref.dtype)
    v_ref[...] = jnp.dot(ckv, wv_ref[...], preferred_element_type=F32).astype(v_ref.dtype)


def mla_prep(c_lat, c_col_block, g_cq, g_ckv, w_q, w_k, w_v, cos_t, sin_t, seq):
    t = c_lat.shape[0]
    tm = min(TOKEN_TILE, seq)
    per_seq = seq // tm
    full = lambda a: pl.BlockSpec(a.shape, lambda i: (0, 0))
    g_cq = g_cq.reshape(1, -1)
    g_ckv = g_ckv.reshape(1, -1)
    return pl.pallas_call(
        _mla_prep_kernel,
        grid=(t // tm,),
        in_specs=[pl.BlockSpec((tm, C_LATENT_WIDTH), lambda i: (i, c_col_block)),
                  full(g_cq), full(g_ckv), full(w_q), full(w_k), full(w_v),
                  pl.BlockSpec((tm, LANES), lambda i: (i % per_seq, 0)),
                  pl.BlockSpec((tm, LANES), lambda i: (i % per_seq, 0))],
        out_specs=[pl.BlockSpec((tm, C_QK_WIDTH), lambda i: (i, 0)),
                   pl.BlockSpec((tm, C_QK_WIDTH), lambda i: (i, 0)),
                   pl.BlockSpec((tm, C_V_WIDTH), lambda i: (i, 0))],
        out_shape=[jax.ShapeDtypeStruct((t, C_QK_WIDTH), BF16), jax.ShapeDtypeStruct((t, C_QK_WIDTH), BF16),
                   jax.ShapeDtypeStruct((t, C_V_WIDTH), BF16)],
        compiler_params=pltpu.CompilerParams(dimension_semantics=("parallel",)),
        name="mla_prep",
    )(c_lat, g_cq, g_ckv, w_q, w_k, w_v, cos_t, sin_t)


def _merge_proj_kernel(x_ref, ga_ref, gb_ref, gc_ref, o1_ref, o2_ref, o3_ref, l1_ref, l2_ref, l3_ref,
                       yb_ref, yc_ref, wa_ref, wb_ref, wc_ref, wo_ref, gffn_ref, wpq_ref,
                       xo_ref, h2_ref, qp_ref):
    l1, l2, l3 = l1_ref[...], l2_ref[...], l3_ref[...]
    m = jnp.maximum(jnp.maximum(l1, l2), l3)
    e1, e2, e3 = jnp.exp(l1 - m), jnp.exp(l2 - m), jnp.exp(l3 - m)
    den = e1 + e2 + e3
    ya = (e1 / den) * o1_ref[...] + (e2 / den) * o2_ref[...] + (e3 / den) * o3_ref[...]
    pa = jnp.dot(ya.astype(BF16), wa_ref[...], preferred_element_type=F32)
    pb = jnp.dot(yb_ref[...], wb_ref[...], preferred_element_type=F32)
    pc = jnp.dot(yc_ref[...], wc_ref[...], preferred_element_type=F32)
    mix = (jax.nn.sigmoid(ga_ref[...]) * pa + jax.nn.sigmoid(gb_ref[...]) * pb
           + jax.nn.sigmoid(gc_ref[...]) * pc)
    x_new = x_ref[...] + jnp.dot(mix.astype(BF16), wo_ref[...], preferred_element_type=F32)
    xo_ref[...] = x_new
    h2 = _rms(x_new, gffn_ref[...]).astype(BF16)
    h2_ref[...] = h2
    qp_ref[...] = jnp.dot(h2, wpq_ref[...], preferred_element_type=F32)


def merge_proj(x, gc_arr, a_outs, a_lses, y_b, y_c, w_a, w_b, w_c, w_o, g_ffn, w_pq):
    t, d = x.shape
    tm = min(TOKEN_TILE, t)
    row = lambda width, col=0: pl.BlockSpec((tm, width), lambda i: (i, col))
    full = lambda a: pl.BlockSpec(a.shape, lambda i: (0, 0))
    g_ffn = g_ffn.reshape(1, d)
    in_specs = ([row(d)] + [row(d, c) for c in range(3)]
                + [row(A_OUT_WIDTH)] * 6 + [row(y_b.shape[1]), row(y_c.shape[1])]
                + [full(w_a), full(w_b), full(w_c), full(w_o), full(g_ffn), full(w_pq)])
    return pl.pallas_call(
        _merge_proj_kernel,
        grid=(t // tm,),
        in_specs=in_specs,
        out_specs=[row(d), row(d), row(d)],
        out_shape=[jax.ShapeDtypeStruct((t, d), F32), jax.ShapeDtypeStruct((t, d), BF16),
                   jax.ShapeDtypeStruct((t, d), F32)],
        compiler_params=pltpu.CompilerParams(dimension_semantics=("parallel",), vmem_limit_bytes=VMEM_LIMIT),
        name="merge_proj",
    )(x, gc_arr, gc_arr, gc_arr, *a_outs, *a_lses, y_b, y_c, w_a, w_b, w_c, w_o, g_ffn, w_pq)


def _oddeven_merge_sort_pairs(n):
    pairs = []
    p = 1
    while p < n:
        k = p
        while k >= 1:
            for j in range(k % p, n - k, 2 * k):
                for i in range(min(k, n - j - k)):
                    if (i + j) // (2 * p) == (i + j + k) // (2 * p):
                        pairs.append((i + j, i + j + k))
            k //= 2
        p *= 2
    return pairs


def _bitonic_merge_pairs(n):
    pairs = []
    stride = n // 2
    while stride >= 1:
        pairs += [(i, i + stride) for i in range(n) if not i & stride]
        stride //= 2
    return pairs


def _compare_exchange(vals, pairs):
    for i, j in pairs:
        a, b = vals[i], vals[j]
        if b is None:
            continue
        if a is None:
            vals[i], vals[j] = b, None
        else:
            vals[i], vals[j] = jnp.maximum(a, b), jnp.minimum(a, b)
    return vals


_SORT16 = _oddeven_merge_sort_pairs(PEER_TOPK)
_MERGE16 = _bitonic_merge_pairs(PEER_TOPK)
_CAND_PAIRS = [(a, b) for a in range(PEER_TOPK) for b in range(PEER_TOPK) if (a + 1) * (b + 1) <= PEER_TOPK]
_SORT64 = _oddeven_merge_sort_pairs(64)


def _top16_sorted(scores_t):
    slabs = [scores_t[v * SUBLANES:(v + 1) * SUBLANES, :] for v in range(PEER_KEYS // SUBLANES)]
    slabs = _compare_exchange(slabs, _SORT16)
    shift = SUBLANES // 2
    while shift >= 1:
        other = [pltpu.roll(s, shift, 0) for s in slabs]
        slabs = [jnp.maximum(slabs[r], other[PEER_TOPK - 1 - r]) for r in range(PEER_TOPK)]
        slabs = _compare_exchange(slabs, _MERGE16)
        shift //= 2
    return slabs


def _peer_select_kernel(qp_ref, keys_ref, s1_ref, s2_ref, c1_ref, e2_ref, thr_ref):
    tm = qp_ref.shape[0]
    sub = lax.broadcasted_iota(jnp.int32, (SUBLANES, tm), 0)
    tops = [[None] * PEER_TOPK for _ in range(2)]
    for h in range(PEER_HEADS):
        qh = qp_ref[:, h * LANES:(h + 1) * LANES].astype(BF16)
        for p in range(2):
            s_t = _nt_dot(keys_ref[2 * h + p], qh)
            if p == 0:
                s1_ref[h] = s_t
            else:
                for c in range(tm // LANES):
                    s2_ref[h, c] = s_t[:, c * LANES:(c + 1) * LANES]
            top = _top16_sorted(s_t)
            for r in range(PEER_TOPK):
                tops[p][r] = top[r] if h == 0 else jnp.where(sub == h, top[r], tops[p][r])
    cand = [tops[0][a] + tops[1][b] for a, b in _CAND_PAIRS] + [None] * (64 - len(_CAND_PAIRS))
    best = _compare_exchange(cand, _SORT64)[:PEER_TOPK]
    mx = best[0]
    z = functools.reduce(lambda u, w_: u + w_, [jnp.exp(b - mx) for b in best])
    thr_ref[...] = best[PEER_TOPK - 1]
    m1, m2 = tops[0][0], tops[1][0]
    for h in range(PEER_HEADS):
        c1_ref[h] = jnp.exp(s1_ref[h] - m1[h:h + 1, :]) / z[h:h + 1, :]
        for c in range(tm // LANES):
            e2_ref[h, c] = jnp.exp(s2_ref[h, c] - m2[h:h + 1, c * LANES:(c + 1) * LANES])


def peer_select(qp, keys_padded):
    t = qp.shape[0]
    tm = min(PEER_TOKEN_TILE, t)
    first = jax.ShapeDtypeStruct((PEER_HEADS, PEER_KEYS, t), F32)
    first_spec = pl.BlockSpec((PEER_HEADS, PEER_KEYS, tm), lambda i: (0, 0, i))
    second = jax.ShapeDtypeStruct((PEER_HEADS, t // LANES, PEER_KEYS, LANES), F32)
    second_spec = pl.BlockSpec((PEER_HEADS, tm // LANES, PEER_KEYS, LANES), lambda i: (0, i, 0, 0))
    return pl.pallas_call(
        _peer_select_kernel,
        grid=(t // tm,),
        in_specs=[pl.BlockSpec((tm, PEER_HEADS * PEER_KEY_DIM), lambda i: (i, 0)),
                  pl.BlockSpec(keys_padded.shape, lambda i: (0, 0, 0))],
        out_specs=[first_spec, second_spec, first_spec, second_spec,
                   pl.BlockSpec((PEER_HEADS, tm), lambda i: (0, i))],
        out_shape=[first, second, first, second, jax.ShapeDtypeStruct((PEER_HEADS, t), F32)],
        compiler_params=pltpu.CompilerParams(dimension_semantics=("parallel",), vmem_limit_bytes=VMEM_LIMIT),
        name="peer_select",
    )(qp, keys_padded)


def _peer_dense_kernel(h_ref, u_ref, vt_ref, s1_ref, s2_ref, c1_ref, e2_ref, thr_ref, x_ref, g_ref,
                       o_ref, acc_scr, act_scr, wa_scr, *, final_norm):
    j = pl.program_id(1)
    tm = h_ref.shape[0]
    ec = u_ref.shape[0]
    rows_per_tile = ec // PEER_KEYS

    @pl.when(j == 0)
    def _():
        acc_scr[...] = jnp.zeros(acc_scr.shape, F32)

    n_chunks = tm // LANES
    halves = [range(0, n_chunks // 2), range(n_chunks // 2, n_chunks)]
    for half in halves:
        rows = slice(half[0] * LANES, (half[-1] + 1) * LANES)
        act = _gelu(_nt_dot(u_ref[...], h_ref[rows, :]))
        for n, tc in enumerate(half):
            act_scr[tc] = act[:, n * LANES:(n + 1) * LANES]

    def build(tc, carry):
        cols = pl.ds(pl.multiple_of(tc * LANES, LANES), LANES)
        for ii in range(rows_per_tile):
            rows = slice(ii * PEER_KEYS, (ii + 1) * PEER_KEYS)
            w = jnp.zeros((PEER_KEYS, LANES), F32)
            for h in range(PEER_HEADS):
                pair_score = s2_ref[h, tc] + s1_ref[h, ii:ii + 1, cols]
                gate = e2_ref[h, tc] * c1_ref[h, ii:ii + 1, cols]
                w = w + jnp.where(pair_score >= thr_ref[h:h + 1, cols], gate, 0.0)
            wa_scr[tc, rows, :] = (w * act_scr[tc, rows, :]).astype(BF16)
        return carry

    lax.fori_loop(0, n_chunks, build, 0)
    for half in halves:
        cols = slice(half[0] * LANES, (half[-1] + 1) * LANES)
        wa = jnp.concatenate([wa_scr[tc] for tc in half], axis=1)
        acc_scr[:, cols] += jnp.dot(vt_ref[...], wa, preferred_element_type=F32)

    @pl.when(j == pl.num_programs(1) - 1)
    def _():
        y = x_ref[...] + acc_scr[...].T
        if final_norm:
            y = _rms(y, g_ref[...])
        o_ref[...] = y


def peer_dense(h2, u_bf, vt_bf, s1, s2, c1, e2, thr, x, g_final, final_norm):
    t, d = x.shape
    n_exp = u_bf.shape[0]
    tm = min(PEER_TOKEN_TILE, t)
    ec = PEER_EXPERT_TILE
    sel_spec = pl.BlockSpec((PEER_HEADS, tm // LANES, PEER_KEYS, LANES), lambda i, j: (0, i, 0, 0))
    key_spec = pl.BlockSpec((PEER_HEADS, ec // PEER_KEYS, tm), lambda i, j: (0, j, i))
    return pl.pallas_call(
        functools.partial(_peer_dense_kernel, final_norm=final_norm),
        grid=(t // tm, n_exp // ec),
        in_specs=[pl.BlockSpec((tm, d), lambda i, j: (i, 0)),
                  pl.BlockSpec((ec, d), lambda i, j: (j, 0)),
                  pl.BlockSpec((d, ec), lambda i, j: (0, j)),
                  key_spec, sel_spec, key_spec, sel_spec,
                  pl.BlockSpec((PEER_HEADS, tm), lambda i, j: (0, i)),
                  pl.BlockSpec((tm, d), lambda i, j: (i, 0)),
                  pl.BlockSpec((1, d), lambda i, j: (0, 0))],
        out_specs=pl.BlockSpec((tm, d), lambda i, j: (i, 0)),
        out_shape=jax.ShapeDtypeStruct((t, d), F32),
        scratch_shapes=[pltpu.VMEM((d, tm), F32), pltpu.VMEM((tm // LANES, ec, LANES), F32),
                        pltpu.VMEM((tm // LANES, ec, LANES), BF16)],
        compiler_params=pltpu.CompilerParams(dimension_semantics=("parallel", "arbitrary"),
                                             vmem_limit_bytes=VMEM_LIMIT),
        name="peer_dense",
    )(h2, u_bf, vt_bf, s1, s2, c1, e2, thr, x, g_final.reshape(1, d))


def _t5_bucket_table(max_dist):
    n = np.arange(max_dist + 1)
    max_exact = N_BUCKETS // 2
    nf = np.maximum(n, max_exact).astype(np.float32)
    large = max_exact + (np.log(nf / np.float32(max_exact)) / np.float32(math.log(MAX_DISTANCE / max_exact))
                         * np.float32(N_BUCKETS - max_exact)).astype(np.int32)
    large = np.minimum(large, N_BUCKETS - 1)
    return np.where(n < max_exact, n, large).astype(np.int32)


def _skew(vec, rows):
    h, length = vec.shape
    return jnp.tile(vec, (1, rows))[:, :rows * (length - 1)].reshape(h, rows, length - 1)


def _banded_bias(bias_heads, dilation):
    w = A_BLOCK
    heads = bias_heads.shape[1]
    by_rel = bias_heads[_t5_bucket_table(w * dilation)[np.arange(w + 1) * dilation]].T.astype(F32)
    vec = jnp.concatenate([by_rel, jnp.full((heads, 3 * w - (w + 1)), NEG_INF, F32)], axis=1)
    return jnp.transpose(_skew(vec, 2 * w)[:, :, w:2 * w], (0, 2, 1))


def _diagonal_tiles(by_dist, tq, tk, n_tiles):
    heads = by_dist.shape[0]
    n_cols = (n_tiles - 1) * tk + tq
    mask = jnp.full((heads, tk), NEG_INF, F32)
    vec = jnp.concatenate([mask, by_dist, mask], axis=1)
    assert vec.shape[1] == n_cols + tk
    skew = _skew(vec, tk)
    tiles = jnp.stack([skew[:, :, i * tk:i * tk + tq] for i in range(n_tiles)], axis=1)
    return jnp.transpose(tiles, (0, 1, 3, 2))


def _causal_bias_tiles(bias_heads, tq, tk):
    n_tiles = 2 * -(-(MAX_DISTANCE - 1 + tk + tk) // (2 * tk))
    far = bias_heads[N_BUCKETS - 1]
    n_dist = (n_tiles - 2) * tk + tq
    by_dist = (bias_heads - far[None, :])[_t5_bucket_table(n_dist - 1)].T.astype(F32) * LOG2E
    return _diagonal_tiles(by_dist, tq, tk, n_tiles)


def _causal_mask_tiles(tq, tk):
    return _diagonal_tiles(jnp.zeros((1, tq), F32), tq, tk, 2)


def _rope_tables(seq):
    half = C_ROPE_DIM // 2
    inv = ROPE_BASE ** (-jnp.arange(half, dtype=F32) / half)
    ang = jnp.arange(seq, dtype=F32)[:, None] * inv[None, :]
    cos, sin = jnp.cos(ang), jnp.sin(ang)
    ones = jnp.ones((seq, C_NOPE_DIM), F32)
    zeros_n = jnp.zeros((seq, C_NOPE_DIM), F32)
    spare = jnp.zeros((seq, LANES - C_NOPE_DIM - C_ROPE_DIM), F32)
    cos_t = jnp.concatenate([ones, cos, cos, spare], axis=1)
    sin_t = jnp.concatenate([zeros_n, -sin, sin, spare], axis=1)
    return cos_t, sin_t


def _swap_halves(w):
    half = w.shape[-1] // 2
    return jnp.concatenate([w[..., half:], w[..., :half]], axis=-1)


def _prep_in_proj(w_in):
    d = w_in.shape[0]
    bounds = np.cumsum([A_COLS, B_QK_COLS, B_QK_COLS, B_V_COLS, C_Q_RANK, C_KV_RANK, C_ROPE_DIM])
    wa, wbq, wbk, wbv, wcq, wckv, wckr, wg = jnp.split(w_in, bounds, axis=1)
    zeros = lambda n: jnp.zeros((d, n), w_in.dtype)
    wa = wa.reshape(d, 3, A_HEADS, A_HEAD_DIM)
    qa = wa[:, 0] * (1.0 / math.sqrt(A_HEAD_DIM))
    q_blocks = []
    for h in range(A_HEADS):
        pad = zeros(A_HEAD_DIM)
        q_blocks += [qa[:, h], pad] if h % 2 == 0 else [pad, qa[:, h]]
    w_a = jnp.concatenate(q_blocks + [wa[:, 1].reshape(d, -1), wa[:, 2].reshape(d, -1)], axis=1)
    qb = wbq.reshape(d, B_HEADS, 2, B_QK_DIM) * (LOG2E / math.sqrt(B_QK_DIM))
    q_blocks = []
    for h in range(B_HEADS):
        pad = zeros(B_QK_DIM)
        q_blocks += [qb[:, h, 0], pad, pad, qb[:, h, 1]]
    w_b = jnp.concatenate(q_blocks + [wbk, wbv], axis=1)
    w_gc = jnp.concatenate([wg, wcq, wckv, zeros(C_NOPE_DIM), wckr, _swap_halves(wckr)], axis=1)
    return w_a.astype(BF16), w_b.astype(BF16), w_gc.astype(BF16)


def _prep_mla_weights(w_uq, w_ukv):
    rq = w_uq.shape[0]
    wq = w_uq.reshape(rq, C_HEADS, C_NOPE_DIM + C_ROPE_DIM)
    q_rope = wq[..., C_NOPE_DIM:]
    wq = jnp.concatenate([wq, _swap_halves(q_rope)], axis=-1).reshape(rq, C_QK_WIDTH)
    rkv = w_ukv.shape[0]
    wkv = w_ukv.reshape(rkv, C_HEADS, C_NOPE_DIM + C_V_DIM)
    wk = jnp.concatenate([wkv[..., :C_NOPE_DIM], jnp.zeros((rkv, C_HEADS, LANES - C_NOPE_DIM), w_ukv.dtype)],
                         axis=-1).reshape(rkv, C_QK_WIDTH)
    wv = wkv[..., C_NOPE_DIM:].reshape(rkv, C_V_WIDTH)
    return wq.astype(BF16), wk.astype(BF16), wv.astype(BF16)


def _prep_peer_keys(sub_keys):
    half = PEER_KEY_DIM // 2
    z = jnp.zeros(sub_keys.shape[:1] + sub_keys.shape[2:], sub_keys.dtype)
    first = jnp.concatenate([sub_keys[:, 0], z], axis=-1)
    second = jnp.concatenate([z, sub_keys[:, 1]], axis=-1)
    assert first.shape[-1] == LANES and half * 2 == LANES
    return jnp.stack([first, second], axis=1).reshape(2 * PEER_HEADS, PEER_KEYS, LANES).astype(BF16)


def kernel(x, rel_bias, w_in, g_mix, w_uq, g_cq, w_ukv, g_ckv, lam_q1, lam_k1, lam_q2, lam_k2, g_subln,
           w_branch_a, w_branch_b, w_branch_c, w_out, g_ffn, w_peer_q, peer_sub_keys, peer_u, peer_v, g_final):
    bsz, seq, d = x.shape
    t = bsz * seq
    depth = w_in.shape[0]
    bias_a = rel_bias[:, :A_HEADS]
    bias_b = rel_bias[:, A_HEADS:]
    a_biases = [_banded_bias(bias_a, dil) for _, dil in A_CONFIGS]
    b_bias = _causal_bias_tiles(bias_b, FLASH_Q_TILE, FLASH_K_TILE)
    c_mask = _causal_mask_tiles(FLASH_Q_TILE, FLASH_K_TILE)
    cos_t, sin_t = _rope_tables(seq)
    gate_blocks = 3 * d // C_LATENT_WIDTH

    xf = x.reshape(t, d)
    for l in range(depth):
        w_a, w_b, w_gc = _prep_in_proj(w_in[l])
        qkv_a = norm_matmul(xf, g_mix[l], w_a, BF16).reshape(bsz, seq, -1)
        qkv_b = norm_matmul(xf, g_mix[l], w_b, BF16).reshape(bsz, seq, -1)
        gc = norm_matmul(xf, g_mix[l], w_gc, F32)

        a_outs, a_lses = [], []
        for (_, dil), bias in zip(A_CONFIGS, a_biases):
            o, lse = local_attn(qkv_a, bias, dil)
            a_outs.append(o.reshape(t, -1))
            a_lses.append(lse.reshape(t, -1))

        lam_init = 0.8 - 0.6 * math.exp(-0.3 * l)
        pad = lambda v: jnp.pad(v.astype(F32), (0, LANES - v.shape[0]))
        lam_prm = jnp.stack([pad(lam_q1[l]), pad(lam_k1[l]), pad(lam_q2[l]), pad(lam_k2[l]),
                             jnp.full((LANES,), lam_init, F32)]
                            + [jnp.zeros((LANES,), F32)] * (SUBLANES - 5))
        y_b = diff_attn(qkv_b, b_bias, lam_prm, g_subln[l]).reshape(t, -1)

        wq, wk, wv = _prep_mla_weights(w_uq[l], w_ukv[l])
        q_c, k_c, v_c = mla_prep(gc, gate_blocks, g_cq[l], g_ckv[l], wq, wk, wv, cos_t, sin_t, seq)
        y_c = mla_attn(q_c.reshape(bsz, seq, -1), k_c.reshape(bsz, seq, -1), v_c.reshape(bsz, seq, -1),
                       c_mask).reshape(t, -1)

        xf, h2, qp = merge_proj(xf, gc, a_outs, a_lses, y_b, y_c,
                                w_branch_a[l].astype(BF16), w_branch_b[l].astype(BF16),
                                w_branch_c[l].astype(BF16), w_out[l].astype(BF16), g_ffn[l],
                                w_peer_q[l].astype(BF16))

        s1, s2, c1, e2, thr = peer_select(qp, _prep_peer_keys(peer_sub_keys[l]))
        xf = peer_dense(h2, peer_u[l].astype(BF16), peer_v[l].T.astype(BF16), s1, s2, c1, e2, thr, xf,
                        g_final, final_norm=(l == depth - 1))
    return xf.reshape(bsz, seq, d)
```

```python
import functools
import math

import numpy as np
import jax
import jax.numpy as jnp
from jax import lax
from jax.experimental import pallas as pl
from jax.experimental.pallas import tpu as pltpu

D_MODEL = 1024
DEPTH = 4
A_HEADS = 6
A_HEAD_DIM = 64
A_CONFIGS = ((128, 1), (512, 4), (2048, 16))
B_HEADS = 4
B_QK_DIM = 64
B_V_DIM = 128
C_HEADS = 6
C_NOPE_DIM = 64
C_ROPE_DIM = 32
C_V_DIM = 64
C_Q_RANK = 256
C_KV_RANK = 128
ROPE_BASE = 10000.0
N_BUCKETS = 32
MAX_DISTANCE = 2048
PEER_HEADS = 8
PEER_KEYS = 128
PEER_EXPERTS = PEER_KEYS * PEER_KEYS
PEER_KEY_DIM = 128
PEER_TOPK = 16
RMS_EPS = 1e-6
NEG_INF = -1e30
LOG2E = math.log2(math.e)

A_COLS = 3 * A_HEADS * A_HEAD_DIM
B_QK_COLS = B_HEADS * 2 * B_QK_DIM
B_V_COLS = B_HEADS * B_V_DIM

LANES = 128
SUBLANES = 8
VMEM_LIMIT = 56 * 1024 * 1024

F32 = jnp.float32
BF16 = jnp.bfloat16

TOKEN_TILE = 512
PROJ_TOKEN_TILE = 2048
PROJ_COL_TILE = 512
A_BLOCK = 128
A_BLOCKS_PER_STEP = 4
FLASH_Q_TILE = 512
FLASH_K_TILE = 256
FLASH_ROW_BLOCK = 128
PEER_TOKEN_TILE = 512
PEER_EXPERT_TILE = 1024


def _nt_dot(a, b):
    return lax.dot_general(a, b, (((1,), (1,)), ((), ())), preferred_element_type=F32)


def _gelu(x):
    return 0.5 * x * (1.0 + lax.erf(x * math.sqrt(0.5)))


def _rms(x, g):
    return x * lax.rsqrt(jnp.mean(x * x, axis=-1, keepdims=True) + RMS_EPS) * g


def _norm_matmul_kernel(x_ref, g_ref, w_ref, o_ref, h_scr):
    @pl.when(pl.program_id(1) == 0)
    def _():
        h_scr[...] = _rms(x_ref[...], g_ref[...]).astype(h_scr.dtype)

    o_ref[...] = jnp.dot(h_scr[...], w_ref[...], preferred_element_type=F32).astype(o_ref.dtype)


def norm_matmul(x, g, w, out_dtype):
    t, k = x.shape
    n = w.shape[1]
    tm, tn = min(PROJ_TOKEN_TILE, t), PROJ_COL_TILE
    return pl.pallas_call(
        _norm_matmul_kernel,
        grid=(t // tm, n // tn),
        in_specs=[pl.BlockSpec((tm, k), lambda i, j: (i, 0)),
                  pl.BlockSpec((1, k), lambda i, j: (0, 0)),
                  pl.BlockSpec((k, tn), lambda i, j: (0, j))],
        out_specs=pl.BlockSpec((tm, tn), lambda i, j: (i, j)),
        out_shape=jax.ShapeDtypeStruct((t, n), out_dtype),
        scratch_shapes=[pltpu.VMEM((tm, k), BF16)],
        compiler_params=pltpu.CompilerParams(dimension_semantics=("parallel", "arbitrary"),
                                             vmem_limit_bytes=VMEM_LIMIT),
        name="norm_matmul",
    )(x, g.reshape(1, k), w)


def _norm_matmul_dilated_kernel(x_ref, g_ref, w_ref, *rest, dilations):
    out_refs, (h_scr, res_scr) = rest[:len(dilations)], rest[len(dilations):]
    @pl.when(pl.program_id(1) == 0)
    def _():
        h_scr[...] = _rms(x_ref[...], g_ref[...]).astype(h_scr.dtype)

    res = jnp.dot(h_scr[...], w_ref[...], preferred_element_type=F32)
    n_lane_blocks, tm, _ = res_scr.shape
    tn = n_lane_blocks * LANES
    for c in range(n_lane_blocks):
        res_scr[c] = res[:, c * LANES:(c + 1) * LANES]
    for d, o_ref in zip(dilations, out_refs):
        for r in range(d):
            for c in range(n_lane_blocks):
                cols = slice(r * tn + c * LANES, r * tn + (c + 1) * LANES)
                o_ref[:, cols] = res_scr[c, pl.ds(r, tm // d, stride=d), :].astype(o_ref.dtype)


def norm_matmul_dilated(x, g, w, dilations, tn):
    t, k = x.shape
    n = w.shape[1]
    tm = min(PROJ_TOKEN_TILE, t)
    assert n % tn == 0 and all(tm % (d * SUBLANES) == 0 for d in dilations)
    return pl.pallas_call(
        functools.partial(_norm_matmul_dilated_kernel, dilations=dilations),
        grid=(t // tm, n // tn),
        in_specs=[pl.BlockSpec((tm, k), lambda i, j: (i, 0)),
                  pl.BlockSpec((1, k), lambda i, j: (0, 0)),
                  pl.BlockSpec((k, tn), lambda i, j: (0, j))],
        out_specs=[pl.BlockSpec((tm // d, d * tn), lambda i, j: (i, j)) for d in dilations],
        out_shape=[jax.ShapeDtypeStruct((t // d, d * n), BF16) for d in dilations],
        scratch_shapes=[pltpu.VMEM((tm, k), BF16), pltpu.VMEM((tn // LANES, tm, LANES), F32)],
        compiler_params=pltpu.CompilerParams(dimension_semantics=("parallel", "arbitrary"),
                                             vmem_limit_bytes=VMEM_LIMIT),
        name="norm_matmul_dilated",
    )(x, g.reshape(1, k), w)


A_Q_WIDTH = A_HEADS * LANES
A_PAIRS = A_HEADS // 2
A_KV_WIDTH = 2 * A_PAIRS * LANES
A_OUT_WIDTH = A_PAIRS * LANES


def _local_attn_kernel(q_ref, kvc_ref, kvp_ref, bias_ref, o_ref, lse_ref, kv_scr, *, nblk):
    w = A_BLOCK
    first_class_block = pl.program_id(2) == 0
    kv_scr[0:w, :] = kvp_ref[0]
    kv_scr[w:, :] = kvc_ref[0]
    lane = lax.broadcasted_iota(jnp.int32, (w, LANES), 1)
    col = lax.broadcasted_iota(jnp.int32, (w, 2 * w), 1)
    no_prev = jnp.where(col < w, jnp.where(first_class_block, NEG_INF, 0.0), 0.0)
    for i in range(nblk):
        rows = slice(i * w, (i + 1) * w)
        for p in range(A_PAIRS):
            k_blk = kv_scr[i * w:(i + 2) * w, p * LANES:(p + 1) * LANES]
            v_blk = kv_scr[i * w:(i + 2) * w, (A_PAIRS + p) * LANES:(A_PAIRS + p + 1) * LANES]
            outs, lses = [], []
            for e in range(2):
                hd = 2 * p + e
                q = q_ref[0, rows, hd * LANES:(hd + 1) * LANES]
                s = _nt_dot(q, k_blk) + bias_ref[hd]
                if i == 0:
                    s = s + no_prev
                m = jnp.max(s, axis=-1, keepdims=True)
                pr = jnp.exp(s - m)
                l = jnp.sum(pr, axis=-1, keepdims=True)
                o = jnp.dot(pr.astype(BF16), v_blk, preferred_element_type=F32)
                outs.append(o / l)
                lses.append(m + jnp.log(l))
            o_ref[0, rows, p * LANES:(p + 1) * LANES] = jnp.where(lane < A_HEAD_DIM, outs[0], outs[1])
            lse_ref[0, rows, p * LANES:(p + 1) * LANES] = jnp.where(lane < A_HEAD_DIM, lses[0], lses[1])


def local_attn(view, bias, dilation):
    bsz, rows, width = view.shape
    d = dilation
    w = A_BLOCK
    nblk = min(A_BLOCKS_PER_STEP, rows // w)
    assert rows % (w * nblk) == 0 and width == d * (A_Q_WIDTH + A_KV_WIDTH) and A_Q_WIDTH == A_KV_WIDTH
    grid = (bsz, d, rows // (w * nblk))
    out_shape = jax.ShapeDtypeStruct((bsz, rows, d * A_OUT_WIDTH), F32)
    o, lse = pl.pallas_call(
        functools.partial(_local_attn_kernel, nblk=nblk),
        grid=grid,
        in_specs=[pl.BlockSpec((1, w * nblk, A_Q_WIDTH), lambda b, r, n: (b, n, r)),
                  pl.BlockSpec((1, w * nblk, A_KV_WIDTH), lambda b, r, n: (b, n, d + r)),
                  pl.BlockSpec((1, w, A_KV_WIDTH), lambda b, r, n: (b, jnp.maximum(n * nblk - 1, 0), d + r)),
                  pl.BlockSpec((A_HEADS, w, 2 * w), lambda b, r, n: (0, 0, 0))],
        out_specs=[pl.BlockSpec((1, w * nblk, A_OUT_WIDTH), lambda b, r, n: (b, n, r)),
                   pl.BlockSpec((1, w * nblk, A_OUT_WIDTH), lambda b, r, n: (b, n, r))],
        out_shape=[out_shape, out_shape],
        scratch_shapes=[pltpu.VMEM((w * (nblk + 1), A_KV_WIDTH), BF16)],
        compiler_params=pltpu.CompilerParams(dimension_semantics=("parallel", "parallel", "arbitrary")),
        name=f"local_attn_d{d}",
    )(view, view, view, bias)
    return o, lse


def _flash_pair_kernel(*refs, tq, tk, n_near, shared_k, mode):
    if shared_k:
        qa_ref, qb_ref, ka_ref, v_ref, bias_ref = refs[:5]
        kb_ref = ka_ref
        rest = refs[5:]
    else:
        qa_ref, qb_ref, ka_ref, kb_ref, v_ref, bias_ref = refs[:6]
        rest = refs[6:]
    if mode == "diff":
        prm_ref, gsub_ref, o_ref, m_scr, alpha_scr, acc_scr, s0_scr, s1_scr, p_scr = rest
    else:
        o_ref, m_scr, alpha_scr, acc_scr, s0_scr, s1_scr, p_scr = rest

    qi = pl.program_id(2)
    rb = FLASH_ROW_BLOCK
    m_scr[...] = jnp.full(m_scr.shape, NEG_INF, F32)
    acc_scr[...] = jnp.zeros(acc_scr.shape, F32)
    ones = jnp.ones((tk, LANES), BF16)
    last_kt = 2 * qi + 1

    def scores(kt, s_scr):
        koff = pl.multiple_of(jnp.minimum(kt, last_kt) * tk, tk)
        ka = ka_ref[0, pl.ds(koff, tk), :]
        if shared_k:
            s_scr[...] = _nt_dot(jnp.concatenate([qa_ref[0], qb_ref[0]], axis=0), ka)
        else:
            s_scr[0:tq, :] = _nt_dot(qa_ref[0], ka)
            s_scr[tq:, :] = _nt_dot(qb_ref[0], kb_ref[0, pl.ds(koff, tk), :])

    def accumulate(kt, s_scr, near):
        koff = pl.multiple_of(kt * tk, tk)
        for r in range(2 * tq // rb):
            rows = slice(r * rb, (r + 1) * rb)
            s = s_scr[rows, :]
            if near:
                b0 = (r * rb) % tq
                s = s + bias_ref[0, last_kt - kt, b0:b0 + rb, :]
            m_prev = m_scr[rows, :]
            m_new = jnp.maximum(m_prev, jnp.max(s, axis=-1, keepdims=True))
            alpha_scr[rows, :] = jnp.exp2(m_prev - m_new)
            m_scr[rows, :] = m_new
            p_scr[rows, :] = jnp.exp2(s - jnp.concatenate([m_new] * (tk // LANES), axis=1)).astype(BF16)
        v_ones = jnp.concatenate([v_ref[0, pl.ds(koff, tk), :], ones], axis=1)
        pv = jnp.dot(p_scr[...], v_ones, preferred_element_type=F32)
        alpha = alpha_scr[...]
        acc_scr[...] = jnp.concatenate([alpha, alpha], axis=1) * acc_scr[...] + pv

    n_tiles = 2 * qi + 2
    n_far = jnp.maximum(n_tiles - n_near, 0)

    def run(first, count, near):
        def unrolled(kt, steps):
            for step in range(steps):
                nxt, cur = (s1_scr, s0_scr) if step % 2 == 0 else (s0_scr, s1_scr)
                scores(kt + step + 1, nxt)
                accumulate(kt + step, cur, near)

        def quad(i, carry):
            unrolled(first + 4 * i, 4)
            return carry

        def pair(i, carry):
            unrolled(first + 4 * (count // 4) + 2 * i, 2)
            return carry

        lax.fori_loop(0, count // 4, quad, 0)
        lax.fori_loop(0, (count % 4) // 2, pair, 0)

    scores(0, s0_scr)
    run(0, n_far, False)
    run(n_far, n_tiles - n_far, True)

    o = acc_scr[:, :LANES] / acc_scr[:, LANES:]
    oa, ob = o[:tq], o[tq:]
    if mode == "diff":
        prm = prm_ref[...]
        lam_init = prm[4:5, 0:1]
        lam = (jnp.exp(jnp.sum(prm[0:1] * prm[1:2], axis=-1, keepdims=True))
               - jnp.exp(jnp.sum(prm[2:3] * prm[3:4], axis=-1, keepdims=True)) + lam_init)
        d = oa - lam * ob
        o_ref[0] = (_rms(d, gsub_ref[...]) * (1.0 - lam_init)).astype(o_ref.dtype)
    else:
        lane = lax.broadcasted_iota(jnp.int32, oa.shape, 1)
        o_ref[0] = jnp.where(lane < C_V_DIM, oa, ob).astype(o_ref.dtype)


def _flash_pair_call(mode, q_arr, k_arr, v_arr, bias, groups, col_maps, out_cols, extra):
    bsz, seq, _ = q_arr.shape
    tq, tk = FLASH_Q_TILE, FLASH_K_TILE
    assert seq % tq == 0 and tq == 2 * tk
    n_near = bias.shape[1]
    assert n_near % 2 == 0
    qa_c, qb_c, ka_c, kb_c, v_c = col_maps
    shared_k = kb_c is None
    q_spec = lambda cm: pl.BlockSpec((1, tq, LANES), lambda b, g, i: (b, i, cm(g)))
    kv_spec = lambda cm: pl.BlockSpec((1, seq, LANES), lambda b, g, i: (b, 0, cm(g)))
    in_specs = [q_spec(qa_c), q_spec(qb_c), kv_spec(ka_c)]
    args = [q_arr, q_arr, k_arr]
    if not shared_k:
        in_specs.append(kv_spec(kb_c))
        args.append(k_arr)
    in_specs.append(kv_spec(v_c))
    args.append(v_arr)
    bias_group = (lambda g: g) if bias.shape[0] > 1 else (lambda g: 0)
    in_specs.append(pl.BlockSpec((1, n_near, tq, tk), lambda b, g, i: (bias_group(g), 0, 0, 0)))
    args.append(bias)
    for e in extra:
        in_specs.append(pl.BlockSpec(e.shape, lambda b, g, i: (0, 0)))
        args.append(e)
    return pl.pallas_call(
        functools.partial(_flash_pair_kernel, tq=tq, tk=tk, n_near=n_near, shared_k=shared_k, mode=mode),
        grid=(bsz, groups, seq // tq),
        in_specs=in_specs,
        out_specs=pl.BlockSpec((1, tq, LANES), lambda b, g, i: (b, i, g)),
        out_shape=jax.ShapeDtypeStruct((bsz, seq, out_cols), BF16),
        scratch_shapes=[pltpu.VMEM((2 * tq, LANES), F32), pltpu.VMEM((2 * tq, LANES), F32),
                        pltpu.VMEM((2 * tq, 2 * LANES), F32), pltpu.VMEM((2 * tq, tk), F32),
                        pltpu.VMEM((2 * tq, tk), F32), pltpu.VMEM((2 * tq, tk), BF16)],
        compiler_params=pltpu.CompilerParams(dimension_semantics=("parallel", "parallel", "arbitrary"),
                                             vmem_limit_bytes=VMEM_LIMIT),
        name=f"flash_pair_{mode}",
    )(*args)


B_Q_BLOCKS = 2 * B_HEADS
B_WIDTH_PADDED = (B_Q_BLOCKS + 2 * B_HEADS) * LANES


def diff_attn(qkv, bias, lam_prm, g_subln):
    return _flash_pair_call(
        "diff", qkv, qkv, qkv, bias, B_HEADS,
        (lambda g: 2 * g, lambda g: 2 * g + 1, lambda g: B_Q_BLOCKS + g, None,
         lambda g: B_Q_BLOCKS + B_HEADS + g),
        B_HEADS * B_V_DIM, [lam_prm, g_subln.reshape(1, B_V_DIM)])


def mla_attn(q, k, v, mask):
    return _flash_pair_call(
        "mla", q, k, v, mask, C_HEADS // 2,
        (lambda g: 2 * g, lambda g: 2 * g + 1, lambda g: 2 * g, lambda g: 2 * g + 1, lambda g: g),
        C_HEADS * C_V_DIM, [])


C_QK_WIDTH = C_HEADS * LANES
C_V_WIDTH = C_HEADS * C_V_DIM
C_LATENT_WIDTH = C_Q_RANK + C_KV_RANK + LANES
ROPE_SHIFT = LANES - C_ROPE_DIM


def _mla_prep_kernel(c_ref, gq_ref, gkv_ref, wq_ref, wk_ref, wv_ref, cos_ref, sin_ref, q_ref, k_ref, v_ref):
    c = c_ref[...]
    cos = cos_ref[...]
    sin = sin_ref[...]

    def rope(blk):
        return blk * cos + pltpu.roll(blk, ROPE_SHIFT, 1) * sin

    cq = _rms(c[:, :C_Q_RANK], gq_ref[...]).astype(BF16)
    ckv = _rms(c[:, C_Q_RANK:C_Q_RANK + C_KV_RANK], gkv_ref[...]).astype(BF16)
    q_all = jnp.dot(cq, wq_ref[...], preferred_element_type=F32)
    k_all = jnp.dot(ckv, wk_ref[...], preferred_element_type=F32)
    k_rope = rope(c[:, C_Q_RANK + C_KV_RANK:])
    scale = LOG2E / math.sqrt(C_NOPE_DIM + C_ROPE_DIM)
    for h in range(C_HEADS):
        cols = slice(h * LANES, (h + 1) * LANES)
        q_ref[:, cols] = (rope(q_all[:, cols]) * scale).astype(q_ref.dtype)
        k_ref[:, cols] = (k_all[:, cols] + k_rope).astype(k_ref.dtype)
    v_ref[...] = jnp.dot(ckv, wv_ref[...], preferred_element_type=F32).astype(v_ref.dtype)


def mla_prep(c_lat, c_col_block, g_cq, g_ckv, w_q, w_k, w_v, cos_t, sin_t, seq):
    t = c_lat.shape[0]
    tm = min(TOKEN_TILE, seq)
    per_seq = seq // tm
    full = lambda a: pl.BlockSpec(a.shape, lambda i: (0, 0))
    g_cq = g_cq.reshape(1, -1)
    g_ckv = g_ckv.reshape(1, -1)
    return pl.pallas_call(
        _mla_prep_kernel,
        grid=(t // tm,),
        in_specs=[pl.BlockSpec((tm, C_LATENT_WIDTH), lambda i: (i, c_col_block)),
                  full(g_cq), full(g_ckv), full(w_q), full(w_k), full(w_v),
                  pl.BlockSpec((tm, LANES), lambda i: (i % per_seq, 0)),
                  pl.BlockSpec((tm, LANES), lambda i: (i % per_seq, 0))],
        out_specs=[pl.BlockSpec((tm, C_QK_WIDTH), lambda i: (i, 0)),
                   pl.BlockSpec((tm, C_QK_WIDTH), lambda i: (i, 0)),
                   pl.BlockSpec((tm, C_V_WIDTH), lambda i: (i, 0))],
        out_shape=[jax.ShapeDtypeStruct((t, C_QK_WIDTH), BF16), jax.ShapeDtypeStruct((t, C_QK_WIDTH), BF16),
                   jax.ShapeDtypeStruct((t, C_V_WIDTH), BF16)],
        compiler_params=pltpu.CompilerParams(dimension_semantics=("parallel",)),
        name="mla_prep",
    )(c_lat, g_cq, g_ckv, w_q, w_k, w_v, cos_t, sin_t)


def _merge_proj_kernel(x_ref, ga_ref, gb_ref, gc_ref, o1_ref, o2_ref, o3_ref, l1_ref, l2_ref, l3_ref,
                       yb_ref, yc_ref, wa_ref, wb_ref, wc_ref, wo_ref, gffn_ref, wpq_ref,
                       xo_ref, h2_ref, qp_ref, *token_order_scr):
    tm = x_ref.shape[0]

    def token_order(ref, scr):
        d = ref.shape[1] // A_OUT_WIDTH
        if d == 1:
            return ref[...]
        for r in range(d):
            for c in range(A_PAIRS):
                cols = slice(r * A_OUT_WIDTH + c * LANES, r * A_OUT_WIDTH + (c + 1) * LANES)
                scr[c, pl.ds(r, tm // d, stride=d), :] = ref[:, cols]
        return jnp.concatenate([scr[c] for c in range(A_PAIRS)], axis=1)

    o1, o2, o3, l1, l2, l3 = [token_order(ref, scr) for ref, scr in
                              zip((o1_ref, o2_ref, o3_ref, l1_ref, l2_ref, l3_ref), token_order_scr)]
    m = jnp.maximum(jnp.maximum(l1, l2), l3)
    e1, e2, e3 = jnp.exp(l1 - m), jnp.exp(l2 - m), jnp.exp(l3 - m)
    den = e1 + e2 + e3
    ya = (e1 / den) * o1 + (e2 / den) * o2 + (e3 / den) * o3
    pa = jnp.dot(ya.astype(BF16), wa_ref[...], preferred_element_type=F32)
    pb = jnp.dot(yb_ref[...], wb_ref[...], preferred_element_type=F32)
    pc = jnp.dot(yc_ref[...], wc_ref[...], preferred_element_type=F32)
    mix = (jax.nn.sigmoid(ga_ref[...]) * pa + jax.nn.sigmoid(gb_ref[...]) * pb
           + jax.nn.sigmoid(gc_ref[...]) * pc)
    x_new = x_ref[...] + jnp.dot(mix.astype(BF16), wo_ref[...], preferred_element_type=F32)
    xo_ref[...] = x_new
    h2 = _rms(x_new, gffn_ref[...]).astype(BF16)
    h2_ref[...] = h2
    qp_ref[...] = jnp.dot(h2, wpq_ref[...], preferred_element_type=F32)


def merge_proj(x, gc_arr, a_outs, a_lses, y_b, y_c, w_a, w_b, w_c, w_o, g_ffn, w_pq):
    t, d = x.shape
    tm = min(TOKEN_TILE, t)
    row = lambda width, col=0: pl.BlockSpec((tm, width), lambda i: (i, col))
    full = lambda a: pl.BlockSpec(a.shape, lambda i: (0, 0))
    g_ffn = g_ffn.reshape(1, d)
    group = lambda a: pl.BlockSpec((tm // (a.shape[1] // A_OUT_WIDTH), a.shape[1]), lambda i: (i, 0))
    in_specs = ([row(d)] + [row(d, c) for c in range(3)]
                + [group(a) for a in (*a_outs, *a_lses)] + [row(y_b.shape[1]), row(y_c.shape[1])]
                + [full(w_a), full(w_b), full(w_c), full(w_o), full(g_ffn), full(w_pq)])
    return pl.pallas_call(
        _merge_proj_kernel,
        grid=(t // tm,),
        in_specs=in_specs,
        out_specs=[row(d), row(d), row(d)],
        out_shape=[jax.ShapeDtypeStruct((t, d), F32), jax.ShapeDtypeStruct((t, d), BF16),
                   jax.ShapeDtypeStruct((t, d), F32)],
        scratch_shapes=[pltpu.VMEM((A_PAIRS, tm, LANES), F32)] * 6,
        compiler_params=pltpu.CompilerParams(dimension_semantics=("parallel",), vmem_limit_bytes=VMEM_LIMIT),
        name="merge_proj",
    )(x, gc_arr, gc_arr, gc_arr, *a_outs, *a_lses, y_b, y_c, w_a, w_b, w_c, w_o, g_ffn, w_pq)


def _oddeven_merge_sort_pairs(n):
    pairs = []
    p = 1
    while p < n:
        k = p
        while k >= 1:
            for j in range(k % p, n - k, 2 * k):
                for i in range(min(k, n - j - k)):
                    if (i + j) // (2 * p) == (i + j + k) // (2 * p):
                        pairs.append((i + j, i + j + k))
            k //= 2
        p *= 2
    return pairs


def _bitonic_merge_pairs(n):
    pairs = []
    stride = n // 2
    while stride >= 1:
        pairs += [(i, i + stride) for i in range(n) if not i & stride]
        stride //= 2
    return pairs


def _compare_exchange(vals, pairs):
    for i, j in pairs:
        a, b = vals[i], vals[j]
        if b is None:
            continue
        if a is None:
            vals[i], vals[j] = b, None
        else:
            vals[i], vals[j] = jnp.maximum(a, b), jnp.minimum(a, b)
    return vals


_SORT16 = _oddeven_merge_sort_pairs(PEER_TOPK)
_MERGE16 = _bitonic_merge_pairs(PEER_TOPK)
_CAND_PAIRS = [(a, b) for a in range(PEER_TOPK) for b in range(PEER_TOPK) if (a + 1) * (b + 1) <= PEER_TOPK]
_SORT64 = _oddeven_merge_sort_pairs(64)


def _top16_sorted(scores_t):
    slabs = [scores_t[v * SUBLANES:(v + 1) * SUBLANES, :] for v in range(PEER_KEYS // SUBLANES)]
    slabs = _compare_exchange(slabs, _SORT16)
    shift = SUBLANES // 2
    while shift >= 1:
        other = [pltpu.roll(s, shift, 0) for s in slabs]
        slabs = [jnp.maximum(slabs[r], other[PEER_TOPK - 1 - r]) for r in range(PEER_TOPK)]
        slabs = _compare_exchange(slabs, _MERGE16)
        shift //= 2
    return slabs


def _peer_select_kernel(qp_ref, keys_ref, s1_ref, s2_ref, c1_ref, e2_ref, thr_ref):
    tm = qp_ref.shape[0]
    sub = lax.broadcasted_iota(jnp.int32, (SUBLANES, tm), 0)
    tops = [[None] * PEER_TOPK for _ in range(2)]
    for h in range(PEER_HEADS):
        qh = qp_ref[:, h * LANES:(h + 1) * LANES].astype(BF16)
        for p in range(2):
            s_t = _nt_dot(keys_ref[2 * h + p], qh)
            if p == 0:
                s1_ref[h] = s_t
            else:
                for c in range(tm // LANES):
                    s2_ref[h, c] = s_t[:, c * LANES:(c + 1) * LANES]
            top = _top16_sorted(s_t)
            for r in range(PEER_TOPK):
                tops[p][r] = top[r] if h == 0 else jnp.where(sub == h, top[r], tops[p][r])
    cand = [tops[0][a] + tops[1][b] for a, b in _CAND_PAIRS] + [None] * (64 - len(_CAND_PAIRS))
    best = _compare_exchange(cand, _SORT64)[:PEER_TOPK]
    mx = best[0]
    z = functools.reduce(lambda u, w_: u + w_, [jnp.exp(b - mx) for b in best])
    thr_ref[...] = best[PEER_TOPK - 1]
    m1, m2 = tops[0][0], tops[1][0]
    for h in range(PEER_HEADS):
        c1_ref[h] = jnp.exp(s1_ref[h] - m1[h:h + 1, :]) / z[h:h + 1, :]
        for c in range(tm // LANES):
            e2_ref[h, c] = jnp.exp(s2_ref[h, c] - m2[h:h + 1, c * LANES:(c + 1) * LANES])


def peer_select(qp, keys_padded):
    t = qp.shape[0]
    tm = min(PEER_TOKEN_TILE, t)
    first = jax.ShapeDtypeStruct((PEER_HEADS, PEER_KEYS, t), F32)
    first_spec = pl.BlockSpec((PEER_HEADS, PEER_KEYS, tm), lambda i: (0, 0, i))
    second = jax.ShapeDtypeStruct((PEER_HEADS, t // LANES, PEER_KEYS, LANES), F32)
    second_spec = pl.BlockSpec((PEER_HEADS, tm // LANES, PEER_KEYS, LANES), lambda i: (0, i, 0, 0))
    return pl.pallas_call(
        _peer_select_kernel,
        grid=(t // tm,),
        in_specs=[pl.BlockSpec((tm, PEER_HEADS * PEER_KEY_DIM), lambda i: (i, 0)),
                  pl.BlockSpec(keys_padded.shape, lambda i: (0, 0, 0))],
        out_specs=[first_spec, second_spec, first_spec, second_spec,
                   pl.BlockSpec((PEER_HEADS, tm), lambda i: (0, i))],
        out_shape=[first, second, first, second, jax.ShapeDtypeStruct((PEER_HEADS, t), F32)],
        compiler_params=pltpu.CompilerParams(dimension_semantics=("parallel",), vmem_limit_bytes=VMEM_LIMIT),
        name="peer_select",
    )(qp, keys_padded)


def _peer_dense_kernel(h_ref, u_ref, vt_ref, s1_ref, s2_ref, c1_ref, e2_ref, thr_ref, x_ref, g_ref,
                       o_ref, acc_scr, act_scr, wa_scr, *, final_norm):
    j = pl.program_id(1)
    tm = h_ref.shape[0]
    ec = u_ref.shape[0]
    rows_per_tile = ec // PEER_KEYS

    @pl.when(j == 0)
    def _():
        acc_scr[...] = jnp.zeros(acc_scr.shape, F32)

    n_chunks = tm // LANES
    halves = [range(0, n_chunks // 2), range(n_chunks // 2, n_chunks)]
    for half in halves:
        rows = slice(half[0] * LANES, (half[-1] + 1) * LANES)
        act = _gelu(_nt_dot(u_ref[...], h_ref[rows, :]))
        for n, tc in enumerate(half):
            act_scr[tc] = act[:, n * LANES:(n + 1) * LANES]

    def build(tc, carry):
        cols = pl.ds(pl.multiple_of(tc * LANES, LANES), LANES)
        for ii in range(rows_per_tile):
            rows = slice(ii * PEER_KEYS, (ii + 1) * PEER_KEYS)
            w = jnp.zeros((PEER_KEYS, LANES), F32)
            for h in range(PEER_HEADS):
                pair_score = s2_ref[h, tc] + s1_ref[h, ii:ii + 1, cols]
                gate = e2_ref[h, tc] * c1_ref[h, ii:ii + 1, cols]
                w = w + jnp.where(pair_score >= thr_ref[h:h + 1, cols], gate, 0.0)
            wa_scr[tc, rows, :] = (w * act_scr[tc, rows, :]).astype(BF16)
        return carry

    lax.fori_loop(0, n_chunks, build, 0)
    for half in halves:
        cols = slice(half[0] * LANES, (half[-1] + 1) * LANES)
        wa = jnp.concatenate([wa_scr[tc] for tc in half], axis=1)
        acc_scr[:, cols] += jnp.dot(vt_ref[...], wa, preferred_element_type=F32)

    @pl.when(j == pl.num_programs(1) - 1)
    def _():
        y = x_ref[...] + acc_scr[...].T
        if final_norm:
            y = _rms(y, g_ref[...])
        o_ref[...] = y


def peer_dense(h2, u_bf, vt_bf, s1, s2, c1, e2, thr, x, g_final, final_norm):
    t, d = x.shape
    n_exp = u_bf.shape[0]
    tm = min(PEER_TOKEN_TILE, t)
    ec = PEER_EXPERT_TILE
    sel_spec = pl.BlockSpec((PEER_HEADS, tm // LANES, PEER_KEYS, LANES), lambda i, j: (0, i, 0, 0))
    key_spec = pl.BlockSpec((PEER_HEADS, ec // PEER_KEYS, tm), lambda i, j: (0, j, i))
    return pl.pallas_call(
        functools.partial(_peer_dense_kernel, final_norm=final_norm),
        grid=(t // tm, n_exp // ec),
        in_specs=[pl.BlockSpec((tm, d), lambda i, j: (i, 0)),
                  pl.BlockSpec((ec, d), lambda i, j: (j, 0)),
                  pl.BlockSpec((d, ec), lambda i, j: (0, j)),
                  key_spec, sel_spec, key_spec, sel_spec,
                  pl.BlockSpec((PEER_HEADS, tm), lambda i, j: (0, i)),
                  pl.BlockSpec((tm, d), lambda i, j: (i, 0)),
                  pl.BlockSpec((1, d), lambda i, j: (0, 0))],
        out_specs=pl.BlockSpec((tm, d), lambda i, j: (i, 0)),
        out_shape=jax.ShapeDtypeStruct((t, d), F32),
        scratch_shapes=[pltpu.VMEM((d, tm), F32), pltpu.VMEM((tm // LANES, ec, LANES), F32),
                        pltpu.VMEM((tm // LANES, ec, LANES), BF16)],
        compiler_params=pltpu.CompilerParams(dimension_semantics=("parallel", "arbitrary"),
                                             vmem_limit_bytes=VMEM_LIMIT),
        name="peer_dense",
    )(h2, u_bf, vt_bf, s1, s2, c1, e2, thr, x, g_final.reshape(1, d))


def _t5_bucket_table(max_dist):
    n = np.arange(max_dist + 1)
    max_exact = N_BUCKETS // 2
    nf = np.maximum(n, max_exact).astype(np.float32)
    large = max_exact + (np.log(nf / np.float32(max_exact)) / np.float32(math.log(MAX_DISTANCE / max_exact))
                         * np.float32(N_BUCKETS - max_exact)).astype(np.int32)
    large = np.minimum(large, N_BUCKETS - 1)
    return np.where(n < max_exact, n, large).astype(np.int32)


def _skew(vec, rows):
    h, length = vec.shape
    return jnp.tile(vec, (1, rows))[:, :rows * (length - 1)].reshape(h, rows, length - 1)


def _banded_bias(bias_heads, dilation):
    w = A_BLOCK
    heads = bias_heads.shape[1]
    by_rel = bias_heads[_t5_bucket_table(w * dilation)[np.arange(w + 1) * dilation]].T.astype(F32)
    vec = jnp.concatenate([by_rel, jnp.full((heads, 3 * w - (w + 1)), NEG_INF, F32)], axis=1)
    return jnp.transpose(_skew(vec, 2 * w)[:, :, w:2 * w], (0, 2, 1))


def _diagonal_tiles(by_dist, tq, tk, n_tiles):
    heads = by_dist.shape[0]
    n_cols = (n_tiles - 1) * tk + tq
    mask = jnp.full((heads, tk), NEG_INF, F32)
    vec = jnp.concatenate([mask, by_dist, mask], axis=1)
    assert vec.shape[1] == n_cols + tk
    skew = _skew(vec, tk)
    tiles = jnp.stack([skew[:, :, i * tk:i * tk + tq] for i in range(n_tiles)], axis=1)
    return jnp.transpose(tiles, (0, 1, 3, 2))


def _causal_bias_tiles(bias_heads, tq, tk):
    n_tiles = 2 * -(-(MAX_DISTANCE - 1 + tk + tk) // (2 * tk))
    far = bias_heads[N_BUCKETS - 1]
    n_dist = (n_tiles - 2) * tk + tq
    by_dist = (bias_heads - far[None, :])[_t5_bucket_table(n_dist - 1)].T.astype(F32) * LOG2E
    return _diagonal_tiles(by_dist, tq, tk, n_tiles)


def _causal_mask_tiles(tq, tk):
    return _diagonal_tiles(jnp.zeros((1, tq), F32), tq, tk, 2)


def _rope_tables(seq):
    half = C_ROPE_DIM // 2
    inv = ROPE_BASE ** (-jnp.arange(half, dtype=F32) / half)
    ang = jnp.arange(seq, dtype=F32)[:, None] * inv[None, :]
    cos, sin = jnp.cos(ang), jnp.sin(ang)
    ones = jnp.ones((seq, C_NOPE_DIM), F32)
    zeros_n = jnp.zeros((seq, C_NOPE_DIM), F32)
    spare = jnp.zeros((seq, LANES - C_NOPE_DIM - C_ROPE_DIM), F32)
    cos_t = jnp.concatenate([ones, cos, cos, spare], axis=1)
    sin_t = jnp.concatenate([zeros_n, -sin, sin, spare], axis=1)
    return cos_t, sin_t


def _swap_halves(w):
    half = w.shape[-1] // 2
    return jnp.concatenate([w[..., half:], w[..., :half]], axis=-1)


def _prep_in_proj(w_in):
    d = w_in.shape[0]
    bounds = np.cumsum([A_COLS, B_QK_COLS, B_QK_COLS, B_V_COLS, C_Q_RANK, C_KV_RANK, C_ROPE_DIM])
    wa, wbq, wbk, wbv, wcq, wckv, wckr, wg = jnp.split(w_in, bounds, axis=1)
    zeros = lambda n: jnp.zeros((d, n), w_in.dtype)
    wa = wa.reshape(d, 3, A_HEADS, A_HEAD_DIM)
    qa = wa[:, 0] * (1.0 / math.sqrt(A_HEAD_DIM))
    q_blocks = []
    for h in range(A_HEADS):
        pad = zeros(A_HEAD_DIM)
        q_blocks += [qa[:, h], pad] if h % 2 == 0 else [pad, qa[:, h]]
    w_a = jnp.concatenate(q_blocks + [wa[:, 1].reshape(d, -1), wa[:, 2].reshape(d, -1)], axis=1)
    qb = wbq.reshape(d, B_HEADS, 2, B_QK_DIM) * (LOG2E / math.sqrt(B_QK_DIM))
    q_blocks = []
    for h in range(B_HEADS):
        pad = zeros(B_QK_DIM)
        q_blocks += [qb[:, h, 0], pad, pad, qb[:, h, 1]]
    w_b = jnp.concatenate(q_blocks + [wbk, wbv], axis=1)
    w_gc = jnp.concatenate([wg, wcq, wckv, zeros(C_NOPE_DIM), wckr, _swap_halves(wckr)], axis=1)
    return w_a.astype(BF16), w_b.astype(BF16), w_gc.astype(BF16)


def _prep_mla_weights(w_uq, w_ukv):
    rq = w_uq.shape[0]
    wq = w_uq.reshape(rq, C_HEADS, C_NOPE_DIM + C_ROPE_DIM)
    q_rope = wq[..., C_NOPE_DIM:]
    wq = jnp.concatenate([wq, _swap_halves(q_rope)], axis=-1).reshape(rq, C_QK_WIDTH)
    rkv = w_ukv.shape[0]
    wkv = w_ukv.reshape(rkv, C_HEADS, C_NOPE_DIM + C_V_DIM)
    wk = jnp.concatenate([wkv[..., :C_NOPE_DIM], jnp.zeros((rkv, C_HEADS, LANES - C_NOPE_DIM), w_ukv.dtype)],
                         axis=-1).reshape(rkv, C_QK_WIDTH)
    wv = wkv[..., C_NOPE_DIM:].reshape(rkv, C_V_WIDTH)
    return wq.astype(BF16), wk.astype(BF16), wv.astype(BF16)


def _prep_peer_keys(sub_keys):
    half = PEER_KEY_DIM // 2
    z = jnp.zeros(sub_keys.shape[:1] + sub_keys.shape[2:], sub_keys.dtype)
    first = jnp.concatenate([sub_keys[:, 0], z], axis=-1)
    second = jnp.concatenate([z, sub_keys[:, 1]], axis=-1)
    assert first.shape[-1] == LANES and half * 2 == LANES
    return jnp.stack([first, second], axis=1).reshape(2 * PEER_HEADS, PEER_KEYS, LANES).astype(BF16)


def kernel(x, rel_bias, w_in, g_mix, w_uq, g_cq, w_ukv, g_ckv, lam_q1, lam_k1, lam_q2, lam_k2, g_subln,
           w_branch_a, w_branch_b, w_branch_c, w_out, g_ffn, w_peer_q, peer_sub_keys, peer_u, peer_v, g_final):
    bsz, seq, d = x.shape
    t = bsz * seq
    depth = w_in.shape[0]
    bias_a = rel_bias[:, :A_HEADS]
    bias_b = rel_bias[:, A_HEADS:]
    a_biases = [_banded_bias(bias_a, dil) for _, dil in A_CONFIGS]
    b_bias = _causal_bias_tiles(bias_b, FLASH_Q_TILE, FLASH_K_TILE)
    c_mask = _causal_mask_tiles(FLASH_Q_TILE, FLASH_K_TILE)
    cos_t, sin_t = _rope_tables(seq)
    gate_blocks = 3 * d // C_LATENT_WIDTH

    xf = x.reshape(t, d)
    for l in range(depth):
        w_a, w_b, w_gc = _prep_in_proj(w_in[l])
        dilations = tuple(dil for _, dil in A_CONFIGS)
        views_a = norm_matmul_dilated(xf, g_mix[l], w_a, dilations, A_Q_WIDTH)
        qkv_b = norm_matmul(xf, g_mix[l], w_b, BF16).reshape(bsz, seq, -1)
        gc = norm_matmul(xf, g_mix[l], w_gc, F32)

        a_outs, a_lses = [], []
        for dil, view, bias in zip(dilations, views_a, a_biases):
            o, lse = local_attn(view.reshape(bsz, seq // dil, -1), bias, dil)
            a_outs.append(o.reshape(t // dil, -1))
            a_lses.append(lse.reshape(t // dil, -1))

        lam_init = 0.8 - 0.6 * math.exp(-0.3 * l)
        pad = lambda v: jnp.pad(v.astype(F32), (0, LANES - v.shape[0]))
        lam_prm = jnp.stack([pad(lam_q1[l]), pad(lam_k1[l]), pad(lam_q2[l]), pad(lam_k2[l]),
                             jnp.full((LANES,), lam_init, F32)]
                            + [jnp.zeros((LANES,), F32)] * (SUBLANES - 5))
        y_b = diff_attn(qkv_b, b_bias, lam_prm, g_subln[l]).reshape(t, -1)

        wq, wk, wv = _prep_mla_weights(w_uq[l], w_ukv[l])
        q_c, k_c, v_c = mla_prep(gc, gate_blocks, g_cq[l], g_ckv[l], wq, wk, wv, cos_t, sin_t, seq)
        y_c = mla_attn(q_c.reshape(bsz, seq, -1), k_c.reshape(bsz, seq, -1), v_c.reshape(bsz, seq, -1),
                       c_mask).reshape(t, -1)

        xf, h2, qp = merge_proj(xf, gc, a_outs, a_lses, y_b, y_c,
                                w_branch_a[l].astype(BF16), w_branch_b[l].astype(BF16),
                                w_branch_c[l].astype(BF16), w_out[l].astype(BF16), g_ffn[l],
                                w_peer_q[l].astype(BF16))

        s1, s2, c1, e2, thr = peer_select(qp, _prep_peer_keys(peer_sub_keys[l]))
        xf = peer_dense(h2, peer_u[l].astype(BF16), peer_v[l].T.astype(BF16), s1, s2, c1, e2, thr, xf,
                        g_final, final_norm=(l == depth - 1))
    return xf.reshape(bsz, seq, d)
```

```python
import functools
import math

import numpy as np
import jax
import jax.numpy as jnp
from jax import lax
from jax.experimental import pallas as pl
from jax.experimental.pallas import tpu as pltpu

D_MODEL = 1024
DEPTH = 4
A_HEADS = 6
A_HEAD_DIM = 64
A_CONFIGS = ((128, 1), (512, 4), (2048, 16))
B_HEADS = 4
B_QK_DIM = 64
B_V_DIM = 128
C_HEADS = 6
C_NOPE_DIM = 64
C_ROPE_DIM = 32
C_V_DIM = 64
C_Q_RANK = 256
C_KV_RANK = 128
ROPE_BASE = 10000.0
N_BUCKETS = 32
MAX_DISTANCE = 2048
PEER_HEADS = 8
PEER_KEYS = 128
PEER_EXPERTS = PEER_KEYS * PEER_KEYS
PEER_KEY_DIM = 128
PEER_TOPK = 16
RMS_EPS = 1e-6
NEG_INF = -1e30
LOG2E = math.log2(math.e)

A_COLS = 3 * A_HEADS * A_HEAD_DIM
B_QK_COLS = B_HEADS * 2 * B_QK_DIM
B_V_COLS = B_HEADS * B_V_DIM

LANES = 128
SUBLANES = 8
VMEM_LIMIT = 56 * 1024 * 1024

F32 = jnp.float32
BF16 = jnp.bfloat16

TOKEN_TILE = 512
PROJ_TOKEN_TILE = 2048
PROJ_COL_TILE = 512
A_BLOCK = 128
A_BLOCKS_PER_STEP = 4
FLASH_Q_TILE = 512
FLASH_K_TILE = 256
FLASH_ROW_BLOCK = 128
PEER_TOKEN_TILE = 512
PEER_EXPERT_TILE = 1024


def _nt_dot(a, b):
    return lax.dot_general(a, b, (((1,), (1,)), ((), ())), preferred_element_type=F32)


def _gelu(x):
    return 0.5 * x * (1.0 + lax.erf(x * math.sqrt(0.5)))


def _rms(x, g):
    return x * lax.rsqrt(jnp.mean(x * x, axis=-1, keepdims=True) + RMS_EPS) * g


def _norm_matmul_kernel(x_ref, g_ref, w_ref, o_ref, h_scr):
    @pl.when(pl.program_id(1) == 0)
    def _():
        h_scr[...] = _rms(x_ref[...], g_ref[...]).astype(h_scr.dtype)

    o_ref[...] = jnp.dot(h_scr[...], w_ref[...], preferred_element_type=F32).astype(o_ref.dtype)


def norm_matmul(x, g, w, out_dtype):
    t, k = x.shape
    n = w.shape[1]
    tm, tn = min(PROJ_TOKEN_TILE, t), PROJ_COL_TILE
    return pl.pallas_call(
        _norm_matmul_kernel,
        grid=(t // tm, n // tn),
        in_specs=[pl.BlockSpec((tm, k), lambda i, j: (i, 0)),
                  pl.BlockSpec((1, k), lambda i, j: (0, 0)),
                  pl.BlockSpec((k, tn), lambda i, j: (0, j))],
        out_specs=pl.BlockSpec((tm, tn), lambda i, j: (i, j)),
        out_shape=jax.ShapeDtypeStruct((t, n), out_dtype),
        scratch_shapes=[pltpu.VMEM((tm, k), BF16)],
        compiler_params=pltpu.CompilerParams(dimension_semantics=("parallel", "arbitrary"),
                                             vmem_limit_bytes=VMEM_LIMIT),
        name="norm_matmul",
    )(x, g.reshape(1, k), w)


def _norm_matmul_dilated_kernel(x_ref, g_ref, w_ref, *rest, dilations):
    out_refs, (h_scr, res_scr) = rest[:len(dilations)], rest[len(dilations):]
    @pl.when(pl.program_id(1) == 0)
    def _():
        h_scr[...] = _rms(x_ref[...], g_ref[...]).astype(h_scr.dtype)

    res = jnp.dot(h_scr[...], w_ref[...], preferred_element_type=F32)
    n_lane_blocks, tm, _ = res_scr.shape
    tn = n_lane_blocks * LANES
    for c in range(n_lane_blocks):
        res_scr[c] = res[:, c * LANES:(c + 1) * LANES]
    for d, o_ref in zip(dilations, out_refs):
        for r in range(d):
            for c in range(n_lane_blocks):
                cols = slice(r * tn + c * LANES, r * tn + (c + 1) * LANES)
                o_ref[:, cols] = res_scr[c, pl.ds(r, tm // d, stride=d), :].astype(o_ref.dtype)


def norm_matmul_dilated(x, g, w, dilations, tn):
    t, k = x.shape
    n = w.shape[1]
    tm = min(PROJ_TOKEN_TILE, t)
    assert n % tn == 0 and all(tm % (d * SUBLANES) == 0 for d in dilations)
    return pl.pallas_call(
        functools.partial(_norm_matmul_dilated_kernel, dilations=dilations),
        grid=(t // tm, n // tn),
        in_specs=[pl.BlockSpec((tm, k), lambda i, j: (i, 0)),
                  pl.BlockSpec((1, k), lambda i, j: (0, 0)),
                  pl.BlockSpec((k, tn), lambda i, j: (0, j))],
        out_specs=[pl.BlockSpec((tm // d, d * tn), lambda i, j: (i, j)) for d in dilations],
        out_shape=[jax.ShapeDtypeStruct((t // d, d * n), BF16) for d in dilations],
        scratch_shapes=[pltpu.VMEM((tm, k), BF16), pltpu.VMEM((tn // LANES, tm, LANES), F32)],
        compiler_params=pltpu.CompilerParams(dimension_semantics=("parallel", "arbitrary"),
                                             vmem_limit_bytes=VMEM_LIMIT),
        name="norm_matmul_dilated",
    )(x, g.reshape(1, k), w)


A_Q_WIDTH = A_HEADS * LANES
A_PAIRS = A_HEADS // 2
A_KV_WIDTH = 2 * A_PAIRS * LANES
A_OUT_WIDTH = A_PAIRS * LANES


def _local_attn_kernel(q_ref, kvc_ref, kvp_ref, bias_ref, o_ref, lse_ref, kv_scr, *, nblk):
    w = A_BLOCK
    first_class_block = pl.program_id(2) == 0
    kv_scr[0:w, :] = kvp_ref[0]
    kv_scr[w:, :] = kvc_ref[0]
    lane = lax.broadcasted_iota(jnp.int32, (w, LANES), 1)
    col = lax.broadcasted_iota(jnp.int32, (w, 2 * w), 1)
    no_prev = jnp.where(col < w, jnp.where(first_class_block, NEG_INF, 0.0), 0.0)
    for i in range(nblk):
        rows = slice(i * w, (i + 1) * w)
        for p in range(A_PAIRS):
            k_blk = kv_scr[i * w:(i + 2) * w, p * LANES:(p + 1) * LANES]
            v_blk = kv_scr[i * w:(i + 2) * w, (A_PAIRS + p) * LANES:(A_PAIRS + p + 1) * LANES]
            outs, lses = [], []
            for e in range(2):
                hd = 2 * p + e
                q = q_ref[0, rows, hd * LANES:(hd + 1) * LANES]
                s = _nt_dot(q, k_blk) + bias_ref[hd]
                if i == 0:
                    s = s + no_prev
                m = jnp.max(s, axis=-1, keepdims=True)
                pr = jnp.exp(s - m)
                l = jnp.sum(pr, axis=-1, keepdims=True)
                o = jnp.dot(pr.astype(BF16), v_blk, preferred_element_type=F32)
                outs.append(o / l)
                lses.append(m + jnp.log(l))
            o_ref[0, rows, p * LANES:(p + 1) * LANES] = jnp.where(lane < A_HEAD_DIM, outs[0], outs[1])
            lse_ref[0, rows, p * LANES:(p + 1) * LANES] = jnp.where(lane < A_HEAD_DIM, lses[0], lses[1])


def local_attn(view, bias, dilation):
    bsz, rows, width = view.shape
    d = dilation
    w = A_BLOCK
    nblk = min(A_BLOCKS_PER_STEP, rows // w)
    assert rows % (w * nblk) == 0 and width == d * (A_Q_WIDTH + A_KV_WIDTH) and A_Q_WIDTH == A_KV_WIDTH
    grid = (bsz, d, rows // (w * nblk))
    out_shape = jax.ShapeDtypeStruct((bsz, rows, d * A_OUT_WIDTH), F32)
    o, lse = pl.pallas_call(
        functools.partial(_local_attn_kernel, nblk=nblk),
        grid=grid,
        in_specs=[pl.BlockSpec((1, w * nblk, A_Q_WIDTH), lambda b, r, n: (b, n, r)),
                  pl.BlockSpec((1, w * nblk, A_KV_WIDTH), lambda b, r, n: (b, n, d + r)),
                  pl.BlockSpec((1, w, A_KV_WIDTH), lambda b, r, n: (b, jnp.maximum(n * nblk - 1, 0), d + r)),
                  pl.BlockSpec((A_HEADS, w, 2 * w), lambda b, r, n: (0, 0, 0))],
        out_specs=[pl.BlockSpec((1, w * nblk, A_OUT_WIDTH), lambda b, r, n: (b, n, r)),
                   pl.BlockSpec((1, w * nblk, A_OUT_WIDTH), lambda b, r, n: (b, n, r))],
        out_shape=[out_shape, out_shape],
        scratch_shapes=[pltpu.VMEM((w * (nblk + 1), A_KV_WIDTH), BF16)],
        compiler_params=pltpu.CompilerParams(dimension_semantics=("parallel", "parallel", "arbitrary")),
        name=f"local_attn_d{d}",
    )(view, view, view, bias)
    return o, lse


def _flash_pair_kernel(*refs, tq, tk, n_near, shared_k, mode):
    if shared_k:
        qa_ref, qb_ref, ka_ref, v_ref, bias_ref = refs[:5]
        kb_ref = ka_ref
        rest = refs[5:]
    else:
        qa_ref, qb_ref, ka_ref, kb_ref, v_ref, bias_ref = refs[:6]
        rest = refs[6:]
    if mode == "diff":
        prm_ref, gsub_ref, o_ref, m_scr, alpha_scr, acc_scr, s0_scr, s1_scr, p_scr = rest
    else:
        o_ref, m_scr, alpha_scr, acc_scr, s0_scr, s1_scr, p_scr = rest

    qi = pl.program_id(2)
    rb = FLASH_ROW_BLOCK
    m_scr[...] = jnp.full(m_scr.shape, NEG_INF, F32)
    acc_scr[...] = jnp.zeros(acc_scr.shape, F32)
    ones = jnp.ones((tk, LANES), BF16)
    last_kt = 2 * qi + 1

    def scores(kt, s_scr):
        koff = pl.multiple_of(jnp.minimum(kt, last_kt) * tk, tk)
        ka = ka_ref[0, pl.ds(koff, tk), :]
        if shared_k:
            s_scr[...] = _nt_dot(jnp.concatenate([qa_ref[0], qb_ref[0]], axis=0), ka)
        else:
            s_scr[0:tq, :] = _nt_dot(qa_ref[0], ka)
            s_scr[tq:, :] = _nt_dot(qb_ref[0], kb_ref[0, pl.ds(koff, tk), :])

    def accumulate(kt, s_scr, near):
        koff = pl.multiple_of(kt * tk, tk)
        for r in range(2 * tq // rb):
            rows = slice(r * rb, (r + 1) * rb)
            s = s_scr[rows, :]
            if near:
                b0 = (r * rb) % tq
                s = s + bias_ref[0, last_kt - kt, b0:b0 + rb, :]
            m_prev = m_scr[rows, :]
            m_new = jnp.maximum(m_prev, jnp.max(s, axis=-1, keepdims=True))
            alpha_scr[rows, :] = jnp.exp2(m_prev - m_new)
            m_scr[rows, :] = m_new
            p_scr[rows, :] = jnp.exp2(s - jnp.concatenate([m_new] * (tk // LANES), axis=1)).astype(BF16)
        v_ones = jnp.concatenate([v_ref[0, pl.ds(koff, tk), :], ones], axis=1)
        pv = jnp.dot(p_scr[...], v_ones, preferred_element_type=F32)
        alpha = alpha_scr[...]
        acc_scr[...] = jnp.concatenate([alpha, alpha], axis=1) * acc_scr[...] + pv

    n_tiles = 2 * qi + 2
    n_far = jnp.maximum(n_tiles - n_near, 0)

    def run(first, count, near):
        def unrolled(kt, steps):
            for step in range(steps):
                nxt, cur = (s1_scr, s0_scr) if step % 2 == 0 else (s0_scr, s1_scr)
                scores(kt + step + 1, nxt)
                accumulate(kt + step, cur, near)

        def quad(i, carry):
            unrolled(first + 4 * i, 4)
            return carry

        def pair(i, carry):
            unrolled(first + 4 * (count // 4) + 2 * i, 2)
            return carry

        lax.fori_loop(0, count // 4, quad, 0)
        lax.fori_loop(0, (count % 4) // 2, pair, 0)

    scores(0, s0_scr)
    run(0, n_far, False)
    run(n_far, n_tiles - n_far, True)

    o = acc_scr[:, :LANES] / acc_scr[:, LANES:]
    oa, ob = o[:tq], o[tq:]
    if mode == "diff":
        prm = prm_ref[...]
        lam_init = prm[4:5, 0:1]
        lam = (jnp.exp(jnp.sum(prm[0:1] * prm[1:2], axis=-1, keepdims=True))
               - jnp.exp(jnp.sum(prm[2:3] * prm[3:4], axis=-1, keepdims=True)) + lam_init)
        d = oa - lam * ob
        o_ref[0] = (_rms(d, gsub_ref[...]) * (1.0 - lam_init)).astype(o_ref.dtype)
    else:
        lane = lax.broadcasted_iota(jnp.int32, oa.shape, 1)
        o_ref[0] = jnp.where(lane < C_V_DIM, oa, ob).astype(o_ref.dtype)


def _flash_pair_call(mode, q_arr, k_arr, v_arr, bias, groups, col_maps, out_cols, extra):
    bsz, seq, _ = q_arr.shape
    tq, tk = FLASH_Q_TILE, FLASH_K_TILE
    assert seq % tq == 0 and tq == 2 * tk
    n_near = bias.shape[1]
    assert n_near % 2 == 0
    qa_c, qb_c, ka_c, kb_c, v_c = col_maps
    shared_k = kb_c is None
    q_spec = lambda cm: pl.BlockSpec((1, tq, LANES), lambda b, g, i: (b, i, cm(g)))
    kv_spec = lambda cm: pl.BlockSpec((1, seq, LANES), lambda b, g, i: (b, 0, cm(g)))
    in_specs = [q_spec(qa_c), q_spec(qb_c), kv_spec(ka_c)]
    args = [q_arr, q_arr, k_arr]
    if not shared_k:
        in_specs.append(kv_spec(kb_c))
        args.append(k_arr)
    in_specs.append(kv_spec(v_c))
    args.append(v_arr)
    bias_group = (lambda g: g) if bias.shape[0] > 1 else (lambda g: 0)
    in_specs.append(pl.BlockSpec((1, n_near, tq, tk), lambda b, g, i: (bias_group(g), 0, 0, 0)))
    args.append(bias)
    for e in extra:
        in_specs.append(pl.BlockSpec(e.shape, lambda b, g, i: (0, 0)))
        args.append(e)
    return pl.pallas_call(
        functools.partial(_flash_pair_kernel, tq=tq, tk=tk, n_near=n_near, shared_k=shared_k, mode=mode),
        grid=(bsz, groups, seq // tq),
        in_specs=in_specs,
        out_specs=pl.BlockSpec((1, tq, LANES), lambda b, g, i: (b, i, g)),
        out_shape=jax.ShapeDtypeStruct((bsz, seq, out_cols), BF16),
        scratch_shapes=[pltpu.VMEM((2 * tq, LANES), F32), pltpu.VMEM((2 * tq, LANES), F32),
                        pltpu.VMEM((2 * tq, 2 * LANES), F32), pltpu.VMEM((2 * tq, tk), F32),
                        pltpu.VMEM((2 * tq, tk), F32), pltpu.VMEM((2 * tq, tk), BF16)],
        compiler_params=pltpu.CompilerParams(dimension_semantics=("parallel", "parallel", "arbitrary"),
                                             vmem_limit_bytes=VMEM_LIMIT),
        name=f"flash_pair_{mode}",
    )(*args)


B_Q_BLOCKS = 2 * B_HEADS
B_WIDTH_PADDED = (B_Q_BLOCKS + 2 * B_HEADS) * LANES


def diff_attn(qkv, bias, lam_prm, g_subln):
    return _flash_pair_call(
        "diff", qkv, qkv, qkv, bias, B_HEADS,
        (lambda g: 2 * g, lambda g: 2 * g + 1, lambda g: B_Q_BLOCKS + g, None,
         lambda g: B_Q_BLOCKS + B_HEADS + g),
        B_HEADS * B_V_DIM, [lam_prm, g_subln.reshape(1, B_V_DIM)])


def mla_attn(q, k, v, mask):
    return _flash_pair_call(
        "mla", q, k, v, mask, C_HEADS // 2,
        (lambda g: 2 * g, lambda g: 2 * g + 1, lambda g: 2 * g, lambda g: 2 * g + 1, lambda g: g),
        C_HEADS * C_V_DIM, [])


C_QK_WIDTH = C_HEADS * LANES
C_V_WIDTH = C_HEADS * C_V_DIM
C_LATENT_WIDTH = C_Q_RANK + C_KV_RANK + LANES
ROPE_SHIFT = LANES - C_ROPE_DIM


def _mla_prep_kernel(c_ref, gq_ref, gkv_ref, wq_ref, wk_ref, wv_ref, cos_ref, sin_ref, q_ref, k_ref, v_ref):
    c = c_ref[...]
    cos = cos_ref[...]
    sin = sin_ref[...]

    def rope(blk):
        return blk * cos + pltpu.roll(blk, ROPE_SHIFT, 1) * sin

    cq = _rms(c[:, :C_Q_RANK], gq_ref[...]).astype(BF16)
    ckv = _rms(c[:, C_Q_RANK:C_Q_RANK + C_KV_RANK], gkv_ref[...]).astype(BF16)
    q_all = jnp.dot(cq, wq_ref[...], preferred_element_type=F32)
    k_all = jnp.dot(ckv, wk_ref[...], preferred_element_type=F32)
    k_rope = rope(c[:, C_Q_RANK + C_KV_RANK:])
    scale = LOG2E / math.sqrt(C_NOPE_DIM + C_ROPE_DIM)
    for h in range(C_HEADS):
        cols = slice(h * LANES, (h + 1) * LANES)
        q_ref[:, cols] = (rope(q_all[:, cols]) * scale).astype(q_ref.dtype)
        k_ref[:, cols] = (k_all[:, cols] + k_rope).astype(k_ref.dtype)
    v_ref[...] = jnp.dot(ckv, wv_ref[...], preferred_element_type=F32).astype(v_ref.dtype)


def mla_prep(c_lat, c_col_block, g_cq, g_ckv, w_q, w_k, w_v, cos_t, sin_t, seq):
    t = c_lat.shape[0]
    tm = min(TOKEN_TILE, seq)
    per_seq = seq // tm
    full = lambda a: pl.BlockSpec(a.shape, lambda i: (0, 0))
    g_cq = g_cq.reshape(1, -1)
    g_ckv = g_ckv.reshape(1, -1)
    return pl.pallas_call(
        _mla_prep_kernel,
        grid=(t // tm,),
        in_specs=[pl.BlockSpec((tm, C_LATENT_WIDTH), lambda i: (i, c_col_block)),
                  full(g_cq), full(g_ckv), full(w_q), full(w_k), full(w_v),
                  pl.BlockSpec((tm, LANES), lambda i: (i % per_seq, 0)),
                  pl.BlockSpec((tm, LANES), lambda i: (i % per_seq, 0))],
        out_specs=[pl.BlockSpec((tm, C_QK_WIDTH), lambda i: (i, 0)),
                   pl.BlockSpec((tm, C_QK_WIDTH), lambda i: (i, 0)),
                   pl.BlockSpec((tm, C_V_WIDTH), lambda i: (i, 0))],
        out_shape=[jax.ShapeDtypeStruct((t, C_QK_WIDTH), BF16), jax.ShapeDtypeStruct((t, C_QK_WIDTH), BF16),
                   jax.ShapeDtypeStruct((t, C_V_WIDTH), BF16)],
        compiler_params=pltpu.CompilerParams(dimension_semantics=("parallel",)),
        name="mla_prep",
    )(c_lat, g_cq, g_ckv, w_q, w_k, w_v, cos_t, sin_t)


def _merge_proj_kernel(x_ref, ga_ref, gb_ref, gc_ref, o1_ref, o2_ref, o3_ref, l1_ref, l2_ref, l3_ref,
                       yb_ref, yc_ref, wa_ref, wb_ref, wc_ref, wo_ref, gffn_ref, wpq_ref,
                       xo_ref, h2_ref, qp_ref, *token_order_scr):
    tm = x_ref.shape[0]

    def token_order(ref, scr):
        d = ref.shape[1] // A_OUT_WIDTH
        if d == 1:
            return ref[...]
        for r in range(d):
            for c in range(A_PAIRS):
                cols = slice(r * A_OUT_WIDTH + c * LANES, r * A_OUT_WIDTH + (c + 1) * LANES)
                scr[c, pl.ds(r, tm // d, stride=d), :] = ref[:, cols]
        return jnp.concatenate([scr[c] for c in range(A_PAIRS)], axis=1)

    o1, o2, o3, l1, l2, l3 = [token_order(ref, scr) for ref, scr in
                              zip((o1_ref, o2_ref, o3_ref, l1_ref, l2_ref, l3_ref), token_order_scr)]
    m = jnp.maximum(jnp.maximum(l1, l2), l3)
    e1, e2, e3 = jnp.exp(l1 - m), jnp.exp(l2 - m), jnp.exp(l3 - m)
    den = e1 + e2 + e3
    ya = (e1 / den) * o1 + (e2 / den) * o2 + (e3 / den) * o3
    pa = jnp.dot(ya.astype(BF16), wa_ref[...], preferred_element_type=F32)
    pb = jnp.dot(yb_ref[...], wb_ref[...], preferred_element_type=F32)
    pc = jnp.dot(yc_ref[...], wc_ref[...], preferred_element_type=F32)
    mix = (jax.nn.sigmoid(ga_ref[...]) * pa + jax.nn.sigmoid(gb_ref[...]) * pb
           + jax.nn.sigmoid(gc_ref[...]) * pc)
    x_new = x_ref[...] + jnp.dot(mix.astype(BF16), wo_ref[...], preferred_element_type=F32)
    xo_ref[...] = x_new
    h2 = _rms(x_new, gffn_ref[...]).astype(BF16)
    h2_ref[...] = h2
    qp_ref[...] = jnp.dot(h2, wpq_ref[...], preferred_element_type=F32)


def merge_proj(x, gc_arr, a_outs, a_lses, y_b, y_c, w_a, w_b, w_c, w_o, g_ffn, w_pq):
    t, d = x.shape
    tm = min(TOKEN_TILE, t)
    row = lambda width, col=0: pl.BlockSpec((tm, width), lambda i: (i, col))
    full = lambda a: pl.BlockSpec(a.shape, lambda i: (0, 0))
    g_ffn = g_ffn.reshape(1, d)
    group = lambda a: pl.BlockSpec((tm // (a.shape[1] // A_OUT_WIDTH), a.shape[1]), lambda i: (i, 0))
    in_specs = ([row(d)] + [row(d, c) for c in range(3)]
                + [group(a) for a in (*a_outs, *a_lses)] + [row(y_b.shape[1]), row(y_c.shape[1])]
                + [full(w_a), full(w_b), full(w_c), full(w_o), full(g_ffn), full(w_pq)])
    return pl.pallas_call(
        _merge_proj_kernel,
        grid=(t // tm,),
        in_specs=in_specs,
        out_specs=[row(d), row(d), row(d)],
        out_shape=[jax.ShapeDtypeStruct((t, d), F32), jax.ShapeDtypeStruct((t, d), BF16),
                   jax.ShapeDtypeStruct((t, d), F32)],
        scratch_shapes=[pltpu.VMEM((A_PAIRS, tm, LANES), F32)] * 6,
        compiler_params=pltpu.CompilerParams(dimension_semantics=("parallel",), vmem_limit_bytes=VMEM_LIMIT),
        name="merge_proj",
    )(x, gc_arr, gc_arr, gc_arr, *a_outs, *a_lses, y_b, y_c, w_a, w_b, w_c, w_o, g_ffn, w_pq)


def _oddeven_merge_sort_pairs(n):
    pairs = []
    p = 1
    while p < n:
        k = p
        while k >= 1:
            for j in range(k % p, n - k, 2 * k):
                for i in range(min(k, n - j - k)):
                    if (i + j) // (2 * p) == (i + j + k) // (2 * p):
                        pairs.append((i + j, i + j + k))
            k //= 2
        p *= 2
    return pairs


def _bitonic_merge_pairs(n):
    pairs = []
    stride = n // 2
    while stride >= 1:
        pairs += [(i, i + stride) for i in range(n) if not i & stride]
        stride //= 2
    return pairs


def _compare_exchange(vals, pairs):
    for i, j in pairs:
        a, b = vals[i], vals[j]
        if b is None:
            continue
        if a is None:
            vals[i], vals[j] = b, None
        else:
            vals[i], vals[j] = jnp.maximum(a, b), jnp.minimum(a, b)
    return vals


_SORT16 = _oddeven_merge_sort_pairs(PEER_TOPK)
_MERGE16 = _bitonic_merge_pairs(PEER_TOPK)
_CAND_PAIRS = [(a, b) for a in range(PEER_TOPK) for b in range(PEER_TOPK) if (a + 1) * (b + 1) <= PEER_TOPK]
_SORT64 = _oddeven_merge_sort_pairs(64)


def _top16_sorted(scores_t):
    slabs = [scores_t[v * SUBLANES:(v + 1) * SUBLANES, :] for v in range(PEER_KEYS // SUBLANES)]
    slabs = _compare_exchange(slabs, _SORT16)
    shift = SUBLANES // 2
    while shift >= 1:
        other = [pltpu.roll(s, shift, 0) for s in slabs]
        slabs = [jnp.maximum(slabs[r], other[PEER_TOPK - 1 - r]) for r in range(PEER_TOPK)]
        slabs = _compare_exchange(slabs, _MERGE16)
        shift //= 2
    return slabs


def _peer_select_kernel(qp_ref, keys_ref, theta_ref, s2_ref, c1_ref, e2_ref, s1_scr):
    tm = qp_ref.shape[0]
    sub = lax.broadcasted_iota(jnp.int32, (SUBLANES, tm), 0)
    tops = [[None] * PEER_TOPK for _ in range(2)]
    for h in range(PEER_HEADS):
        qh = qp_ref[:, h * LANES:(h + 1) * LANES].astype(BF16)
        for p in range(2):
            s_t = _nt_dot(keys_ref[2 * h + p], qh)
            if p == 0:
                s1_scr[h] = s_t
            else:
                for c in range(tm // LANES):
                    s2_ref[h, c] = s_t[:, c * LANES:(c + 1) * LANES]
            top = _top16_sorted(s_t)
            for r in range(PEER_TOPK):
                tops[p][r] = top[r] if h == 0 else jnp.where(sub == h, top[r], tops[p][r])
    cand = [tops[0][a] + tops[1][b] for a, b in _CAND_PAIRS] + [None] * (64 - len(_CAND_PAIRS))
    best = _compare_exchange(cand, _SORT64)[:PEER_TOPK]
    mx = best[0]
    z = functools.reduce(lambda u, w_: u + w_, [jnp.exp(b - mx) for b in best])
    thr = best[PEER_TOPK - 1]
    m1, m2 = tops[0][0], tops[1][0]

    def smallest_selected(s1_vals, thr_vals, t2_of, n_b):
        theta = jnp.full(s1_vals.shape, -NEG_INF, F32)
        for b in range(n_b):
            theta = jnp.where(s1_vals + t2_of(b) >= thr_vals, t2_of(b), theta)
        return theta

    n_best = 3
    short = PEER_TOPK // (n_best + 1)
    theta_best = [smallest_selected(tops[0][a], thr, lambda b: tops[1][b], PEER_TOPK) for a in range(n_best)]
    for h in range(PEER_HEADS):
        row = lambda a: a[h:h + 1, :]
        s1 = s1_scr[h]
        theta = smallest_selected(s1, row(thr), lambda b: row(tops[1][b]), short)
        for a in range(n_best):
            theta = jnp.where(s1 == row(tops[0][a]), row(theta_best[a]), theta)
        theta_ref[h] = theta
        c1_ref[h] = jnp.exp(s1 - row(m1)) / row(z)
        for c in range(tm // LANES):
            e2_ref[h, c] = jnp.exp(s2_ref[h, c] - m2[h:h + 1, c * LANES:(c + 1) * LANES])


def peer_select(qp, keys_padded):
    t = qp.shape[0]
    tm = min(PEER_TOKEN_TILE, t)
    first = jax.ShapeDtypeStruct((PEER_HEADS, PEER_KEYS, t), F32)
    first_spec = pl.BlockSpec((PEER_HEADS, PEER_KEYS, tm), lambda i: (0, 0, i))
    second = jax.ShapeDtypeStruct((PEER_HEADS, t // LANES, PEER_KEYS, LANES), F32)
    second_spec = pl.BlockSpec((PEER_HEADS, tm // LANES, PEER_KEYS, LANES), lambda i: (0, i, 0, 0))
    return pl.pallas_call(
        _peer_select_kernel,
        grid=(t // tm,),
        in_specs=[pl.BlockSpec((tm, PEER_HEADS * PEER_KEY_DIM), lambda i: (i, 0)),
                  pl.BlockSpec(keys_padded.shape, lambda i: (0, 0, 0))],
        out_specs=[first_spec, second_spec, first_spec, second_spec],
        out_shape=[first, second, first, second],
        scratch_shapes=[pltpu.VMEM((PEER_HEADS, PEER_KEYS, tm), F32)],
        compiler_params=pltpu.CompilerParams(dimension_semantics=("parallel",), vmem_limit_bytes=VMEM_LIMIT),
        name="peer_select",
    )(qp, keys_padded)


def _peer_dense_kernel(h_ref, u_ref, vt_ref, theta_ref, s2_ref, c1_ref, e2_ref, x_ref, g_ref,
                       o_ref, acc_scr, act_scr, wa_scr, *, final_norm):
    j = pl.program_id(1)
    tm = h_ref.shape[0]
    ec = u_ref.shape[0]
    rows_per_tile = ec // PEER_KEYS

    @pl.when(j == 0)
    def _():
        acc_scr[...] = jnp.zeros(acc_scr.shape, F32)

    n_chunks = tm // LANES
    halves = [range(0, n_chunks // 2), range(n_chunks // 2, n_chunks)]
    for half in halves:
        rows = slice(half[0] * LANES, (half[-1] + 1) * LANES)
        act = _gelu(_nt_dot(u_ref[...], h_ref[rows, :]))
        for n, tc in enumerate(half):
            act_scr[tc] = act[:, n * LANES:(n + 1) * LANES]

    def build(tc, carry):
        cols = pl.ds(pl.multiple_of(tc * LANES, LANES), LANES)
        for ii in range(rows_per_tile):
            rows = slice(ii * PEER_KEYS, (ii + 1) * PEER_KEYS)
            w = jnp.zeros((PEER_KEYS, LANES), F32)
            for h in range(PEER_HEADS):
                gate = e2_ref[h, tc] * c1_ref[h, ii:ii + 1, cols]
                w = w + jnp.where(s2_ref[h, tc] >= theta_ref[h, ii:ii + 1, cols], gate, 0.0)
            wa_scr[tc, rows, :] = (w * act_scr[tc, rows, :]).astype(BF16)
        return carry

    lax.fori_loop(0, n_chunks, build, 0)
    for half in halves:
        cols = slice(half[0] * LANES, (half[-1] + 1) * LANES)
        wa = jnp.concatenate([wa_scr[tc] for tc in half], axis=1)
        acc_scr[:, cols] += jnp.dot(vt_ref[...], wa, preferred_element_type=F32)

    @pl.when(j == pl.num_programs(1) - 1)
    def _():
        y = x_ref[...] + acc_scr[...].T
        if final_norm:
            y = _rms(y, g_ref[...])
        o_ref[...] = y


def peer_dense(h2, u_bf, vt_bf, theta, s2, c1, e2, x, g_final, final_norm):
    t, d = x.shape
    n_exp = u_bf.shape[0]
    tm = min(PEER_TOKEN_TILE, t)
    ec = PEER_EXPERT_TILE
    sel_spec = pl.BlockSpec((PEER_HEADS, tm // LANES, PEER_KEYS, LANES), lambda i, j: (0, i, 0, 0))
    key_spec = pl.BlockSpec((PEER_HEADS, ec // PEER_KEYS, tm), lambda i, j: (0, j, i))
    return pl.pallas_call(
        functools.partial(_peer_dense_kernel, final_norm=final_norm),
        grid=(t // tm, n_exp // ec),
        in_specs=[pl.BlockSpec((tm, d), lambda i, j: (i, 0)),
                  pl.BlockSpec((ec, d), lambda i, j: (j, 0)),
                  pl.BlockSpec((d, ec), lambda i, j: (0, j)),
                  key_spec, sel_spec, key_spec, sel_spec,
                  pl.BlockSpec((tm, d), lambda i, j: (i, 0)),
                  pl.BlockSpec((1, d), lambda i, j: (0, 0))],
        out_specs=pl.BlockSpec((tm, d), lambda i, j: (i, 0)),
        out_shape=jax.ShapeDtypeStruct((t, d), F32),
        scratch_shapes=[pltpu.VMEM((d, tm), F32), pltpu.VMEM((tm // LANES, ec, LANES), F32),
                        pltpu.VMEM((tm // LANES, ec, LANES), BF16)],
        compiler_params=pltpu.CompilerParams(dimension_semantics=("parallel", "arbitrary"),
                                             vmem_limit_bytes=VMEM_LIMIT),
        name="peer_dense",
    )(h2, u_bf, vt_bf, theta, s2, c1, e2, x, g_final.reshape(1, d))


def _t5_bucket_table(max_dist):
    n = np.arange(max_dist + 1)
    max_exact = N_BUCKETS // 2
    nf = np.maximum(n, max_exact).astype(np.float32)
    large = max_exact + (np.log(nf / np.float32(max_exact)) / np.float32(math.log(MAX_DISTANCE / max_exact))
                         * np.float32(N_BUCKETS - max_exact)).astype(np.int32)
    large = np.minimum(large, N_BUCKETS - 1)
    return np.where(n < max_exact, n, large).astype(np.int32)


def _skew(vec, rows):
    h, length = vec.shape
    return jnp.tile(vec, (1, rows))[:, :rows * (length - 1)].reshape(h, rows, length - 1)


def _banded_bias(bias_heads, dilation):
    w = A_BLOCK
    heads = bias_heads.shape[1]
    by_rel = bias_heads[_t5_bucket_table(w * dilation)[np.arange(w + 1) * dilation]].T.astype(F32)
    vec = jnp.concatenate([by_rel, jnp.full((heads, 3 * w - (w + 1)), NEG_INF, F32)], axis=1)
    return jnp.transpose(_skew(vec, 2 * w)[:, :, w:2 * w], (0, 2, 1))


def _diagonal_tiles(by_dist, tq, tk, n_tiles):
    heads = by_dist.shape[0]
    n_cols = (n_tiles - 1) * tk + tq
    mask = jnp.full((heads, tk), NEG_INF, F32)
    vec = jnp.concatenate([mask, by_dist, mask], axis=1)
    assert vec.shape[1] == n_cols + tk
    skew = _skew(vec, tk)
    tiles = jnp.stack([skew[:, :, i * tk:i * tk + tq] for i in range(n_tiles)], axis=1)
    return jnp.transpose(tiles, (0, 1, 3, 2))


def _causal_bias_tiles(bias_heads, tq, tk):
    n_tiles = 2 * -(-(MAX_DISTANCE - 1 + tk + tk) // (2 * tk))
    far = bias_heads[N_BUCKETS - 1]
    n_dist = (n_tiles - 2) * tk + tq
    by_dist = (bias_heads - far[None, :])[_t5_bucket_table(n_dist - 1)].T.astype(F32) * LOG2E
    return _diagonal_tiles(by_dist, tq, tk, n_tiles)


def _causal_mask_tiles(tq, tk):
    return _diagonal_tiles(jnp.zeros((1, tq), F32), tq, tk, 2)


def _rope_tables(seq):
    half = C_ROPE_DIM // 2
    inv = ROPE_BASE ** (-jnp.arange(half, dtype=F32) / half)
    ang = jnp.arange(seq, dtype=F32)[:, None] * inv[None, :]
    cos, sin = jnp.cos(ang), jnp.sin(ang)
    ones = jnp.ones((seq, C_NOPE_DIM), F32)
    zeros_n = jnp.zeros((seq, C_NOPE_DIM), F32)
    spare = jnp.zeros((seq, LANES - C_NOPE_DIM - C_ROPE_DIM), F32)
    cos_t = jnp.concatenate([ones, cos, cos, spare], axis=1)
    sin_t = jnp.concatenate([zeros_n, -sin, sin, spare], axis=1)
    return cos_t, sin_t


def _swap_halves(w):
    half = w.shape[-1] // 2
    return jnp.concatenate([w[..., half:], w[..., :half]], axis=-1)


def _prep_in_proj(w_in):
    d = w_in.shape[0]
    bounds = np.cumsum([A_COLS, B_QK_COLS, B_QK_COLS, B_V_COLS, C_Q_RANK, C_KV_RANK, C_ROPE_DIM])
    wa, wbq, wbk, wbv, wcq, wckv, wckr, wg = jnp.split(w_in, bounds, axis=1)
    zeros = lambda n: jnp.zeros((d, n), w_in.dtype)
    wa = wa.reshape(d, 3, A_HEADS, A_HEAD_DIM)
    qa = wa[:, 0] * (1.0 / math.sqrt(A_HEAD_DIM))
    q_blocks = []
    for h in range(A_HEADS):
        pad = zeros(A_HEAD_DIM)
        q_blocks += [qa[:, h], pad] if h % 2 == 0 else [pad, qa[:, h]]
    w_a = jnp.concatenate(q_blocks + [wa[:, 1].reshape(d, -1), wa[:, 2].reshape(d, -1)], axis=1)
    qb = wbq.reshape(d, B_HEADS, 2, B_QK_DIM) * (LOG2E / math.sqrt(B_QK_DIM))
    q_blocks = []
    for h in range(B_HEADS):
        pad = zeros(B_QK_DIM)
        q_blocks += [qb[:, h, 0], pad, pad, qb[:, h, 1]]
    w_b = jnp.concatenate(q_blocks + [wbk, wbv], axis=1)
    w_gc = jnp.concatenate([wg, wcq, wckv, zeros(C_NOPE_DIM), wckr, _swap_halves(wckr)], axis=1)
    return w_a.astype(BF16), w_b.astype(BF16), w_gc.astype(BF16)


def _prep_mla_weights(w_uq, w_ukv):
    rq = w_uq.shape[0]
    wq = w_uq.reshape(rq, C_HEADS, C_NOPE_DIM + C_ROPE_DIM)
    q_rope = wq[..., C_NOPE_DIM:]
    wq = jnp.concatenate([wq, _swap_halves(q_rope)], axis=-1).reshape(rq, C_QK_WIDTH)
    rkv = w_ukv.shape[0]
    wkv = w_ukv.reshape(rkv, C_HEADS, C_NOPE_DIM + C_V_DIM)
    wk = jnp.concatenate([wkv[..., :C_NOPE_DIM], jnp.zeros((rkv, C_HEADS, LANES - C_NOPE_DIM), w_ukv.dtype)],
                         axis=-1).reshape(rkv, C_QK_WIDTH)
    wv = wkv[..., C_NOPE_DIM:].reshape(rkv, C_V_WIDTH)
    return wq.astype(BF16), wk.astype(BF16), wv.astype(BF16)


def _prep_peer_keys(sub_keys):
    half = PEER_KEY_DIM // 2
    z = jnp.zeros(sub_keys.shape[:1] + sub_keys.shape[2:], sub_keys.dtype)
    first = jnp.concatenate([sub_keys[:, 0], z], axis=-1)
    second = jnp.concatenate([z, sub_keys[:, 1]], axis=-1)
    assert first.shape[-1] == LANES and half * 2 == LANES
    return jnp.stack([first, second], axis=1).reshape(2 * PEER_HEADS, PEER_KEYS, LANES).astype(BF16)


def kernel(x, rel_bias, w_in, g_mix, w_uq, g_cq, w_ukv, g_ckv, lam_q1, lam_k1, lam_q2, lam_k2, g_subln,
           w_branch_a, w_branch_b, w_branch_c, w_out, g_ffn, w_peer_q, peer_sub_keys, peer_u, peer_v, g_final):
    bsz, seq, d = x.shape
    t = bsz * seq
    depth = w_in.shape[0]
    bias_a = rel_bias[:, :A_HEADS]
    bias_b = rel_bias[:, A_HEADS:]
    a_biases = [_banded_bias(bias_a, dil) for _, dil in A_CONFIGS]
    b_bias = _causal_bias_tiles(bias_b, FLASH_Q_TILE, FLASH_K_TILE)
    c_mask = _causal_mask_tiles(FLASH_Q_TILE, FLASH_K_TILE)
    cos_t, sin_t = _rope_tables(seq)
    gate_blocks = 3 * d // C_LATENT_WIDTH

    xf = x.reshape(t, d)
    for l in range(depth):
        w_a, w_b, w_gc = _prep_in_proj(w_in[l])
        dilations = tuple(dil for _, dil in A_CONFIGS)
        views_a = norm_matmul_dilated(xf, g_mix[l], w_a, dilations, A_Q_WIDTH)
        qkv_b = norm_matmul(xf, g_mix[l], w_b, BF16).reshape(bsz, seq, -1)
        gc = norm_matmul(xf, g_mix[l], w_gc, F32)

        a_outs, a_lses = [], []
        for dil, view, bias in zip(dilations, views_a, a_biases):
            o, lse = local_attn(view.reshape(bsz, seq // dil, -1), bias, dil)
            a_outs.append(o.reshape(t // dil, -1))
            a_lses.append(lse.reshape(t // dil, -1))

        lam_init = 0.8 - 0.6 * math.exp(-0.3 * l)
        pad = lambda v: jnp.pad(v.astype(F32), (0, LANES - v.shape[0]))
        lam_prm = jnp.stack([pad(lam_q1[l]), pad(lam_k1[l]), pad(lam_q2[l]), pad(lam_k2[l]),
                             jnp.full((LANES,), lam_init, F32)]
                            + [jnp.zeros((LANES,), F32)] * (SUBLANES - 5))
        y_b = diff_attn(qkv_b, b_bias, lam_prm, g_subln[l]).reshape(t, -1)

        wq, wk, wv = _prep_mla_weights(w_uq[l], w_ukv[l])
        q_c, k_c, v_c = mla_prep(gc, gate_blocks, g_cq[l], g_ckv[l], wq, wk, wv, cos_t, sin_t, seq)
        y_c = mla_attn(q_c.reshape(bsz, seq, -1), k_c.reshape(bsz, seq, -1), v_c.reshape(bsz, seq, -1),
                       c_mask).reshape(t, -1)

        xf, h2, qp = merge_proj(xf, gc, a_outs, a_lses, y_b, y_c,
                                w_branch_a[l].astype(BF16), w_branch_b[l].astype(BF16),
                                w_branch_c[l].astype(BF16), w_out[l].astype(BF16), g_ffn[l],
                                w_peer_q[l].astype(BF16))

        theta, s2, c1, e2 = peer_select(qp, _prep_peer_keys(peer_sub_keys[l]))
        xf = peer_dense(h2, peer_u[l].astype(BF16), peer_v[l].T.astype(BF16), theta, s2, c1, e2, xf,
                        g_final, final_norm=(l == depth - 1))
    return xf.reshape(bsz, seq, d)
```

```python
import functools
import math

import numpy as np
import jax
import jax.numpy as jnp
from jax import lax
from jax.experimental import pallas as pl
from jax.experimental.pallas import tpu as pltpu

D_MODEL = 1024
DEPTH = 4
A_HEADS = 6
A_HEAD_DIM = 64
A_CONFIGS = ((128, 1), (512, 4), (2048, 16))
B_HEADS = 4
B_QK_DIM = 64
B_V_DIM = 128
C_HEADS = 6
C_NOPE_DIM = 64
C_ROPE_DIM = 32
C_V_DIM = 64
C_Q_RANK = 256
C_KV_RANK = 128
ROPE_BASE = 10000.0
N_BUCKETS = 32
MAX_DISTANCE = 2048
PEER_HEADS = 8
PEER_KEYS = 128
PEER_EXPERTS = PEER_KEYS * PEER_KEYS
PEER_KEY_DIM = 128
PEER_TOPK = 16
RMS_EPS = 1e-6
NEG_INF = -1e30
LOG2E = math.log2(math.e)

A_COLS = 3 * A_HEADS * A_HEAD_DIM
B_QK_COLS = B_HEADS * 2 * B_QK_DIM
B_V_COLS = B_HEADS * B_V_DIM

LANES = 128
SUBLANES = 8
VMEM_LIMIT = 56 * 1024 * 1024

F32 = jnp.float32
BF16 = jnp.bfloat16

TOKEN_TILE = 512
PROJ_TOKEN_TILE = 2048
PROJ_COL_TILE = 512
A_BLOCK = 128
A_BLOCKS_PER_STEP = 4
FLASH_Q_TILE = 512
FLASH_K_TILE = 256
FLASH_ROW_BLOCK = 128
FLASH_UNROLLS = (8, 4, 2)
PEER_TOKEN_TILE = 512
PEER_EXPERT_TILE = 2048


def _nt_dot(a, b):
    return lax.dot_general(a, b, (((1,), (1,)), ((), ())), preferred_element_type=F32)


def _gelu(x):
    return 0.5 * x * (1.0 + lax.erf(x * math.sqrt(0.5)))


def _rms(x, g):
    return x * lax.rsqrt(jnp.mean(x * x, axis=-1, keepdims=True) + RMS_EPS) * g


def _norm_matmul_kernel(x_ref, g_ref, w_ref, o_ref, h_scr):
    @pl.when(pl.program_id(1) == 0)
    def _():
        h_scr[...] = _rms(x_ref[...], g_ref[...]).astype(h_scr.dtype)

    o_ref[...] = jnp.dot(h_scr[...], w_ref[...], preferred_element_type=F32).astype(o_ref.dtype)


def norm_matmul(x, g, w, out_dtype):
    t, k = x.shape
    n = w.shape[1]
    tm, tn = min(PROJ_TOKEN_TILE, t), PROJ_COL_TILE
    return pl.pallas_call(
        _norm_matmul_kernel,
        grid=(t // tm, n // tn),
        in_specs=[pl.BlockSpec((tm, k), lambda i, j: (i, 0)),
                  pl.BlockSpec((1, k), lambda i, j: (0, 0)),
                  pl.BlockSpec((k, tn), lambda i, j: (0, j))],
        out_specs=pl.BlockSpec((tm, tn), lambda i, j: (i, j)),
        out_shape=jax.ShapeDtypeStruct((t, n), out_dtype),
        scratch_shapes=[pltpu.VMEM((tm, k), BF16)],
        compiler_params=pltpu.CompilerParams(dimension_semantics=("parallel", "arbitrary"),
                                             vmem_limit_bytes=VMEM_LIMIT),
        name="norm_matmul",
    )(x, g.reshape(1, k), w)


def _norm_matmul_dilated_kernel(x_ref, g_ref, w_ref, *rest, dilations):
    out_refs, (h_scr, res_scr) = rest[:len(dilations)], rest[len(dilations):]
    @pl.when(pl.program_id(1) == 0)
    def _():
        h_scr[...] = _rms(x_ref[...], g_ref[...]).astype(h_scr.dtype)

    res = jnp.dot(h_scr[...], w_ref[...], preferred_element_type=F32)
    n_lane_blocks, tm, _ = res_scr.shape
    tn = n_lane_blocks * LANES
    for c in range(n_lane_blocks):
        res_scr[c] = res[:, c * LANES:(c + 1) * LANES]
    for d, o_ref in zip(dilations, out_refs):
        for r in range(d):
            for c in range(n_lane_blocks):
                cols = slice(r * tn + c * LANES, r * tn + (c + 1) * LANES)
                o_ref[:, cols] = res_scr[c, pl.ds(r, tm // d, stride=d), :].astype(o_ref.dtype)


def norm_matmul_dilated(x, g, w, dilations, tn):
    t, k = x.shape
    n = w.shape[1]
    tm = min(PROJ_TOKEN_TILE, t)
    assert n % tn == 0 and all(tm % (d * SUBLANES) == 0 for d in dilations)
    return pl.pallas_call(
        functools.partial(_norm_matmul_dilated_kernel, dilations=dilations),
        grid=(t // tm, n // tn),
        in_specs=[pl.BlockSpec((tm, k), lambda i, j: (i, 0)),
                  pl.BlockSpec((1, k), lambda i, j: (0, 0)),
                  pl.BlockSpec((k, tn), lambda i, j: (0, j))],
        out_specs=[pl.BlockSpec((tm // d, d * tn), lambda i, j: (i, j)) for d in dilations],
        out_shape=[jax.ShapeDtypeStruct((t // d, d * n), BF16) for d in dilations],
        scratch_shapes=[pltpu.VMEM((tm, k), BF16), pltpu.VMEM((tn // LANES, tm, LANES), F32)],
        compiler_params=pltpu.CompilerParams(dimension_semantics=("parallel", "arbitrary"),
                                             vmem_limit_bytes=VMEM_LIMIT),
        name="norm_matmul_dilated",
    )(x, g.reshape(1, k), w)


A_Q_WIDTH = A_HEADS * LANES
A_PAIRS = A_HEADS // 2
A_KV_WIDTH = 2 * A_PAIRS * LANES
A_OUT_WIDTH = A_PAIRS * LANES


def _local_attn_kernel(q_ref, kvc_ref, kvp_ref, bias_ref, o_ref, lse_ref, kv_scr, *, nblk):
    w = A_BLOCK
    first_class_block = pl.program_id(2) == 0
    kv_scr[0:w, :] = kvp_ref[0]
    kv_scr[w:, :] = kvc_ref[0]
    lane = lax.broadcasted_iota(jnp.int32, (w, LANES), 1)
    col = lax.broadcasted_iota(jnp.int32, (w, 2 * w), 1)
    no_prev = jnp.where(col < w, jnp.where(first_class_block, NEG_INF, 0.0), 0.0)
    ones = jnp.ones((2 * w, LANES), BF16)
    for i in range(nblk):
        rows = slice(i * w, (i + 1) * w)
        for p in range(A_PAIRS):
            k_blk = kv_scr[i * w:(i + 2) * w, p * LANES:(p + 1) * LANES]
            v_blk = kv_scr[i * w:(i + 2) * w, (A_PAIRS + p) * LANES:(A_PAIRS + p + 1) * LANES]
            heads = (2 * p, 2 * p + 1)
            q = jnp.concatenate([q_ref[0, rows, hd * LANES:(hd + 1) * LANES] for hd in heads], axis=0)
            bias = jnp.concatenate([bias_ref[hd] for hd in heads], axis=0)
            if i == 0:
                bias = bias + jnp.concatenate([no_prev, no_prev], axis=0)
            s = _nt_dot(q, k_blk) + bias
            m = jnp.max(s, axis=-1, keepdims=True)
            pr = jnp.exp(s - m)
            pv = jnp.dot(pr.astype(BF16), jnp.concatenate([v_blk, ones], axis=1), preferred_element_type=F32)
            l = pv[:, LANES:]
            o = pv[:, :LANES] / l
            lse = m + jnp.log(l)
            o_ref[0, rows, p * LANES:(p + 1) * LANES] = jnp.where(lane < A_HEAD_DIM, o[:w], o[w:])
            lse_ref[0, rows, p * LANES:(p + 1) * LANES] = jnp.where(lane < A_HEAD_DIM, lse[:w], lse[w:])


def local_attn(view, bias, dilation):
    bsz, rows, width = view.shape
    d = dilation
    w = A_BLOCK
    nblk = min(A_BLOCKS_PER_STEP, rows // w)
    assert rows % (w * nblk) == 0 and width == d * (A_Q_WIDTH + A_KV_WIDTH) and A_Q_WIDTH == A_KV_WIDTH
    grid = (bsz, d, rows // (w * nblk))
    out_shape = jax.ShapeDtypeStruct((bsz, rows, d * A_OUT_WIDTH), F32)
    o, lse = pl.pallas_call(
        functools.partial(_local_attn_kernel, nblk=nblk),
        grid=grid,
        in_specs=[pl.BlockSpec((1, w * nblk, A_Q_WIDTH), lambda b, r, n: (b, n, r)),
                  pl.BlockSpec((1, w * nblk, A_KV_WIDTH), lambda b, r, n: (b, n, d + r)),
                  pl.BlockSpec((1, w, A_KV_WIDTH), lambda b, r, n: (b, jnp.maximum(n * nblk - 1, 0), d + r)),
                  pl.BlockSpec((A_HEADS, w, 2 * w), lambda b, r, n: (0, 0, 0))],
        out_specs=[pl.BlockSpec((1, w * nblk, A_OUT_WIDTH), lambda b, r, n: (b, n, r)),
                   pl.BlockSpec((1, w * nblk, A_OUT_WIDTH), lambda b, r, n: (b, n, r))],
        out_shape=[out_shape, out_shape],
        scratch_shapes=[pltpu.VMEM((w * (nblk + 1), A_KV_WIDTH), BF16)],
        compiler_params=pltpu.CompilerParams(dimension_semantics=("parallel", "parallel", "arbitrary")),
        name=f"local_attn_d{d}",
    )(view, view, view, bias)
    return o, lse


def _flash_pair_kernel(*refs, tq, tk, n_near, shared_k, mode):
    if shared_k:
        qa_ref, qb_ref, ka_ref, v_ref, bias_ref = refs[:5]
        kb_ref = ka_ref
        rest = refs[5:]
    else:
        qa_ref, qb_ref, ka_ref, kb_ref, v_ref, bias_ref = refs[:6]
        rest = refs[6:]
    if mode == "diff":
        prm_ref, gsub_ref, o_ref, m_scr, alpha_scr, acc_scr, s0_scr, s1_scr, p_scr = rest
    else:
        o_ref, m_scr, alpha_scr, acc_scr, s0_scr, s1_scr, p_scr = rest

    qi = pl.program_id(2)
    rb = FLASH_ROW_BLOCK
    m_scr[...] = jnp.full(m_scr.shape, NEG_INF, F32)
    acc_scr[...] = jnp.zeros(acc_scr.shape, F32)
    ones = jnp.ones((tk, LANES), BF16)
    last_kt = 2 * qi + 1

    def scores(kt, s_scr):
        koff = pl.multiple_of(jnp.minimum(kt, last_kt) * tk, tk)
        ka = ka_ref[0, pl.ds(koff, tk), :]
        if shared_k:
            s_scr[...] = _nt_dot(jnp.concatenate([qa_ref[0], qb_ref[0]], axis=0), ka)
        else:
            s_scr[0:tq, :] = _nt_dot(qa_ref[0], ka)
            s_scr[tq:, :] = _nt_dot(qb_ref[0], kb_ref[0, pl.ds(koff, tk), :])

    def accumulate(kt, s_scr, near):
        koff = pl.multiple_of(kt * tk, tk)
        for r in range(2 * tq // rb):
            rows = slice(r * rb, (r + 1) * rb)
            s = s_scr[rows, :]
            if near:
                b0 = (r * rb) % tq
                s = s + bias_ref[0, last_kt - kt, b0:b0 + rb, :]
            m_prev = m_scr[rows, :]
            m_new = jnp.maximum(m_prev, jnp.max(s, axis=-1, keepdims=True))
            alpha_scr[rows, :] = jnp.exp2(m_prev - m_new)
            m_scr[rows, :] = m_new
            p_scr[rows, :] = jnp.exp2(s - jnp.concatenate([m_new] * (tk // LANES), axis=1)).astype(BF16)
        v_ones = jnp.concatenate([v_ref[0, pl.ds(koff, tk), :], ones], axis=1)
        pv = jnp.dot(p_scr[...], v_ones, preferred_element_type=F32)
        alpha = alpha_scr[...]
        acc_scr[...] = jnp.concatenate([alpha, alpha], axis=1) * acc_scr[...] + pv

    n_tiles = 2 * qi + 2
    n_far = jnp.maximum(n_tiles - n_near, 0)

    def run(first, count, near):
        def unrolled(kt, steps):
            for step in range(steps):
                nxt, cur = (s1_scr, s0_scr) if step % 2 == 0 else (s0_scr, s1_scr)
                scores(kt + step + 1, nxt)
                accumulate(kt + step, cur, near)

        done = 0
        for steps in FLASH_UNROLLS:
            def body(i, carry, steps=steps, done=done):
                unrolled(first + done + steps * i, steps)
                return carry

            trips = (count - done) // steps
            lax.fori_loop(0, trips, body, 0)
            done = done + trips * steps

    scores(0, s0_scr)
    run(0, n_far, False)
    run(n_far, n_tiles - n_far, True)

    o = acc_scr[:, :LANES] / acc_scr[:, LANES:]
    oa, ob = o[:tq], o[tq:]
    if mode == "diff":
        prm = prm_ref[...]
        lam_init = prm[4:5, 0:1]
        lam = (jnp.exp(jnp.sum(prm[0:1] * prm[1:2], axis=-1, keepdims=True))
               - jnp.exp(jnp.sum(prm[2:3] * prm[3:4], axis=-1, keepdims=True)) + lam_init)
        d = oa - lam * ob
        o_ref[0] = (_rms(d, gsub_ref[...]) * (1.0 - lam_init)).astype(o_ref.dtype)
    else:
        lane = lax.broadcasted_iota(jnp.int32, oa.shape, 1)
        o_ref[0] = jnp.where(lane < C_V_DIM, oa, ob).astype(o_ref.dtype)


def _flash_pair_call(mode, q_arr, k_arr, v_arr, bias, groups, col_maps, out_cols, extra):
    bsz, seq, _ = q_arr.shape
    tq, tk = FLASH_Q_TILE, FLASH_K_TILE
    assert seq % tq == 0 and tq == 2 * tk
    n_near = bias.shape[1]
    assert n_near % 2 == 0
    qa_c, qb_c, ka_c, kb_c, v_c = col_maps
    shared_k = kb_c is None
    q_spec = lambda cm: pl.BlockSpec((1, tq, LANES), lambda b, g, i: (b, i, cm(g)))
    kv_spec = lambda cm: pl.BlockSpec((1, seq, LANES), lambda b, g, i: (b, 0, cm(g)))
    in_specs = [q_spec(qa_c), q_spec(qb_c), kv_spec(ka_c)]
    args = [q_arr, q_arr, k_arr]
    if not shared_k:
        in_specs.append(kv_spec(kb_c))
        args.append(k_arr)
    in_specs.append(kv_spec(v_c))
    args.append(v_arr)
    bias_group = (lambda g: g) if bias.shape[0] > 1 else (lambda g: 0)
    in_specs.append(pl.BlockSpec((1, n_near, tq, tk), lambda b, g, i: (bias_group(g), 0, 0, 0)))
    args.append(bias)
    for e in extra:
        in_specs.append(pl.BlockSpec(e.shape, lambda b, g, i: (0, 0)))
        args.append(e)
    return pl.pallas_call(
        functools.partial(_flash_pair_kernel, tq=tq, tk=tk, n_near=n_near, shared_k=shared_k, mode=mode),
        grid=(bsz, groups, seq // tq),
        in_specs=in_specs,
        out_specs=pl.BlockSpec((1, tq, LANES), lambda b, g, i: (b, i, g)),
        out_shape=jax.ShapeDtypeStruct((bsz, seq, out_cols), BF16),
        scratch_shapes=[pltpu.VMEM((2 * tq, LANES), F32), pltpu.VMEM((2 * tq, LANES), F32),
                        pltpu.VMEM((2 * tq, 2 * LANES), F32), pltpu.VMEM((2 * tq, tk), F32),
                        pltpu.VMEM((2 * tq, tk), F32), pltpu.VMEM((2 * tq, tk), BF16)],
        compiler_params=pltpu.CompilerParams(dimension_semantics=("parallel", "parallel", "arbitrary"),
                                             vmem_limit_bytes=VMEM_LIMIT),
        name=f"flash_pair_{mode}",
    )(*args)


B_Q_BLOCKS = 2 * B_HEADS
B_WIDTH_PADDED = (B_Q_BLOCKS + 2 * B_HEADS) * LANES


def diff_attn(qkv, bias, lam_prm, g_subln):
    return _flash_pair_call(
        "diff", qkv, qkv, qkv, bias, B_HEADS,
        (lambda g: 2 * g, lambda g: 2 * g + 1, lambda g: B_Q_BLOCKS + g, None,
         lambda g: B_Q_BLOCKS + B_HEADS + g),
        B_HEADS * B_V_DIM, [lam_prm, g_subln.reshape(1, B_V_DIM)])


def mla_attn(q, k, v, mask):
    return _flash_pair_call(
        "mla", q, k, v, mask, C_HEADS // 2,
        (lambda g: 2 * g, lambda g: 2 * g + 1, lambda g: 2 * g, lambda g: 2 * g + 1, lambda g: g),
        C_HEADS * C_V_DIM, [])


C_QK_WIDTH = C_HEADS * LANES
C_V_WIDTH = C_HEADS * C_V_DIM
C_LATENT_WIDTH = C_Q_RANK + C_KV_RANK + LANES
ROPE_SHIFT = LANES - C_ROPE_DIM


def _mla_prep_kernel(c_ref, gq_ref, gkv_ref, wq_ref, wk_ref, wv_ref, cos_ref, sin_ref, q_ref, k_ref, v_ref):
    c = c_ref[...]
    cos = cos_ref[...]
    sin = sin_ref[...]

    def rope(blk):
        return blk * cos + pltpu.roll(blk, ROPE_SHIFT, 1) * sin

    cq = _rms(c[:, :C_Q_RANK], gq_ref[...]).astype(BF16)
    ckv = _rms(c[:, C_Q_RANK:C_Q_RANK + C_KV_RANK], gkv_ref[...]).astype(BF16)
    q_all = jnp.dot(cq, wq_ref[...], preferred_element_type=F32)
    k_all = jnp.dot(ckv, wk_ref[...], preferred_element_type=F32)
    k_rope = rope(c[:, C_Q_RANK + C_KV_RANK:])
    scale = LOG2E / math.sqrt(C_NOPE_DIM + C_ROPE_DIM)
    for h in range(C_HEADS):
        cols = slice(h * LANES, (h + 1) * LANES)
        q_ref[:, cols] = (rope(q_all[:, cols]) * scale).astype(q_ref.dtype)
        k_ref[:, cols] = (k_all[:, cols] + k_rope).astype(k_ref.dtype)
    v_ref[...] = jnp.dot(ckv, wv_ref[...], preferred_element_type=F32).astype(v_ref.dtype)


def mla_prep(c_lat, c_col_block, g_cq, g_ckv, w_q, w_k, w_v, cos_t, sin_t, seq):
    t = c_lat.shape[0]
    tm = min(TOKEN_TILE, seq)
    per_seq = seq // tm
    full = lambda a: pl.BlockSpec(a.shape, lambda i: (0, 0))
    g_cq = g_cq.reshape(1, -1)
    g_ckv = g_ckv.reshape(1, -1)
    return pl.pallas_call(
        _mla_prep_kernel,
        grid=(t // tm,),
        in_specs=[pl.BlockSpec((tm, C_LATENT_WIDTH), lambda i: (i, c_col_block)),
                  full(g_cq), full(g_ckv), full(w_q), full(w_k), full(w_v),
                  pl.BlockSpec((tm, LANES), lambda i: (i % per_seq, 0)),
                  pl.BlockSpec((tm, LANES), lambda i: (i % per_seq, 0))],
        out_specs=[pl.BlockSpec((tm, C_QK_WIDTH), lambda i: (i, 0)),
                   pl.BlockSpec((tm, C_QK_WIDTH), lambda i: (i, 0)),
                   pl.BlockSpec((tm, C_V_WIDTH), lambda i: (i, 0))],
        out_shape=[jax.ShapeDtypeStruct((t, C_QK_WIDTH), BF16), jax.ShapeDtypeStruct((t, C_QK_WIDTH), BF16),
                   jax.ShapeDtypeStruct((t, C_V_WIDTH), BF16)],
        compiler_params=pltpu.CompilerParams(dimension_semantics=("parallel",)),
        name="mla_prep",
    )(c_lat, g_cq, g_ckv, w_q, w_k, w_v, cos_t, sin_t)


def _merge_proj_kernel(x_ref, ga_ref, gb_ref, gc_ref, o1_ref, o2_ref, o3_ref, l1_ref, l2_ref, l3_ref,
                       yb_ref, yc_ref, wa_ref, wb_ref, wc_ref, wo_ref, gffn_ref, wpq_ref,
                       xo_ref, h2_ref, qp_ref, *token_order_scr):
    tm = x_ref.shape[0]

    def token_order(ref, scr):
        d = ref.shape[1] // A_OUT_WIDTH
        if d == 1:
            return ref[...]
        for r in range(d):
            for c in range(A_PAIRS):
                cols = slice(r * A_OUT_WIDTH + c * LANES, r * A_OUT_WIDTH + (c + 1) * LANES)
                scr[c, pl.ds(r, tm // d, stride=d), :] = ref[:, cols]
        return jnp.concatenate([scr[c] for c in range(A_PAIRS)], axis=1)

    o1, o2, o3, l1, l2, l3 = [token_order(ref, scr) for ref, scr in
                              zip((o1_ref, o2_ref, o3_ref, l1_ref, l2_ref, l3_ref), token_order_scr)]
    m = jnp.maximum(jnp.maximum(l1, l2), l3)
    e1, e2, e3 = jnp.exp(l1 - m), jnp.exp(l2 - m), jnp.exp(l3 - m)
    den = e1 + e2 + e3
    ya = (e1 / den) * o1 + (e2 / den) * o2 + (e3 / den) * o3
    pa = jnp.dot(ya.astype(BF16), wa_ref[...], preferred_element_type=F32)
    pb = jnp.dot(yb_ref[...], wb_ref[...], preferred_element_type=F32)
    pc = jnp.dot(yc_ref[...], wc_ref[...], preferred_element_type=F32)
    mix = (jax.nn.sigmoid(ga_ref[...]) * pa + jax.nn.sigmoid(gb_ref[...]) * pb
           + jax.nn.sigmoid(gc_ref[...]) * pc)
    x_new = x_ref[...] + jnp.dot(mix.astype(BF16), wo_ref[...], preferred_element_type=F32)
    xo_ref[...] = x_new
    h2 = _rms(x_new, gffn_ref[...]).astype(BF16)
    h2_ref[...] = h2
    qp_ref[...] = jnp.dot(h2, wpq_ref[...], preferred_element_type=F32)


def merge_proj(x, gc_arr, a_outs, a_lses, y_b, y_c, w_a, w_b, w_c, w_o, g_ffn, w_pq):
    t, d = x.shape
    tm = min(TOKEN_TILE, t)
    row = lambda width, col=0: pl.BlockSpec((tm, width), lambda i: (i, col))
    full = lambda a: pl.BlockSpec(a.shape, lambda i: (0, 0))
    g_ffn = g_ffn.reshape(1, d)
    group = lambda a: pl.BlockSpec((tm // (a.shape[1] // A_OUT_WIDTH), a.shape[1]), lambda i: (i, 0))
    in_specs = ([row(d)] + [row(d, c) for c in range(3)]
                + [group(a) for a in (*a_outs, *a_lses)] + [row(y_b.shape[1]), row(y_c.shape[1])]
                + [full(w_a), full(w_b), full(w_c), full(w_o), full(g_ffn), full(w_pq)])
    return pl.pallas_call(
        _merge_proj_kernel,
        grid=(t // tm,),
        in_specs=in_specs,
        out_specs=[row(d), row(d), row(d)],
        out_shape=[jax.ShapeDtypeStruct((t, d), F32), jax.ShapeDtypeStruct((t, d), BF16),
                   jax.ShapeDtypeStruct((t, d), F32)],
        scratch_shapes=[pltpu.VMEM((A_PAIRS, tm, LANES), F32)] * 6,
        compiler_params=pltpu.CompilerParams(dimension_semantics=("parallel",), vmem_limit_bytes=VMEM_LIMIT),
        name="merge_proj",
    )(x, gc_arr, gc_arr, gc_arr, *a_outs, *a_lses, y_b, y_c, w_a, w_b, w_c, w_o, g_ffn, w_pq)


def _oddeven_merge_sort_pairs(n):
    pairs = []
    p = 1
    while p < n:
        k = p
        while k >= 1:
            for j in range(k % p, n - k, 2 * k):
                for i in range(min(k, n - j - k)):
                    if (i + j) // (2 * p) == (i + j + k) // (2 * p):
                        pairs.append((i + j, i + j + k))
            k //= 2
        p *= 2
    return pairs


def _bitonic_merge_pairs(n):
    pairs = []
    stride = n // 2
    while stride >= 1:
        pairs += [(i, i + stride) for i in range(n) if not i & stride]
        stride //= 2
    return pairs


def _compare_exchange(vals, pairs):
    for i, j in pairs:
        a, b = vals[i], vals[j]
        if b is None:
            continue
        if a is None:
            vals[i], vals[j] = b, None
        else:
            vals[i], vals[j] = jnp.maximum(a, b), jnp.minimum(a, b)
    return vals


_SORT16 = _oddeven_merge_sort_pairs(PEER_TOPK)
_MERGE16 = _bitonic_merge_pairs(PEER_TOPK)
_CAND_PAIRS = [(a, b) for a in range(PEER_TOPK) for b in range(PEER_TOPK) if (a + 1) * (b + 1) <= PEER_TOPK]
_SORT64 = _oddeven_merge_sort_pairs(64)


def _top16_sorted(scores_t):
    slabs = [scores_t[v * SUBLANES:(v + 1) * SUBLANES, :] for v in range(PEER_KEYS // SUBLANES)]
    slabs = _compare_exchange(slabs, _SORT16)
    shift = SUBLANES // 2
    while shift >= 1:
        other = [pltpu.roll(s, shift, 0) for s in slabs]
        slabs = [jnp.maximum(slabs[r], other[PEER_TOPK - 1 - r]) for r in range(PEER_TOPK)]
        slabs = _compare_exchange(slabs, _MERGE16)
        shift //= 2
    return slabs


def _peer_select_kernel(qp_ref, keys_ref, theta_ref, s2_ref, c1_ref, e2_ref, s1_scr):
    tm = qp_ref.shape[0]
    sub = lax.broadcasted_iota(jnp.int32, (SUBLANES, tm), 0)
    tops = [[None] * PEER_TOPK for _ in range(2)]
    for h in range(PEER_HEADS):
        qh = qp_ref[:, h * LANES:(h + 1) * LANES].astype(BF16)
        for p in range(2):
            s_t = _nt_dot(keys_ref[2 * h + p], qh)
            if p == 0:
                s1_scr[h] = s_t
            else:
                for c in range(tm // LANES):
                    s2_ref[h, c] = s_t[:, c * LANES:(c + 1) * LANES]
            top = _top16_sorted(s_t)
            for r in range(PEER_TOPK):
                tops[p][r] = top[r] if h == 0 else jnp.where(sub == h, top[r], tops[p][r])
    cand = [tops[0][a] + tops[1][b] for a, b in _CAND_PAIRS] + [None] * (64 - len(_CAND_PAIRS))
    best = _compare_exchange(cand, _SORT64)[:PEER_TOPK]
    mx = best[0]
    z = functools.reduce(lambda u, w_: u + w_, [jnp.exp(b - mx) for b in best])
    thr = best[PEER_TOPK - 1]
    m1, m2 = tops[0][0], tops[1][0]

    def smallest_selected(s1_vals, thr_vals, t2_of, n_b):
        theta = jnp.full(s1_vals.shape, -NEG_INF, F32)
        for b in range(n_b):
            theta = jnp.where(s1_vals + t2_of(b) >= thr_vals, t2_of(b), theta)
        return theta

    n_best = 3
    short = PEER_TOPK // (n_best + 1)
    theta_best = [smallest_selected(tops[0][a], thr, lambda b: tops[1][b], PEER_TOPK) for a in range(n_best)]
    for h in range(PEER_HEADS):
        row = lambda a: a[h:h + 1, :]
        s1 = s1_scr[h]
        theta = smallest_selected(s1, row(thr), lambda b: row(tops[1][b]), short)
        for a in range(n_best):
            theta = jnp.where(s1 == row(tops[0][a]), row(theta_best[a]), theta)
        theta_ref[h] = theta
        c1_ref[h] = jnp.exp(s1 - row(m1)) / row(z)
        for c in range(tm // LANES):
            e2_ref[h, c] = jnp.exp(s2_ref[h, c] - m2[h:h + 1, c * LANES:(c + 1) * LANES])


def peer_select(qp, keys_padded):
    t = qp.shape[0]
    tm = min(PEER_TOKEN_TILE, t)
    first = jax.ShapeDtypeStruct((PEER_HEADS, PEER_KEYS, t), F32)
    first_spec = pl.BlockSpec((PEER_HEADS, PEER_KEYS, tm), lambda i: (0, 0, i))
    second = jax.ShapeDtypeStruct((PEER_HEADS, t // LANES, PEER_KEYS, LANES), F32)
    second_spec = pl.BlockSpec((PEER_HEADS, tm // LANES, PEER_KEYS, LANES), lambda i: (0, i, 0, 0))
    return pl.pallas_call(
        _peer_select_kernel,
        grid=(t // tm,),
        in_specs=[pl.BlockSpec((tm, PEER_HEADS * PEER_KEY_DIM), lambda i: (i, 0)),
                  pl.BlockSpec(keys_padded.shape, lambda i: (0, 0, 0))],
        out_specs=[first_spec, second_spec, first_spec, second_spec],
        out_shape=[first, second, first, second],
        scratch_shapes=[pltpu.VMEM((PEER_HEADS, PEER_KEYS, tm), F32)],
        compiler_params=pltpu.CompilerParams(dimension_semantics=("parallel",), vmem_limit_bytes=VMEM_LIMIT),
        name="peer_select",
    )(qp, keys_padded)


def _peer_dense_kernel(h_ref, u_ref, vt_ref, theta_ref, s2_ref, c1_ref, e2_ref, x_ref, g_ref,
                       o_ref, acc_scr, act_scr, wa_scr, *, final_norm):
    j = pl.program_id(1)
    tm = h_ref.shape[0]
    ec = u_ref.shape[0]
    rows_per_tile = ec // PEER_KEYS

    @pl.when(j == 0)
    def _():
        acc_scr[...] = jnp.zeros(acc_scr.shape, F32)

    n_chunks = tm // LANES
    halves = [range(0, n_chunks // 2), range(n_chunks // 2, n_chunks)]
    for half in halves:
        rows = slice(half[0] * LANES, (half[-1] + 1) * LANES)
        act = _gelu(_nt_dot(u_ref[...], h_ref[rows, :]))
        for n, tc in enumerate(half):
            act_scr[tc] = act[:, n * LANES:(n + 1) * LANES]

    def build(tc, carry):
        cols = pl.ds(pl.multiple_of(tc * LANES, LANES), LANES)
        for ii in range(rows_per_tile):
            rows = slice(ii * PEER_KEYS, (ii + 1) * PEER_KEYS)
            w = jnp.zeros((PEER_KEYS, LANES), F32)
            for h in range(PEER_HEADS):
                gate = e2_ref[h, tc] * c1_ref[h, ii:ii + 1, cols]
                w = w + jnp.where(s2_ref[h, tc] >= theta_ref[h, ii:ii + 1, cols], gate, 0.0)
            wa_scr[tc, rows, :] = (w * act_scr[tc, rows, :]).astype(BF16)
        return carry

    lax.fori_loop(0, n_chunks, build, 0)
    for half in halves:
        cols = slice(half[0] * LANES, (half[-1] + 1) * LANES)
        wa = jnp.concatenate([wa_scr[tc] for tc in half], axis=1)
        acc_scr[:, cols] += jnp.dot(vt_ref[...], wa, preferred_element_type=F32)

    @pl.when(j == pl.num_programs(1) - 1)
    def _():
        y = x_ref[...] + acc_scr[...].T
        if final_norm:
            y = _rms(y, g_ref[...])
        o_ref[...] = y


def peer_dense(h2, u_bf, vt_bf, theta, s2, c1, e2, x, g_final, final_norm):
    t, d = x.shape
    n_exp = u_bf.shape[0]
    tm = min(PEER_TOKEN_TILE, t)
    ec = PEER_EXPERT_TILE
    sel_spec = pl.BlockSpec((PEER_HEADS, tm // LANES, PEER_KEYS, LANES), lambda i, j: (0, i, 0, 0))
    key_spec = pl.BlockSpec((PEER_HEADS, ec // PEER_KEYS, tm), lambda i, j: (0, j, i))
    return pl.pallas_call(
        functools.partial(_peer_dense_kernel, final_norm=final_norm),
        grid=(t // tm, n_exp // ec),
        in_specs=[pl.BlockSpec((tm, d), lambda i, j: (i, 0)),
                  pl.BlockSpec((ec, d), lambda i, j: (j, 0)),
                  pl.BlockSpec((d, ec), lambda i, j: (0, j)),
                  key_spec, sel_spec, key_spec, sel_spec,
                  pl.BlockSpec((tm, d), lambda i, j: (i, 0)),
                  pl.BlockSpec((1, d), lambda i, j: (0, 0))],
        out_specs=pl.BlockSpec((tm, d), lambda i, j: (i, 0)),
        out_shape=jax.ShapeDtypeStruct((t, d), F32),
        scratch_shapes=[pltpu.VMEM((d, tm), F32), pltpu.VMEM((tm // LANES, ec, LANES), F32),
                        pltpu.VMEM((tm // LANES, ec, LANES), BF16)],
        compiler_params=pltpu.CompilerParams(dimension_semantics=("parallel", "arbitrary"),
                                             vmem_limit_bytes=VMEM_LIMIT),
        name="peer_dense",
    )(h2, u_bf, vt_bf, theta, s2, c1, e2, x, g_final.reshape(1, d))


def _t5_bucket_table(max_dist):
    n = np.arange(max_dist + 1)
    max_exact = N_BUCKETS // 2
    nf = np.maximum(n, max_exact).astype(np.float32)
    large = max_exact + (np.log(nf / np.float32(max_exact)) / np.float32(math.log(MAX_DISTANCE / max_exact))
                         * np.float32(N_BUCKETS - max_exact)).astype(np.int32)
    large = np.minimum(large, N_BUCKETS - 1)
    return np.where(n < max_exact, n, large).astype(np.int32)


def _skew(vec, rows):
    h, length = vec.shape
    return jnp.tile(vec, (1, rows))[:, :rows * (length - 1)].reshape(h, rows, length - 1)


def _banded_bias(bias_heads, dilation):
    w = A_BLOCK
    heads = bias_heads.shape[1]
    by_rel = bias_heads[_t5_bucket_table(w * dilation)[np.arange(w + 1) * dilation]].T.astype(F32)
    vec = jnp.concatenate([by_rel, jnp.full((heads, 3 * w - (w + 1)), NEG_INF, F32)], axis=1)
    return jnp.transpose(_skew(vec, 2 * w)[:, :, w:2 * w], (0, 2, 1))


def _diagonal_tiles(by_dist, tq, tk, n_tiles):
    heads = by_dist.shape[0]
    n_cols = (n_tiles - 1) * tk + tq
    mask = jnp.full((heads, tk), NEG_INF, F32)
    vec = jnp.concatenate([mask, by_dist, mask], axis=1)
    assert vec.shape[1] == n_cols + tk
    skew = _skew(vec, tk)
    tiles = jnp.stack([skew[:, :, i * tk:i * tk + tq] for i in range(n_tiles)], axis=1)
    return jnp.transpose(tiles, (0, 1, 3, 2))


def _causal_bias_tiles(bias_heads, tq, tk):
    n_tiles = 2 * -(-(MAX_DISTANCE - 1 + tk + tk) // (2 * tk))
    far = bias_heads[N_BUCKETS - 1]
    n_dist = (n_tiles - 2) * tk + tq
    by_dist = (bias_heads - far[None, :])[_t5_bucket_table(n_dist - 1)].T.astype(F32) * LOG2E
    return _diagonal_tiles(by_dist, tq, tk, n_tiles)


def _causal_mask_tiles(tq, tk):
    return _diagonal_tiles(jnp.zeros((1, tq), F32), tq, tk, 2)


def _rope_tables(seq):
    half = C_ROPE_DIM // 2
    inv = ROPE_BASE ** (-jnp.arange(half, dtype=F32) / half)
    ang = jnp.arange(seq, dtype=F32)[:, None] * inv[None, :]
    cos, sin = jnp.cos(ang), jnp.sin(ang)
    ones = jnp.ones((seq, C_NOPE_DIM), F32)
    zeros_n = jnp.zeros((seq, C_NOPE_DIM), F32)
    spare = jnp.zeros((seq, LANES - C_NOPE_DIM - C_ROPE_DIM), F32)
    cos_t = jnp.concatenate([ones, cos, cos, spare], axis=1)
    sin_t = jnp.concatenate([zeros_n, -sin, sin, spare], axis=1)
    return cos_t, sin_t


def _swap_halves(w):
    half = w.shape[-1] // 2
    return jnp.concatenate([w[..., half:], w[..., :half]], axis=-1)


def _prep_in_proj(w_in):
    d = w_in.shape[0]
    bounds = np.cumsum([A_COLS, B_QK_COLS, B_QK_COLS, B_V_COLS, C_Q_RANK, C_KV_RANK, C_ROPE_DIM])
    wa, wbq, wbk, wbv, wcq, wckv, wckr, wg = jnp.split(w_in, bounds, axis=1)
    zeros = lambda n: jnp.zeros((d, n), w_in.dtype)
    wa = wa.reshape(d, 3, A_HEADS, A_HEAD_DIM)
    qa = wa[:, 0] * (1.0 / math.sqrt(A_HEAD_DIM))
    q_blocks = []
    for h in range(A_HEADS):
        pad = zeros(A_HEAD_DIM)
        q_blocks += [qa[:, h], pad] if h % 2 == 0 else [pad, qa[:, h]]
    w_a = jnp.concatenate(q_blocks + [wa[:, 1].reshape(d, -1), wa[:, 2].reshape(d, -1)], axis=1)
    qb = wbq.reshape(d, B_HEADS, 2, B_QK_DIM) * (LOG2E / math.sqrt(B_QK_DIM))
    q_blocks = []
    for h in range(B_HEADS):
        pad = zeros(B_QK_DIM)
        q_blocks += [qb[:, h, 0], pad, pad, qb[:, h, 1]]
    w_b = jnp.concatenate(q_blocks + [wbk, wbv], axis=1)
    w_gc = jnp.concatenate([wg, wcq, wckv, zeros(C_NOPE_DIM), wckr, _swap_halves(wckr)], axis=1)
    return w_a.astype(BF16), w_b.astype(BF16), w_gc.astype(BF16)


def _prep_mla_weights(w_uq, w_ukv):
    rq = w_uq.shape[0]
    wq = w_uq.reshape(rq, C_HEADS, C_NOPE_DIM + C_ROPE_DIM)
    q_rope = wq[..., C_NOPE_DIM:]
    wq = jnp.concatenate([wq, _swap_halves(q_rope)], axis=-1).reshape(rq, C_QK_WIDTH)
    rkv = w_ukv.shape[0]
    wkv = w_ukv.reshape(rkv, C_HEADS, C_NOPE_DIM + C_V_DIM)
    wk = jnp.concatenate([wkv[..., :C_NOPE_DIM], jnp.zeros((rkv, C_HEADS, LANES - C_NOPE_DIM), w_ukv.dtype)],
                         axis=-1).reshape(rkv, C_QK_WIDTH)
    wv = wkv[..., C_NOPE_DIM:].reshape(rkv, C_V_WIDTH)
    return wq.astype(BF16), wk.astype(BF16), wv.astype(BF16)


def _prep_peer_keys(sub_keys):
    half = PEER_KEY_DIM // 2
    z = jnp.zeros(sub_keys.shape[:1] + sub_keys.shape[2:], sub_keys.dtype)
    first = jnp.concatenate([sub_keys[:, 0], z], axis=-1)
    second = jnp.concatenate([z, sub_keys[:, 1]], axis=-1)
    assert first.shape[-1] == LANES and half * 2 == LANES
    return jnp.stack([first, second], axis=1).reshape(2 * PEER_HEADS, PEER_KEYS, LANES).astype(BF16)


def kernel(x, rel_bias, w_in, g_mix, w_uq, g_cq, w_ukv, g_ckv, lam_q1, lam_k1, lam_q2, lam_k2, g_subln,
           w_branch_a, w_branch_b, w_branch_c, w_out, g_ffn, w_peer_q, peer_sub_keys, peer_u, peer_v, g_final):
    bsz, seq, d = x.shape
    t = bsz * seq
    depth = w_in.shape[0]
    bias_a = rel_bias[:, :A_HEADS]
    bias_b = rel_bias[:, A_HEADS:]
    a_biases = [_banded_bias(bias_a, dil) for _, dil in A_CONFIGS]
    b_bias = _causal_bias_tiles(bias_b, FLASH_Q_TILE, FLASH_K_TILE)
    c_mask = _causal_mask_tiles(FLASH_Q_TILE, FLASH_K_TILE)
    cos_t, sin_t = _rope_tables(seq)
    gate_blocks = 3 * d // C_LATENT_WIDTH

    xf = x.reshape(t, d)
    for l in range(depth):
        w_a, w_b, w_gc = _prep_in_proj(w_in[l])
        dilations = tuple(dil for _, dil in A_CONFIGS)
        views_a = norm_matmul_dilated(xf, g_mix[l], w_a, dilations, A_Q_WIDTH)
        qkv_b = norm_matmul(xf, g_mix[l], w_b, BF16).reshape(bsz, seq, -1)
        gc = norm_matmul(xf, g_mix[l], w_gc, F32)

        a_outs, a_lses = [], []
        for dil, view, bias in zip(dilations, views_a, a_biases):
            o, lse = local_attn(view.reshape(bsz, seq // dil, -1), bias, dil)
            a_outs.append(o.reshape(t // dil, -1))
            a_lses.append(lse.reshape(t // dil, -1))

        lam_init = 0.8 - 0.6 * math.exp(-0.3 * l)
        pad = lambda v: jnp.pad(v.astype(F32), (0, LANES - v.shape[0]))
        lam_prm = jnp.stack([pad(lam_q1[l]), pad(lam_k1[l]), pad(lam_q2[l]), pad(lam_k2[l]),
                             jnp.full((LANES,), lam_init, F32)]
                            + [jnp.zeros((LANES,), F32)] * (SUBLANES - 5))
        y_b = diff_attn(qkv_b, b_bias, lam_prm, g_subln[l]).reshape(t, -1)

        wq, wk, wv = _prep_mla_weights(w_uq[l], w_ukv[l])
        q_c, k_c, v_c = mla_prep(gc, gate_blocks, g_cq[l], g_ckv[l], wq, wk, wv, cos_t, sin_t, seq)
        y_c = mla_attn(q_c.reshape(bsz, seq, -1), k_c.reshape(bsz, seq, -1), v_c.reshape(bsz, seq, -1),
                       c_mask).reshape(t, -1)

        xf, h2, qp = merge_proj(xf, gc, a_outs, a_lses, y_b, y_c,
                                w_branch_a[l].astype(BF16), w_branch_b[l].astype(BF16),
                                w_branch_c[l].astype(BF16), w_out[l].astype(BF16), g_ffn[l],
                                w_peer_q[l].astype(BF16))

        theta, s2, c1, e2 = peer_select(qp, _prep_peer_keys(peer_sub_keys[l]))
        xf = peer_dense(h2, peer_u[l].astype(BF16), peer_v[l].T.astype(BF16), theta, s2, c1, e2, xf,
                        g_final, final_norm=(l == depth - 1))
    return xf.reshape(bsz, seq, d)
```

```python
import functools
import math

import numpy as np
import jax
import jax.numpy as jnp
from jax import lax
from jax.experimental import pallas as pl
from jax.experimental.pallas import tpu as pltpu

D_MODEL = 1024
DEPTH = 4
A_HEADS = 6
A_HEAD_DIM = 64
A_CONFIGS = ((128, 1), (512, 4), (2048, 16))
B_HEADS = 4
B_QK_DIM = 64
B_V_DIM = 128
C_HEADS = 6
C_NOPE_DIM = 64
C_ROPE_DIM = 32
C_V_DIM = 64
C_Q_RANK = 256
C_KV_RANK = 128
ROPE_BASE = 10000.0
N_BUCKETS = 32
MAX_DISTANCE = 2048
PEER_HEADS = 8
PEER_KEYS = 128
PEER_EXPERTS = PEER_KEYS * PEER_KEYS
PEER_KEY_DIM = 128
PEER_TOPK = 16
RMS_EPS = 1e-6
NEG_INF = -1e30
LOG2E = math.log2(math.e)

A_COLS = 3 * A_HEADS * A_HEAD_DIM
B_QK_COLS = B_HEADS * 2 * B_QK_DIM
B_V_COLS = B_HEADS * B_V_DIM

LANES = 128
SUBLANES = 8
VMEM_LIMIT = 56 * 1024 * 1024

F32 = jnp.float32
BF16 = jnp.bfloat16

TOKEN_TILE = 512
PROJ_TOKEN_TILE = 2048
PROJ_COL_TILE = 512
A_BLOCK = 128
A_BLOCKS_PER_STEP = 4
FLASH_Q_TILE = 512
FLASH_K_TILE = 256
FLASH_ROW_BLOCK = 128
FLASH_UNROLLS = (8, 4, 2)
PEER_TOKEN_TILE = 512
PEER_EXPERT_TILE = 2048


def _nt_dot(a, b):
    return lax.dot_general(a, b, (((1,), (1,)), ((), ())), preferred_element_type=F32)


def _gelu(x):
    return 0.5 * x * (1.0 + lax.erf(x * math.sqrt(0.5)))


def _rms(x, g):
    return x * lax.rsqrt(jnp.mean(x * x, axis=-1, keepdims=True) + RMS_EPS) * g


def _norm_matmul_kernel(x_ref, g_ref, w_ref, o_ref, h_scr):
    @pl.when(pl.program_id(1) == 0)
    def _():
        h_scr[...] = _rms(x_ref[...], g_ref[...]).astype(h_scr.dtype)

    o_ref[...] = jnp.dot(h_scr[...], w_ref[...], preferred_element_type=F32).astype(o_ref.dtype)


def norm_matmul(x, g, w, out_dtype):
    t, k = x.shape
    n = w.shape[1]
    tm, tn = min(PROJ_TOKEN_TILE, t), PROJ_COL_TILE
    return pl.pallas_call(
        _norm_matmul_kernel,
        grid=(t // tm, n // tn),
        in_specs=[pl.BlockSpec((tm, k), lambda i, j: (i, 0)),
                  pl.BlockSpec((1, k), lambda i, j: (0, 0)),
                  pl.BlockSpec((k, tn), lambda i, j: (0, j))],
        out_specs=pl.BlockSpec((tm, tn), lambda i, j: (i, j)),
        out_shape=jax.ShapeDtypeStruct((t, n), out_dtype),
        scratch_shapes=[pltpu.VMEM((tm, k), BF16)],
        compiler_params=pltpu.CompilerParams(dimension_semantics=("parallel", "arbitrary"),
                                             vmem_limit_bytes=VMEM_LIMIT),
        name="norm_matmul",
    )(x, g.reshape(1, k), w)


def _norm_matmul_dilated_kernel(x_ref, g_ref, w_ref, *rest, dilations):
    out_refs, (h_scr, res_scr) = rest[:len(dilations)], rest[len(dilations):]
    @pl.when(pl.program_id(1) == 0)
    def _():
        h_scr[...] = _rms(x_ref[...], g_ref[...]).astype(h_scr.dtype)

    res = jnp.dot(h_scr[...], w_ref[...], preferred_element_type=F32)
    n_lane_blocks, tm, _ = res_scr.shape
    tn = n_lane_blocks * LANES
    for c in range(n_lane_blocks):
        res_scr[c] = res[:, c * LANES:(c + 1) * LANES]
    for d, o_ref in zip(dilations, out_refs):
        for r in range(d):
            for c in range(n_lane_blocks):
                cols = slice(r * tn + c * LANES, r * tn + (c + 1) * LANES)
                o_ref[:, cols] = res_scr[c, pl.ds(r, tm // d, stride=d), :].astype(o_ref.dtype)


def norm_matmul_dilated(x, g, w, dilations, tn):
    t, k = x.shape
    n = w.shape[1]
    tm = min(PROJ_TOKEN_TILE, t)
    assert n % tn == 0 and all(tm % (d * SUBLANES) == 0 for d in dilations)
    return pl.pallas_call(
        functools.partial(_norm_matmul_dilated_kernel, dilations=dilations),
        grid=(t // tm, n // tn),
        in_specs=[pl.BlockSpec((tm, k), lambda i, j: (i, 0)),
                  pl.BlockSpec((1, k), lambda i, j: (0, 0)),
                  pl.BlockSpec((k, tn), lambda i, j: (0, j))],
        out_specs=[pl.BlockSpec((tm // d, d * tn), lambda i, j: (i, j)) for d in dilations],
        out_shape=[jax.ShapeDtypeStruct((t // d, d * n), BF16) for d in dilations],
        scratch_shapes=[pltpu.VMEM((tm, k), BF16), pltpu.VMEM((tn // LANES, tm, LANES), F32)],
        compiler_params=pltpu.CompilerParams(dimension_semantics=("parallel", "arbitrary"),
                                             vmem_limit_bytes=VMEM_LIMIT),
        name="norm_matmul_dilated",
    )(x, g.reshape(1, k), w)


A_Q_WIDTH = A_HEADS * LANES
A_PAIRS = A_HEADS // 2
A_KV_WIDTH = 2 * A_PAIRS * LANES
A_OUT_WIDTH = A_PAIRS * LANES


def _local_attn_kernel(q_ref, kvc_ref, kvp_ref, bias_ref, o_ref, lse_ref, kv_scr, *, nblk):
    w = A_BLOCK
    first_class_block = pl.program_id(2) == 0
    kv_scr[0:w, :] = kvp_ref[0]
    kv_scr[w:, :] = kvc_ref[0]
    lane = lax.broadcasted_iota(jnp.int32, (w, LANES), 1)
    col = lax.broadcasted_iota(jnp.int32, (w, 2 * w), 1)
    no_prev = jnp.where(col < w, jnp.where(first_class_block, NEG_INF, 0.0), 0.0)
    ones = jnp.ones((2 * w, LANES), BF16)
    for i in range(nblk):
        rows = slice(i * w, (i + 1) * w)
        for p in range(A_PAIRS):
            k_blk = kv_scr[i * w:(i + 2) * w, p * LANES:(p + 1) * LANES]
            v_blk = kv_scr[i * w:(i + 2) * w, (A_PAIRS + p) * LANES:(A_PAIRS + p + 1) * LANES]
            heads = (2 * p, 2 * p + 1)
            q = jnp.concatenate([q_ref[0, rows, hd * LANES:(hd + 1) * LANES] for hd in heads], axis=0)
            bias = jnp.concatenate([bias_ref[hd] for hd in heads], axis=0)
            if i == 0:
                bias = bias + jnp.concatenate([no_prev, no_prev], axis=0)
            s = _nt_dot(q, k_blk) + bias
            m = jnp.max(s, axis=-1, keepdims=True)
            pr = jnp.exp(s - m)
            pv = jnp.dot(pr.astype(BF16), jnp.concatenate([v_blk, ones], axis=1), preferred_element_type=F32)
            l = pv[:, LANES:]
            o = pv[:, :LANES] / l
            lse = m + jnp.log(l)
            o_ref[0, rows, p * LANES:(p + 1) * LANES] = jnp.where(lane < A_HEAD_DIM, o[:w], o[w:])
            lse_ref[0, rows, p * LANES:(p + 1) * LANES] = jnp.where(lane < A_HEAD_DIM, lse[:w], lse[w:])


def local_attn(view, bias, dilation):
    bsz, rows, width = view.shape
    d = dilation
    w = A_BLOCK
    nblk = min(A_BLOCKS_PER_STEP, rows // w)
    assert rows % (w * nblk) == 0 and width == d * (A_Q_WIDTH + A_KV_WIDTH) and A_Q_WIDTH == A_KV_WIDTH
    grid = (bsz, d, rows // (w * nblk))
    out_shape = jax.ShapeDtypeStruct((bsz, rows, d * A_OUT_WIDTH), F32)
    o, lse = pl.pallas_call(
        functools.partial(_local_attn_kernel, nblk=nblk),
        grid=grid,
        in_specs=[pl.BlockSpec((1, w * nblk, A_Q_WIDTH), lambda b, r, n: (b, n, r)),
                  pl.BlockSpec((1, w * nblk, A_KV_WIDTH), lambda b, r, n: (b, n, d + r)),
                  pl.BlockSpec((1, w, A_KV_WIDTH), lambda b, r, n: (b, jnp.maximum(n * nblk - 1, 0), d + r)),
                  pl.BlockSpec((A_HEADS, w, 2 * w), lambda b, r, n: (0, 0, 0))],
        out_specs=[pl.BlockSpec((1, w * nblk, A_OUT_WIDTH), lambda b, r, n: (b, n, r)),
                   pl.BlockSpec((1, w * nblk, A_OUT_WIDTH), lambda b, r, n: (b, n, r))],
        out_shape=[out_shape, out_shape],
        scratch_shapes=[pltpu.VMEM((w * (nblk + 1), A_KV_WIDTH), BF16)],
        compiler_params=pltpu.CompilerParams(dimension_semantics=("parallel", "parallel", "arbitrary")),
        name=f"local_attn_d{d}",
    )(view, view, view, bias)
    return o, lse


def _flash_pair_kernel(*refs, tq, tk, n_near, shared_k, mode):
    if shared_k:
        qa_ref, qb_ref, ka_ref, v_ref, bias_ref = refs[:5]
        kb_ref = ka_ref
        rest = refs[5:]
    else:
        qa_ref, qb_ref, ka_ref, kb_ref, v_ref, bias_ref = refs[:6]
        rest = refs[6:]
    if mode == "diff":
        prm_ref, gsub_ref, o_ref, m_scr, alpha_scr, acc_scr, s0_scr, s1_scr, p_scr = rest
    else:
        o_ref, m_scr, alpha_scr, acc_scr, s0_scr, s1_scr, p_scr = rest

    qi = pl.program_id(2)
    rb = FLASH_ROW_BLOCK
    m_scr[...] = jnp.full(m_scr.shape, NEG_INF, F32)
    acc_scr[...] = jnp.zeros(acc_scr.shape, F32)
    ones = jnp.ones((tk, LANES), BF16)
    last_kt = 2 * qi + 1

    def scores(kt, s_scr):
        koff = pl.multiple_of(jnp.minimum(kt, last_kt) * tk, tk)
        ka = ka_ref[0, pl.ds(koff, tk), :]
        if shared_k:
            s_scr[...] = _nt_dot(jnp.concatenate([qa_ref[0], qb_ref[0]], axis=0), ka)
        else:
            s_scr[0:tq, :] = _nt_dot(qa_ref[0], ka)
            s_scr[tq:, :] = _nt_dot(qb_ref[0], kb_ref[0, pl.ds(koff, tk), :])

    def accumulate(kt, s_scr, near):
        koff = pl.multiple_of(kt * tk, tk)
        for r in range(2 * tq // rb):
            rows = slice(r * rb, (r + 1) * rb)
            s = s_scr[rows, :]
            if near:
                b0 = (r * rb) % tq
                s = s + bias_ref[0, last_kt - kt, b0:b0 + rb, :]
            m_prev = m_scr[rows, :]
            m_new = jnp.maximum(m_prev, jnp.max(s, axis=-1, keepdims=True))
            alpha_scr[rows, :] = jnp.exp2(m_prev - m_new)
            m_scr[rows, :] = m_new
            p_scr[rows, :] = jnp.exp2(s - jnp.concatenate([m_new] * (tk // LANES), axis=1)).astype(BF16)
        v_ones = jnp.concatenate([v_ref[0, pl.ds(koff, tk), :], ones], axis=1)
        pv = jnp.dot(p_scr[...], v_ones, preferred_element_type=F32)
        alpha = alpha_scr[...]
        acc_scr[...] = jnp.concatenate([alpha, alpha], axis=1) * acc_scr[...] + pv

    n_tiles = 2 * qi + 2
    n_far = jnp.maximum(n_tiles - n_near, 0)

    def run(first, count, near):
        def unrolled(kt, steps):
            for step in range(steps):
                nxt, cur = (s1_scr, s0_scr) if step % 2 == 0 else (s0_scr, s1_scr)
                scores(kt + step + 1, nxt)
                accumulate(kt + step, cur, near)

        done = 0
        for steps in FLASH_UNROLLS:
            def body(i, carry, steps=steps, done=done):
                unrolled(first + done + steps * i, steps)
                return carry

            trips = (count - done) // steps
            lax.fori_loop(0, trips, body, 0)
            done = done + trips * steps

    scores(0, s0_scr)
    run(0, n_far, False)
    run(n_far, n_tiles - n_far, True)

    o = acc_scr[:, :LANES] / acc_scr[:, LANES:]
    oa, ob = o[:tq], o[tq:]
    if mode == "diff":
        prm = prm_ref[...]
        lam_init = prm[4:5, 0:1]
        lam = (jnp.exp(jnp.sum(prm[0:1] * prm[1:2], axis=-1, keepdims=True))
               - jnp.exp(jnp.sum(prm[2:3] * prm[3:4], axis=-1, keepdims=True)) + lam_init)
        d = oa - lam * ob
        o_ref[0] = (_rms(d, gsub_ref[...]) * (1.0 - lam_init)).astype(o_ref.dtype)
    else:
        lane = lax.broadcasted_iota(jnp.int32, oa.shape, 1)
        o_ref[0] = jnp.where(lane < C_V_DIM, oa, ob).astype(o_ref.dtype)


def _flash_pair_call(mode, q_arr, k_arr, v_arr, bias, groups, col_maps, out_cols, extra):
    bsz, seq, _ = q_arr.shape
    tq, tk = FLASH_Q_TILE, FLASH_K_TILE
    assert seq % tq == 0 and tq == 2 * tk
    n_near = bias.shape[1]
    assert n_near % 2 == 0
    qa_c, qb_c, ka_c, kb_c, v_c = col_maps
    shared_k = kb_c is None
    q_spec = lambda cm: pl.BlockSpec((1, tq, LANES), lambda b, g, i: (b, i, cm(g)))
    kv_spec = lambda cm: pl.BlockSpec((1, seq, LANES), lambda b, g, i: (b, 0, cm(g)))
    in_specs = [q_spec(qa_c), q_spec(qb_c), kv_spec(ka_c)]
    args = [q_arr, q_arr, k_arr]
    if not shared_k:
        in_specs.append(kv_spec(kb_c))
        args.append(k_arr)
    in_specs.append(kv_spec(v_c))
    args.append(v_arr)
    bias_group = (lambda g: g) if bias.shape[0] > 1 else (lambda g: 0)
    in_specs.append(pl.BlockSpec((1, n_near, tq, tk), lambda b, g, i: (bias_group(g), 0, 0, 0)))
    args.append(bias)
    for e in extra:
        in_specs.append(pl.BlockSpec(e.shape, lambda b, g, i: (0, 0)))
        args.append(e)
    return pl.pallas_call(
        functools.partial(_flash_pair_kernel, tq=tq, tk=tk, n_near=n_near, shared_k=shared_k, mode=mode),
        grid=(bsz, groups, seq // tq),
        in_specs=in_specs,
        out_specs=pl.BlockSpec((1, tq, LANES), lambda b, g, i: (b, i, g)),
        out_shape=jax.ShapeDtypeStruct((bsz, seq, out_cols), BF16),
        scratch_shapes=[pltpu.VMEM((2 * tq, LANES), F32), pltpu.VMEM((2 * tq, LANES), F32),
                        pltpu.VMEM((2 * tq, 2 * LANES), F32), pltpu.VMEM((2 * tq, tk), F32),
                        pltpu.VMEM((2 * tq, tk), F32), pltpu.VMEM((2 * tq, tk), BF16)],
        compiler_params=pltpu.CompilerParams(dimension_semantics=("parallel", "parallel", "arbitrary"),
                                             vmem_limit_bytes=VMEM_LIMIT),
        name=f"flash_pair_{mode}",
    )(*args)


B_Q_BLOCKS = 2 * B_HEADS
B_WIDTH_PADDED = (B_Q_BLOCKS + 2 * B_HEADS) * LANES


def diff_attn(qkv, bias, lam_prm, g_subln):
    return _flash_pair_call(
        "diff", qkv, qkv, qkv, bias, B_HEADS,
        (lambda g: 2 * g, lambda g: 2 * g + 1, lambda g: B_Q_BLOCKS + g, None,
         lambda g: B_Q_BLOCKS + B_HEADS + g),
        B_HEADS * B_V_DIM, [lam_prm, g_subln.reshape(1, B_V_DIM)])


def mla_attn(q, k, v, mask):
    return _flash_pair_call(
        "mla", q, k, v, mask, C_HEADS // 2,
        (lambda g: 2 * g, lambda g: 2 * g + 1, lambda g: 2 * g, lambda g: 2 * g + 1, lambda g: g),
        C_HEADS * C_V_DIM, [])


C_QK_WIDTH = C_HEADS * LANES
C_V_WIDTH = C_HEADS * C_V_DIM
C_LATENT_WIDTH = C_Q_RANK + C_KV_RANK + LANES
ROPE_SHIFT = LANES - C_ROPE_DIM


def _mla_prep_kernel(c_ref, gq_ref, gkv_ref, wq_ref, wk_ref, wv_ref, cos_ref, sin_ref, q_ref, k_ref, v_ref):
    c = c_ref[...]
    cos = cos_ref[...]
    sin = sin_ref[...]

    def rope(blk):
        return blk * cos + pltpu.roll(blk, ROPE_SHIFT, 1) * sin

    cq = _rms(c[:, :C_Q_RANK], gq_ref[...]).astype(BF16)
    ckv = _rms(c[:, C_Q_RANK:C_Q_RANK + C_KV_RANK], gkv_ref[...]).astype(BF16)
    q_all = jnp.dot(cq, wq_ref[...], preferred_element_type=F32)
    k_all = jnp.dot(ckv, wk_ref[...], preferred_element_type=F32)
    k_rope = rope(c[:, C_Q_RANK + C_KV_RANK:])
    scale = LOG2E / math.sqrt(C_NOPE_DIM + C_ROPE_DIM)
    for h in range(C_HEADS):
        cols = slice(h * LANES, (h + 1) * LANES)
        q_ref[:, cols] = (rope(q_all[:, cols]) * scale).astype(q_ref.dtype)
        k_ref[:, cols] = (k_all[:, cols] + k_rope).astype(k_ref.dtype)
    v_ref[...] = jnp.dot(ckv, wv_ref[...], preferred_element_type=F32).astype(v_ref.dtype)


def mla_prep(c_lat, c_col_block, g_cq, g_ckv, w_q, w_k, w_v, cos_t, sin_t, seq):
    t = c_lat.shape[0]
    tm = min(TOKEN_TILE, seq)
    per_seq = seq // tm
    full = lambda a: pl.BlockSpec(a.shape, lambda i: (0, 0))
    g_cq = g_cq.reshape(1, -1)
    g_ckv = g_ckv.reshape(1, -1)
    return pl.pallas_call(
        _mla_prep_kernel,
        grid=(t // tm,),
        in_specs=[pl.BlockSpec((tm, C_LATENT_WIDTH), lambda i: (i, c_col_block)),
                  full(g_cq), full(g_ckv), full(w_q), full(w_k), full(w_v),
                  pl.BlockSpec((tm, LANES), lambda i: (i % per_seq, 0)),
                  pl.BlockSpec((tm, LANES), lambda i: (i % per_seq, 0))],
        out_specs=[pl.BlockSpec((tm, C_QK_WIDTH), lambda i: (i, 0)),
                   pl.BlockSpec((tm, C_QK_WIDTH), lambda i: (i, 0)),
                   pl.BlockSpec((tm, C_V_WIDTH), lambda i: (i, 0))],
        out_shape=[jax.ShapeDtypeStruct((t, C_QK_WIDTH), BF16), jax.ShapeDtypeStruct((t, C_QK_WIDTH), BF16),
                   jax.ShapeDtypeStruct((t, C_V_WIDTH), BF16)],
        compiler_params=pltpu.CompilerParams(dimension_semantics=("parallel",)),
        name="mla_prep",
    )(c_lat, g_cq, g_ckv, w_q, w_k, w_v, cos_t, sin_t)


def _merge_proj_kernel(x_ref, ga_ref, gb_ref, gc_ref, o1_ref, o2_ref, o3_ref, l1_ref, l2_ref, l3_ref,
                       yb_ref, yc_ref, wa_ref, wb_ref, wc_ref, wo_ref, gffn_ref, wpq_ref,
                       xo_ref, h2_ref, qp_ref, *token_order_scr):
    tm = x_ref.shape[0]

    def token_order(ref, scr):
        d = ref.shape[1] // A_OUT_WIDTH
        if d == 1:
            return ref[...]
        for r in range(d):
            for c in range(A_PAIRS):
                cols = slice(r * A_OUT_WIDTH + c * LANES, r * A_OUT_WIDTH + (c + 1) * LANES)
                scr[c, pl.ds(r, tm // d, stride=d), :] = ref[:, cols]
        return jnp.concatenate([scr[c] for c in range(A_PAIRS)], axis=1)

    o1, o2, o3, l1, l2, l3 = [token_order(ref, scr) for ref, scr in
                              zip((o1_ref, o2_ref, o3_ref, l1_ref, l2_ref, l3_ref), token_order_scr)]
    m = jnp.maximum(jnp.maximum(l1, l2), l3)
    e1, e2, e3 = jnp.exp(l1 - m), jnp.exp(l2 - m), jnp.exp(l3 - m)
    den = e1 + e2 + e3
    ya = (e1 / den) * o1 + (e2 / den) * o2 + (e3 / den) * o3
    pa = jnp.dot(ya.astype(BF16), wa_ref[...], preferred_element_type=F32)
    pb = jnp.dot(yb_ref[...], wb_ref[...], preferred_element_type=F32)
    pc = jnp.dot(yc_ref[...], wc_ref[...], preferred_element_type=F32)
    mix = (jax.nn.sigmoid(ga_ref[...]) * pa + jax.nn.sigmoid(gb_ref[...]) * pb
           + jax.nn.sigmoid(gc_ref[...]) * pc)
    x_new = x_ref[...] + jnp.dot(mix.astype(BF16), wo_ref[...], preferred_element_type=F32)
    xo_ref[...] = x_new
    h2 = _rms(x_new, gffn_ref[...]).astype(BF16)
    h2_ref[...] = h2
    qp_ref[...] = jnp.dot(h2, wpq_ref[...], preferred_element_type=F32)


def merge_proj(x, gc_arr, a_outs, a_lses, y_b, y_c, w_a, w_b, w_c, w_o, g_ffn, w_pq):
    t, d = x.shape
    tm = min(TOKEN_TILE, t)
    row = lambda width, col=0: pl.BlockSpec((tm, width), lambda i: (i, col))
    full = lambda a: pl.BlockSpec(a.shape, lambda i: (0, 0))
    g_ffn = g_ffn.reshape(1, d)
    group = lambda a: pl.BlockSpec((tm // (a.shape[1] // A_OUT_WIDTH), a.shape[1]), lambda i: (i, 0))
    in_specs = ([row(d)] + [row(d, c) for c in range(3)]
                + [group(a) for a in (*a_outs, *a_lses)] + [row(y_b.shape[1]), row(y_c.shape[1])]
                + [full(w_a), full(w_b), full(w_c), full(w_o), full(g_ffn), full(w_pq)])
    return pl.pallas_call(
        _merge_proj_kernel,
        grid=(t // tm,),
        in_specs=in_specs,
        out_specs=[row(d), row(d), row(d)],
        out_shape=[jax.ShapeDtypeStruct((t, d), F32), jax.ShapeDtypeStruct((t, d), BF16),
                   jax.ShapeDtypeStruct((t, d), F32)],
        scratch_shapes=[pltpu.VMEM((A_PAIRS, tm, LANES), F32)] * 6,
        compiler_params=pltpu.CompilerParams(dimension_semantics=("parallel",), vmem_limit_bytes=VMEM_LIMIT),
        name="merge_proj",
    )(x, gc_arr, gc_arr, gc_arr, *a_outs, *a_lses, y_b, y_c, w_a, w_b, w_c, w_o, g_ffn, w_pq)


def _oddeven_merge_sort_pairs(n):
    pairs = []
    p = 1
    while p < n:
        k = p
        while k >= 1:
            for j in range(k % p, n - k, 2 * k):
                for i in range(min(k, n - j - k)):
                    if (i + j) // (2 * p) == (i + j + k) // (2 * p):
                        pairs.append((i + j, i + j + k))
            k //= 2
        p *= 2
    return pairs


def _bitonic_merge_pairs(n):
    pairs = []
    stride = n // 2
    while stride >= 1:
        pairs += [(i, i + stride) for i in range(n) if not i & stride]
        stride //= 2
    return pairs


def _compare_exchange(vals, pairs):
    for i, j in pairs:
        a, b = vals[i], vals[j]
        if b is None:
            continue
        if a is None:
            vals[i], vals[j] = b, None
        else:
            vals[i], vals[j] = jnp.maximum(a, b), jnp.minimum(a, b)
    return vals


_SORT16 = _oddeven_merge_sort_pairs(PEER_TOPK)
_MERGE16 = _bitonic_merge_pairs(PEER_TOPK)
_CAND_PAIRS = [(a, b) for a in range(PEER_TOPK) for b in range(PEER_TOPK) if (a + 1) * (b + 1) <= PEER_TOPK]
_SORT64 = _oddeven_merge_sort_pairs(64)


def _top16_sorted(scores_t):
    slabs = [scores_t[v * SUBLANES:(v + 1) * SUBLANES, :] for v in range(PEER_KEYS // SUBLANES)]
    slabs = _compare_exchange(slabs, _SORT16)
    shift = SUBLANES // 2
    while shift >= 1:
        other = [pltpu.roll(s, shift, 0) for s in slabs]
        slabs = [jnp.maximum(slabs[r], other[PEER_TOPK - 1 - r]) for r in range(PEER_TOPK)]
        slabs = _compare_exchange(slabs, _MERGE16)
        shift //= 2
    return slabs


def _peer_select_kernel(qp_ref, keys_ref, theta_ref, s2_ref, c1_ref, e2_ref, s1_scr):
    tm = qp_ref.shape[0]
    sub = lax.broadcasted_iota(jnp.int32, (SUBLANES, tm), 0)
    tops = [[None] * PEER_TOPK for _ in range(2)]
    for h in range(PEER_HEADS):
        qh = qp_ref[:, h * LANES:(h + 1) * LANES].astype(BF16)
        for p in range(2):
            s_t = _nt_dot(keys_ref[2 * h + p], qh)
            if p == 0:
                s1_scr[h] = s_t
            else:
                for c in range(tm // LANES):
                    s2_ref[h, c] = s_t[:, c * LANES:(c + 1) * LANES]
            top = _top16_sorted(s_t)
            for r in range(PEER_TOPK):
                tops[p][r] = top[r] if h == 0 else jnp.where(sub == h, top[r], tops[p][r])
    cand = [tops[0][a] + tops[1][b] for a, b in _CAND_PAIRS] + [None] * (64 - len(_CAND_PAIRS))
    best = _compare_exchange(cand, _SORT64)[:PEER_TOPK]
    mx = best[0]
    z = functools.reduce(lambda u, w_: u + w_, [jnp.exp(b - mx) for b in best])
    thr = best[PEER_TOPK - 1]
    m1, m2 = tops[0][0], tops[1][0]

    def smallest_selected(s1_vals, thr_vals, t2_of, n_b):
        theta = jnp.full(s1_vals.shape, -NEG_INF, F32)
        for b in range(n_b):
            theta = jnp.where(s1_vals + t2_of(b) >= thr_vals, t2_of(b), theta)
        return theta

    n_best = 3
    short = PEER_TOPK // (n_best + 1)
    theta_best = [smallest_selected(tops[0][a], thr, lambda b: tops[1][b], PEER_TOPK) for a in range(n_best)]
    for h in range(PEER_HEADS):
        row = lambda a: a[h:h + 1, :]
        s1 = s1_scr[h]
        theta = smallest_selected(s1, row(thr), lambda b: row(tops[1][b]), short)
        for a in range(n_best):
            theta = jnp.where(s1 == row(tops[0][a]), row(theta_best[a]), theta)
        theta_ref[h] = theta
        c1_ref[h] = jnp.exp(s1 - row(m1)) / row(z)
        for c in range(tm // LANES):
            e2_ref[h, c] = jnp.exp(s2_ref[h, c] - m2[h:h + 1, c * LANES:(c + 1) * LANES])


def peer_select(qp, keys_padded):
    t = qp.shape[0]
    tm = min(PEER_TOKEN_TILE, t)
    first = jax.ShapeDtypeStruct((PEER_HEADS, PEER_KEYS, t), F32)
    first_spec = pl.BlockSpec((PEER_HEADS, PEER_KEYS, tm), lambda i: (0, 0, i))
    second = jax.ShapeDtypeStruct((PEER_HEADS, t // LANES, PEER_KEYS, LANES), F32)
    second_spec = pl.BlockSpec((PEER_HEADS, tm // LANES, PEER_KEYS, LANES), lambda i: (0, i, 0, 0))
    return pl.pallas_call(
        _peer_select_kernel,
        grid=(t // tm,),
        in_specs=[pl.BlockSpec((tm, PEER_HEADS * PEER_KEY_DIM), lambda i: (i, 0)),
                  pl.BlockSpec(keys_padded.shape, lambda i: (0, 0, 0))],
        out_specs=[first_spec, second_spec, first_spec, second_spec],
        out_shape=[first, second, first, second],
        scratch_shapes=[pltpu.VMEM((PEER_HEADS, PEER_KEYS, tm), F32)],
        compiler_params=pltpu.CompilerParams(dimension_semantics=("parallel",), vmem_limit_bytes=VMEM_LIMIT),
        name="peer_select",
    )(qp, keys_padded)


def _peer_dense_kernel(h_ref, u_ref, vt_ref, theta_ref, s2_ref, c1_ref, e2_ref, x_ref, g_ref,
                       o_ref, acc_scr, gate_scr, wa_scr, *, final_norm):
    j = pl.program_id(1)
    tm = h_ref.shape[0]
    ec = u_ref.shape[0]
    rows_per_tile = ec // PEER_KEYS

    @pl.when(j == 0)
    def _():
        acc_scr[...] = jnp.zeros(acc_scr.shape, F32)

    n_chunks = tm // LANES

    def build(tc, carry):
        cols = pl.ds(pl.multiple_of(tc * LANES, LANES), LANES)
        for ii in range(rows_per_tile):
            rows = slice(ii * PEER_KEYS, (ii + 1) * PEER_KEYS)
            w = None
            for h in range(PEER_HEADS):
                gate = e2_ref[h, tc] * c1_ref[h, ii:ii + 1, cols]
                term = jnp.where(s2_ref[h, tc] >= theta_ref[h, ii:ii + 1, cols], gate, 0.0)
                w = term if w is None else w + term
            gate_scr[tc, rows, :] = w.astype(BF16)
        return carry

    lax.fori_loop(0, n_chunks, build, 0)
    halves = [range(0, n_chunks // 2), range(n_chunks // 2, n_chunks)]
    for half in halves:
        rows = slice(half[0] * LANES, (half[-1] + 1) * LANES)
        act = _gelu(_nt_dot(u_ref[...], h_ref[rows, :]))
        for n, tc in enumerate(half):
            wa_scr[tc] = (act[:, n * LANES:(n + 1) * LANES] * gate_scr[tc].astype(F32)).astype(BF16)
    for half in halves:
        cols = slice(half[0] * LANES, (half[-1] + 1) * LANES)
        wa = jnp.concatenate([wa_scr[tc] for tc in half], axis=1)
        acc_scr[:, cols] += jnp.dot(vt_ref[...], wa, preferred_element_type=F32)

    @pl.when(j == pl.num_programs(1) - 1)
    def _():
        y = x_ref[...] + acc_scr[...].T
        if final_norm:
            y = _rms(y, g_ref[...])
        o_ref[...] = y


def peer_dense(h2, u_bf, vt_bf, theta, s2, c1, e2, x, g_final, final_norm):
    t, d = x.shape
    n_exp = u_bf.shape[0]
    tm = min(PEER_TOKEN_TILE, t)
    ec = PEER_EXPERT_TILE
    sel_spec = pl.BlockSpec((PEER_HEADS, tm // LANES, PEER_KEYS, LANES), lambda i, j: (0, i, 0, 0))
    key_spec = pl.BlockSpec((PEER_HEADS, ec // PEER_KEYS, tm), lambda i, j: (0, j, i))
    return pl.pallas_call(
        functools.partial(_peer_dense_kernel, final_norm=final_norm),
        grid=(t // tm, n_exp // ec),
        in_specs=[pl.BlockSpec((tm, d), lambda i, j: (i, 0)),
                  pl.BlockSpec((ec, d), lambda i, j: (j, 0)),
                  pl.BlockSpec((d, ec), lambda i, j: (0, j)),
                  key_spec, sel_spec, key_spec, sel_spec,
                  pl.BlockSpec((tm, d), lambda i, j: (i, 0)),
                  pl.BlockSpec((1, d), lambda i, j: (0, 0))],
        out_specs=pl.BlockSpec((tm, d), lambda i, j: (i, 0)),
        out_shape=jax.ShapeDtypeStruct((t, d), F32),
        scratch_shapes=[pltpu.VMEM((d, tm), F32), pltpu.VMEM((tm // LANES, ec, LANES), BF16),
                        pltpu.VMEM((tm // LANES, ec, LANES), BF16)],
        compiler_params=pltpu.CompilerParams(dimension_semantics=("parallel", "arbitrary"),
                                             vmem_limit_bytes=VMEM_LIMIT),
        name="peer_dense",
    )(h2, u_bf, vt_bf, theta, s2, c1, e2, x, g_final.reshape(1, d))


def _t5_bucket_table(max_dist):
    n = np.arange(max_dist + 1)
    max_exact = N_BUCKETS // 2
    nf = np.maximum(n, max_exact).astype(np.float32)
    large = max_exact + (np.log(nf / np.float32(max_exact)) / np.float32(math.log(MAX_DISTANCE / max_exact))
                         * np.float32(N_BUCKETS - max_exact)).astype(np.int32)
    large = np.minimum(large, N_BUCKETS - 1)
    return np.where(n < max_exact, n, large).astype(np.int32)


def _skew(vec, rows):
    h, length = vec.shape
    return jnp.tile(vec, (1, rows))[:, :rows * (length - 1)].reshape(h, rows, length - 1)


def _banded_bias(bias_heads, dilation):
    w = A_BLOCK
    heads = bias_heads.shape[1]
    by_rel = bias_heads[_t5_bucket_table(w * dilation)[np.arange(w + 1) * dilation]].T.astype(F32)
    vec = jnp.concatenate([by_rel, jnp.full((heads, 3 * w - (w + 1)), NEG_INF, F32)], axis=1)
    return jnp.transpose(_skew(vec, 2 * w)[:, :, w:2 * w], (0, 2, 1))


def _diagonal_tiles(by_dist, tq, tk, n_tiles):
    heads = by_dist.shape[0]
    n_cols = (n_tiles - 1) * tk + tq
    mask = jnp.full((heads, tk), NEG_INF, F32)
    vec = jnp.concatenate([mask, by_dist, mask], axis=1)
    assert vec.shape[1] == n_cols + tk
    skew = _skew(vec, tk)
    tiles = jnp.stack([skew[:, :, i * tk:i * tk + tq] for i in range(n_tiles)], axis=1)
    return jnp.transpose(tiles, (0, 1, 3, 2))


def _causal_bias_tiles(bias_heads, tq, tk):
    n_tiles = 2 * -(-(MAX_DISTANCE - 1 + tk + tk) // (2 * tk))
    far = bias_heads[N_BUCKETS - 1]
    n_dist = (n_tiles - 2) * tk + tq
    by_dist = (bias_heads - far[None, :])[_t5_bucket_table(n_dist - 1)].T.astype(F32) * LOG2E
    return _diagonal_tiles(by_dist, tq, tk, n_tiles)


def _causal_mask_tiles(tq, tk):
    return _diagonal_tiles(jnp.zeros((1, tq), F32), tq, tk, 2)


def _rope_tables(seq):
    half = C_ROPE_DIM // 2
    inv = ROPE_BASE ** (-jnp.arange(half, dtype=F32) / half)
    ang = jnp.arange(seq, dtype=F32)[:, None] * inv[None, :]
    cos, sin = jnp.cos(ang), jnp.sin(ang)
    ones = jnp.ones((seq, C_NOPE_DIM), F32)
    zeros_n = jnp.zeros((seq, C_NOPE_DIM), F32)
    spare = jnp.zeros((seq, LANES - C_NOPE_DIM - C_ROPE_DIM), F32)
    cos_t = jnp.concatenate([ones, cos, cos, spare], axis=1)
    sin_t = jnp.concatenate([zeros_n, -sin, sin, spare], axis=1)
    return cos_t, sin_t


def _swap_halves(w):
    half = w.shape[-1] // 2
    return jnp.concatenate([w[..., half:], w[..., :half]], axis=-1)


def _prep_in_proj(w_in):
    d = w_in.shape[0]
    bounds = np.cumsum([A_COLS, B_QK_COLS, B_QK_COLS, B_V_COLS, C_Q_RANK, C_KV_RANK, C_ROPE_DIM])
    wa, wbq, wbk, wbv, wcq, wckv, wckr, wg = jnp.split(w_in, bounds, axis=1)
    zeros = lambda n: jnp.zeros((d, n), w_in.dtype)
    wa = wa.reshape(d, 3, A_HEADS, A_HEAD_DIM)
    qa = wa[:, 0] * (1.0 / math.sqrt(A_HEAD_DIM))
    q_blocks = []
    for h in range(A_HEADS):
        pad = zeros(A_HEAD_DIM)
        q_blocks += [qa[:, h], pad] if h % 2 == 0 else [pad, qa[:, h]]
    w_a = jnp.concatenate(q_blocks + [wa[:, 1].reshape(d, -1), wa[:, 2].reshape(d, -1)], axis=1)
    qb = wbq.reshape(d, B_HEADS, 2, B_QK_DIM) * (LOG2E / math.sqrt(B_QK_DIM))
    q_blocks = []
    for h in range(B_HEADS):
        pad = zeros(B_QK_DIM)
        q_blocks += [qb[:, h, 0], pad, pad, qb[:, h, 1]]
    w_b = jnp.concatenate(q_blocks + [wbk, wbv], axis=1)
    w_gc = jnp.concatenate([wg, wcq, wckv, zeros(C_NOPE_DIM), wckr, _swap_halves(wckr)], axis=1)
    return w_a.astype(BF16), w_b.astype(BF16), w_gc.astype(BF16)


def _prep_mla_weights(w_uq, w_ukv):
    rq = w_uq.shape[0]
    wq = w_uq.reshape(rq, C_HEADS, C_NOPE_DIM + C_ROPE_DIM)
    q_rope = wq[..., C_NOPE_DIM:]
    wq = jnp.concatenate([wq, _swap_halves(q_rope)], axis=-1).reshape(rq, C_QK_WIDTH)
    rkv = w_ukv.shape[0]
    wkv = w_ukv.reshape(rkv, C_HEADS, C_NOPE_DIM + C_V_DIM)
    wk = jnp.concatenate([wkv[..., :C_NOPE_DIM], jnp.zeros((rkv, C_HEADS, LANES - C_NOPE_DIM), w_ukv.dtype)],
                         axis=-1).reshape(rkv, C_QK_WIDTH)
    wv = wkv[..., C_NOPE_DIM:].reshape(rkv, C_V_WIDTH)
    return wq.astype(BF16), wk.astype(BF16), wv.astype(BF16)


def _prep_peer_keys(sub_keys):
    half = PEER_KEY_DIM // 2
    z = jnp.zeros(sub_keys.shape[:1] + sub_keys.shape[2:], sub_keys.dtype)
    first = jnp.concatenate([sub_keys[:, 0], z], axis=-1)
    second = jnp.concatenate([z, sub_keys[:, 1]], axis=-1)
    assert first.shape[-1] == LANES and half * 2 == LANES
    return jnp.stack([first, second], axis=1).reshape(2 * PEER_HEADS, PEER_KEYS, LANES).astype(BF16)


def kernel(x, rel_bias, w_in, g_mix, w_uq, g_cq, w_ukv, g_ckv, lam_q1, lam_k1, lam_q2, lam_k2, g_subln,
           w_branch_a, w_branch_b, w_branch_c, w_out, g_ffn, w_peer_q, peer_sub_keys, peer_u, peer_v, g_final):
    bsz, seq, d = x.shape
    t = bsz * seq
    depth = w_in.shape[0]
    bias_a = rel_bias[:, :A_HEADS]
    bias_b = rel_bias[:, A_HEADS:]
    a_biases = [_banded_bias(bias_a, dil) for _, dil in A_CONFIGS]
    b_bias = _causal_bias_tiles(bias_b, FLASH_Q_TILE, FLASH_K_TILE)
    c_mask = _causal_mask_tiles(FLASH_Q_TILE, FLASH_K_TILE)
    cos_t, sin_t = _rope_tables(seq)
    gate_blocks = 3 * d // C_LATENT_WIDTH

    xf = x.reshape(t, d)
    for l in range(depth):
        w_a, w_b, w_gc = _prep_in_proj(w_in[l])
        dilations = tuple(dil for _, dil in A_CONFIGS)
        views_a = norm_matmul_dilated(xf, g_mix[l], w_a, dilations, A_Q_WIDTH)
        qkv_b = norm_matmul(xf, g_mix[l], w_b, BF16).reshape(bsz, seq, -1)
        gc = norm_matmul(xf, g_mix[l], w_gc, F32)

        a_outs, a_lses = [], []
        for dil, view, bias in zip(dilations, views_a, a_biases):
            o, lse = local_attn(view.reshape(bsz, seq // dil, -1), bias, dil)
            a_outs.append(o.reshape(t // dil, -1))
            a_lses.append(lse.reshape(t // dil, -1))

        lam_init = 0.8 - 0.6 * math.exp(-0.3 * l)
        pad = lambda v: jnp.pad(v.astype(F32), (0, LANES - v.shape[0]))
        lam_prm = jnp.stack([pad(lam_q1[l]), pad(lam_k1[l]), pad(lam_q2[l]), pad(lam_k2[l]),
                             jnp.full((LANES,), lam_init, F32)]
                            + [jnp.zeros((LANES,), F32)] * (SUBLANES - 5))
        y_b = diff_attn(qkv_b, b_bias, lam_prm, g_subln[l]).reshape(t, -1)

        wq, wk, wv = _prep_mla_weights(w_uq[l], w_ukv[l])
        q_c, k_c, v_c = mla_prep(gc, gate_blocks, g_cq[l], g_ckv[l], wq, wk, wv, cos_t, sin_t, seq)
        y_c = mla_attn(q_c.reshape(bsz, seq, -1), k_c.reshape(bsz, seq, -1), v_c.reshape(bsz, seq, -1),
                       c_mask).reshape(t, -1)

        xf, h2, qp = merge_proj(xf, gc, a_outs, a_lses, y_b, y_c,
                                w_branch_a[l].astype(BF16), w_branch_b[l].astype(BF16),
                                w_branch_c[l].astype(BF16), w_out[l].astype(BF16), g_ffn[l],
                                w_peer_q[l].astype(BF16))

        theta, s2, c1, e2 = peer_select(qp, _prep_peer_keys(peer_sub_keys[l]))
        xf = peer_dense(h2, peer_u[l].astype(BF16), peer_v[l].T.astype(BF16), theta, s2, c1, e2, xf,
                        g_final, final_norm=(l == depth - 1))
    return xf.reshape(bsz, seq, d)
```

```python
import functools
import math

import numpy as np
import jax
import jax.numpy as jnp
from jax import lax
from jax.experimental import pallas as pl
from jax.experimental.pallas import tpu as pltpu

D_MODEL = 1024
DEPTH = 4
A_HEADS = 6
A_HEAD_DIM = 64
A_CONFIGS = ((128, 1), (512, 4), (2048, 16))
B_HEADS = 4
B_QK_DIM = 64
B_V_DIM = 128
C_HEADS = 6
C_NOPE_DIM = 64
C_ROPE_DIM = 32
C_V_DIM = 64
C_Q_RANK = 256
C_KV_RANK = 128
ROPE_BASE = 10000.0
N_BUCKETS = 32
MAX_DISTANCE = 2048
PEER_HEADS = 8
PEER_KEYS = 128
PEER_EXPERTS = PEER_KEYS * PEER_KEYS
PEER_KEY_DIM = 128
PEER_TOPK = 16
RMS_EPS = 1e-6
NEG_INF = -1e30
LOG2E = math.log2(math.e)

A_COLS = 3 * A_HEADS * A_HEAD_DIM
B_QK_COLS = B_HEADS * 2 * B_QK_DIM
B_V_COLS = B_HEADS * B_V_DIM

LANES = 128
SUBLANES = 8
VMEM_LIMIT = 56 * 1024 * 1024

F32 = jnp.float32
BF16 = jnp.bfloat16

TOKEN_TILE = 512
PROJ_TOKEN_TILE = 2048
PROJ_COL_TILE = 512
A_BLOCK = 128
A_BLOCKS_PER_STEP = 4
FLASH_Q_TILE = 512
FLASH_K_TILE = 256
FLASH_ROW_BLOCK = 128
FLASH_UNROLLS = (8, 4, 2)
PEER_TOKEN_TILE = 512
PEER_EXPERT_TILE = 2048


def _nt_dot(a, b):
    return lax.dot_general(a, b, (((1,), (1,)), ((), ())), preferred_element_type=F32)


def _gelu(x):
    return 0.5 * x * (1.0 + lax.erf(x * math.sqrt(0.5)))


def _rms(x, g):
    return x * lax.rsqrt(jnp.mean(x * x, axis=-1, keepdims=True) + RMS_EPS) * g


def _norm_matmul_kernel(x_ref, g_ref, w_ref, o_ref, h_scr):
    @pl.when(pl.program_id(1) == 0)
    def _():
        h_scr[...] = _rms(x_ref[...], g_ref[...]).astype(h_scr.dtype)

    o_ref[...] = jnp.dot(h_scr[...], w_ref[...], preferred_element_type=F32).astype(o_ref.dtype)


def norm_matmul(x, g, w, out_dtype):
    t, k = x.shape
    n = w.shape[1]
    tm, tn = min(PROJ_TOKEN_TILE, t), PROJ_COL_TILE
    return pl.pallas_call(
        _norm_matmul_kernel,
        grid=(t // tm, n // tn),
        in_specs=[pl.BlockSpec((tm, k), lambda i, j: (i, 0)),
                  pl.BlockSpec((1, k), lambda i, j: (0, 0)),
                  pl.BlockSpec((k, tn), lambda i, j: (0, j))],
        out_specs=pl.BlockSpec((tm, tn), lambda i, j: (i, j)),
        out_shape=jax.ShapeDtypeStruct((t, n), out_dtype),
        scratch_shapes=[pltpu.VMEM((tm, k), BF16)],
        compiler_params=pltpu.CompilerParams(dimension_semantics=("parallel", "arbitrary"),
                                             vmem_limit_bytes=VMEM_LIMIT),
        name="norm_matmul",
    )(x, g.reshape(1, k), w)


def _norm_matmul_dilated_kernel(x_ref, g_ref, w_ref, *rest, dilations):
    out_refs, (h_scr, res_scr) = rest[:len(dilations)], rest[len(dilations):]
    @pl.when(pl.program_id(1) == 0)
    def _():
        h_scr[...] = _rms(x_ref[...], g_ref[...]).astype(h_scr.dtype)

    res = jnp.dot(h_scr[...], w_ref[...], preferred_element_type=F32)
    n_lane_blocks, tm, _ = res_scr.shape
    tn = n_lane_blocks * LANES
    for c in range(n_lane_blocks):
        res_scr[c] = res[:, c * LANES:(c + 1) * LANES]
    for d, o_ref in zip(dilations, out_refs):
        for r in range(d):
            for c in range(n_lane_blocks):
                cols = slice(r * tn + c * LANES, r * tn + (c + 1) * LANES)
                o_ref[:, cols] = res_scr[c, pl.ds(r, tm // d, stride=d), :].astype(o_ref.dtype)


def norm_matmul_dilated(x, g, w, dilations, tn):
    t, k = x.shape
    n = w.shape[1]
    tm = min(PROJ_TOKEN_TILE, t)
    assert n % tn == 0 and all(tm % (d * SUBLANES) == 0 for d in dilations)
    return pl.pallas_call(
        functools.partial(_norm_matmul_dilated_kernel, dilations=dilations),
        grid=(t // tm, n // tn),
        in_specs=[pl.BlockSpec((tm, k), lambda i, j: (i, 0)),
                  pl.BlockSpec((1, k), lambda i, j: (0, 0)),
                  pl.BlockSpec((k, tn), lambda i, j: (0, j))],
        out_specs=[pl.BlockSpec((tm // d, d * tn), lambda i, j: (i, j)) for d in dilations],
        out_shape=[jax.ShapeDtypeStruct((t // d, d * n), BF16) for d in dilations],
        scratch_shapes=[pltpu.VMEM((tm, k), BF16), pltpu.VMEM((tn // LANES, tm, LANES), F32)],
        compiler_params=pltpu.CompilerParams(dimension_semantics=("parallel", "arbitrary"),
                                             vmem_limit_bytes=VMEM_LIMIT),
        name="norm_matmul_dilated",
    )(x, g.reshape(1, k), w)


A_Q_WIDTH = A_HEADS * LANES
A_PAIRS = A_HEADS // 2
A_KV_WIDTH = 2 * A_PAIRS * LANES
A_OUT_WIDTH = A_PAIRS * LANES


def _local_attn_kernel(q_ref, kvc_ref, kvp_ref, bias_ref, o_ref, lse_ref, kv_scr, *, nblk):
    w = A_BLOCK
    first_class_block = pl.program_id(2) == 0
    kv_scr[0:w, :] = kvp_ref[0]
    kv_scr[w:, :] = kvc_ref[0]
    lane = lax.broadcasted_iota(jnp.int32, (w, LANES), 1)
    col = lax.broadcasted_iota(jnp.int32, (w, 2 * w), 1)
    no_prev = jnp.where(col < w, jnp.where(first_class_block, NEG_INF, 0.0), 0.0)
    ones = jnp.ones((2 * w, LANES), BF16)
    for i in range(nblk):
        rows = slice(i * w, (i + 1) * w)
        for p in range(A_PAIRS):
            k_blk = kv_scr[i * w:(i + 2) * w, p * LANES:(p + 1) * LANES]
            v_blk = kv_scr[i * w:(i + 2) * w, (A_PAIRS + p) * LANES:(A_PAIRS + p + 1) * LANES]
            heads = (2 * p, 2 * p + 1)
            q = jnp.concatenate([q_ref[0, rows, hd * LANES:(hd + 1) * LANES] for hd in heads], axis=0)
            bias = jnp.concatenate([bias_ref[hd] for hd in heads], axis=0)
            if i == 0:
                bias = bias + jnp.concatenate([no_prev, no_prev], axis=0)
            s = _nt_dot(q, k_blk) + bias
            m = jnp.max(s, axis=-1, keepdims=True)
            pr = jnp.exp(s - m)
            pv = jnp.dot(pr.astype(BF16), jnp.concatenate([v_blk, ones], axis=1), preferred_element_type=F32)
            l = pv[:, LANES:]
            o = pv[:, :LANES] / l
            lse = m + jnp.log(l)
            o_ref[0, rows, p * LANES:(p + 1) * LANES] = jnp.where(lane < A_HEAD_DIM, o[:w], o[w:])
            lse_ref[0, rows, p * LANES:(p + 1) * LANES] = jnp.where(lane < A_HEAD_DIM, lse[:w], lse[w:])


def local_attn(view, bias, dilation):
    bsz, rows, width = view.shape
    d = dilation
    w = A_BLOCK
    nblk = min(A_BLOCKS_PER_STEP, rows // w)
    assert rows % (w * nblk) == 0 and width == d * (A_Q_WIDTH + A_KV_WIDTH) and A_Q_WIDTH == A_KV_WIDTH
    grid = (bsz, d, rows // (w * nblk))
    out_shape = jax.ShapeDtypeStruct((bsz, rows, d * A_OUT_WIDTH), F32)
    o, lse = pl.pallas_call(
        functools.partial(_local_attn_kernel, nblk=nblk),
        grid=grid,
        in_specs=[pl.BlockSpec((1, w * nblk, A_Q_WIDTH), lambda b, r, n: (b, n, r)),
                  pl.BlockSpec((1, w * nblk, A_KV_WIDTH), lambda b, r, n: (b, n, d + r)),
                  pl.BlockSpec((1, w, A_KV_WIDTH), lambda b, r, n: (b, jnp.maximum(n * nblk - 1, 0), d + r)),
                  pl.BlockSpec((A_HEADS, w, 2 * w), lambda b, r, n: (0, 0, 0))],
        out_specs=[pl.BlockSpec((1, w * nblk, A_OUT_WIDTH), lambda b, r, n: (b, n, r)),
                   pl.BlockSpec((1, w * nblk, A_OUT_WIDTH), lambda b, r, n: (b, n, r))],
        out_shape=[out_shape, out_shape],
        scratch_shapes=[pltpu.VMEM((w * (nblk + 1), A_KV_WIDTH), BF16)],
        compiler_params=pltpu.CompilerParams(dimension_semantics=("parallel", "parallel", "arbitrary")),
        name=f"local_attn_d{d}",
    )(view, view, view, bias)
    return o, lse


def _flash_pair_kernel(*refs, tq, tk, n_near, shared_k, mode):
    if shared_k:
        qa_ref, qb_ref, ka_ref, v_ref, bias_ref = refs[:5]
        kb_ref = ka_ref
        rest = refs[5:]
    else:
        qa_ref, qb_ref, ka_ref, kb_ref, v_ref, bias_ref = refs[:6]
        rest = refs[6:]
    if mode == "diff":
        prm_ref, gsub_ref, o_ref, m_scr, alpha_scr, acc_scr, s0_scr, s1_scr, p_scr = rest
    else:
        o_ref, m_scr, alpha_scr, acc_scr, s0_scr, s1_scr, p_scr = rest

    qi = pl.program_id(2)
    rb = FLASH_ROW_BLOCK
    m_scr[...] = jnp.full(m_scr.shape, NEG_INF, F32)
    acc_scr[...] = jnp.zeros(acc_scr.shape, F32)
    ones = jnp.ones((tk, LANES), BF16)
    last_kt = 2 * qi + 1

    def scores(kt, s_scr):
        koff = pl.multiple_of(jnp.minimum(kt, last_kt) * tk, tk)
        ka = ka_ref[0, pl.ds(koff, tk), :]
        if shared_k:
            s_scr[...] = _nt_dot(jnp.concatenate([qa_ref[0], qb_ref[0]], axis=0), ka)
        else:
            s_scr[0:tq, :] = _nt_dot(qa_ref[0], ka)
            s_scr[tq:, :] = _nt_dot(qb_ref[0], kb_ref[0, pl.ds(koff, tk), :])

    def accumulate(kt, s_scr, near):
        koff = pl.multiple_of(kt * tk, tk)
        for r in range(2 * tq // rb):
            rows = slice(r * rb, (r + 1) * rb)
            s = s_scr[rows, :]
            if near:
                b0 = (r * rb) % tq
                s = s + bias_ref[0, last_kt - kt, b0:b0 + rb, :]
            m_prev = m_scr[rows, :]
            m_new = jnp.maximum(m_prev, jnp.max(s, axis=-1, keepdims=True))
            alpha_scr[rows, :] = jnp.exp2(m_prev - m_new)
            m_scr[rows, :] = m_new
            p_scr[rows, :] = jnp.exp2(s - jnp.concatenate([m_new] * (tk // LANES), axis=1)).astype(BF16)
        v_ones = jnp.concatenate([v_ref[0, pl.ds(koff, tk), :], ones], axis=1)
        pv = jnp.dot(p_scr[...], v_ones, preferred_element_type=F32)
        alpha = alpha_scr[...]
        acc_scr[...] = jnp.concatenate([alpha, alpha], axis=1) * acc_scr[...] + pv

    n_tiles = 2 * qi + 2
    n_far = jnp.maximum(n_tiles - n_near, 0)

    def run(first, count, near):
        def unrolled(kt, steps):
            for step in range(steps):
                nxt, cur = (s1_scr, s0_scr) if step % 2 == 0 else (s0_scr, s1_scr)
                scores(kt + step + 1, nxt)
                accumulate(kt + step, cur, near)

        done = 0
        for steps in FLASH_UNROLLS:
            def body(i, carry, steps=steps, done=done):
                unrolled(first + done + steps * i, steps)
                return carry

            trips = (count - done) // steps
            lax.fori_loop(0, trips, body, 0)
            done = done + trips * steps

    scores(0, s0_scr)
    run(0, n_far, False)
    run(n_far, n_tiles - n_far, True)

    o = acc_scr[:, :LANES] / acc_scr[:, LANES:]
    oa, ob = o[:tq], o[tq:]
    if mode == "diff":
        prm = prm_ref[...]
        lam_init = prm[4:5, 0:1]
        lam = (jnp.exp(jnp.sum(prm[0:1] * prm[1:2], axis=-1, keepdims=True))
               - jnp.exp(jnp.sum(prm[2:3] * prm[3:4], axis=-1, keepdims=True)) + lam_init)
        d = oa - lam * ob
        o_ref[0] = (_rms(d, gsub_ref[...]) * (1.0 - lam_init)).astype(o_ref.dtype)
    else:
        lane = lax.broadcasted_iota(jnp.int32, oa.shape, 1)
        o_ref[0] = jnp.where(lane < C_V_DIM, oa, ob).astype(o_ref.dtype)


def _flash_pair_call(mode, q_arr, k_arr, v_arr, bias, groups, col_maps, out_cols, extra):
    bsz, seq, _ = q_arr.shape
    tq, tk = FLASH_Q_TILE, FLASH_K_TILE
    assert seq % tq == 0 and tq == 2 * tk
    n_near = bias.shape[1]
    assert n_near % 2 == 0
    qa_c, qb_c, ka_c, kb_c, v_c = col_maps
    shared_k = kb_c is None
    q_spec = lambda cm: pl.BlockSpec((1, tq, LANES), lambda b, g, i: (b, i, cm(g)))
    kv_spec = lambda cm: pl.BlockSpec((1, seq, LANES), lambda b, g, i: (b, 0, cm(g)))
    in_specs = [q_spec(qa_c), q_spec(qb_c), kv_spec(ka_c)]
    args = [q_arr, q_arr, k_arr]
    if not shared_k:
        in_specs.append(kv_spec(kb_c))
        args.append(k_arr)
    in_specs.append(kv_spec(v_c))
    args.append(v_arr)
    bias_group = (lambda g: g) if bias.shape[0] > 1 else (lambda g: 0)
    in_specs.append(pl.BlockSpec((1, n_near, tq, tk), lambda b, g, i: (bias_group(g), 0, 0, 0)))
    args.append(bias)
    for e in extra:
        in_specs.append(pl.BlockSpec(e.shape, lambda b, g, i: (0, 0)))
        args.append(e)
    return pl.pallas_call(
        functools.partial(_flash_pair_kernel, tq=tq, tk=tk, n_near=n_near, shared_k=shared_k, mode=mode),
        grid=(bsz, groups, seq // tq),
        in_specs=in_specs,
        out_specs=pl.BlockSpec((1, tq, LANES), lambda b, g, i: (b, i, g)),
        out_shape=jax.ShapeDtypeStruct((bsz, seq, out_cols), BF16),
        scratch_shapes=[pltpu.VMEM((2 * tq, LANES), F32), pltpu.VMEM((2 * tq, LANES), F32),
                        pltpu.VMEM((2 * tq, 2 * LANES), F32), pltpu.VMEM((2 * tq, tk), F32),
                        pltpu.VMEM((2 * tq, tk), F32), pltpu.VMEM((2 * tq, tk), BF16)],
        compiler_params=pltpu.CompilerParams(dimension_semantics=("parallel", "parallel", "arbitrary"),
                                             vmem_limit_bytes=VMEM_LIMIT),
        name=f"flash_pair_{mode}",
    )(*args)


B_Q_BLOCKS = 2 * B_HEADS
B_WIDTH_PADDED = (B_Q_BLOCKS + 2 * B_HEADS) * LANES


def diff_attn(qkv, bias, lam_prm, g_subln):
    return _flash_pair_call(
        "diff", qkv, qkv, qkv, bias, B_HEADS,
        (lambda g: 2 * g, lambda g: 2 * g + 1, lambda g: B_Q_BLOCKS + g, None,
         lambda g: B_Q_BLOCKS + B_HEADS + g),
        B_HEADS * B_V_DIM, [lam_prm, g_subln.reshape(1, B_V_DIM)])


def mla_attn(q, k, v, mask):
    return _flash_pair_call(
        "mla", q, k, v, mask, C_HEADS // 2,
        (lambda g: 2 * g, lambda g: 2 * g + 1, lambda g: 2 * g, lambda g: 2 * g + 1, lambda g: g),
        C_HEADS * C_V_DIM, [])


C_QK_WIDTH = C_HEADS * LANES
C_V_WIDTH = C_HEADS * C_V_DIM
C_LATENT_WIDTH = C_Q_RANK + C_KV_RANK + LANES
ROPE_SHIFT = LANES - C_ROPE_DIM


def _mla_prep_kernel(c_ref, gq_ref, gkv_ref, wq_ref, wk_ref, wv_ref, cos_ref, sin_ref, q_ref, k_ref, v_ref):
    c = c_ref[...]
    cos = cos_ref[...]
    sin = sin_ref[...]

    def rope(blk):
        return blk * cos + pltpu.roll(blk, ROPE_SHIFT, 1) * sin

    cq = _rms(c[:, :C_Q_RANK], gq_ref[...]).astype(BF16)
    ckv = _rms(c[:, C_Q_RANK:C_Q_RANK + C_KV_RANK], gkv_ref[...]).astype(BF16)
    q_all = jnp.dot(cq, wq_ref[...], preferred_element_type=F32)
    k_all = jnp.dot(ckv, wk_ref[...], preferred_element_type=F32)
    k_rope = rope(c[:, C_Q_RANK + C_KV_RANK:])
    scale = LOG2E / math.sqrt(C_NOPE_DIM + C_ROPE_DIM)
    for h in range(C_HEADS):
        cols = slice(h * LANES, (h + 1) * LANES)
        q_ref[:, cols] = (rope(q_all[:, cols]) * scale).astype(q_ref.dtype)
        k_ref[:, cols] = (k_all[:, cols] + k_rope).astype(k_ref.dtype)
    v_ref[...] = jnp.dot(ckv, wv_ref[...], preferred_element_type=F32).astype(v_ref.dtype)


def mla_prep(c_lat, c_col_block, g_cq, g_ckv, w_q, w_k, w_v, cos_t, sin_t, seq):
    t = c_lat.shape[0]
    tm = min(TOKEN_TILE, seq)
    per_seq = seq // tm
    full = lambda a: pl.BlockSpec(a.shape, lambda i: (0, 0))
    g_cq = g_cq.reshape(1, -1)
    g_ckv = g_ckv.reshape(1, -1)
    return pl.pallas_call(
        _mla_prep_kernel,
        grid=(t // tm,),
        in_specs=[pl.BlockSpec((tm, C_LATENT_WIDTH), lambda i: (i, c_col_block)),
                  full(g_cq), full(g_ckv), full(w_q), full(w_k), full(w_v),
                  pl.BlockSpec((tm, LANES), lambda i: (i % per_seq, 0)),
                  pl.BlockSpec((tm, LANES), lambda i: (i % per_seq, 0))],
        out_specs=[pl.BlockSpec((tm, C_QK_WIDTH), lambda i: (i, 0)),
                   pl.BlockSpec((tm, C_QK_WIDTH), lambda i: (i, 0)),
                   pl.BlockSpec((tm, C_V_WIDTH), lambda i: (i, 0))],
        out_shape=[jax.ShapeDtypeStruct((t, C_QK_WIDTH), BF16), jax.ShapeDtypeStruct((t, C_QK_WIDTH), BF16),
                   jax.ShapeDtypeStruct((t, C_V_WIDTH), BF16)],
        compiler_params=pltpu.CompilerParams(dimension_semantics=("parallel",)),
        name="mla_prep",
    )(c_lat, g_cq, g_ckv, w_q, w_k, w_v, cos_t, sin_t)


def _merge_proj_kernel(x_ref, ga_ref, gb_ref, gc_ref, o1_ref, o2_ref, o3_ref, l1_ref, l2_ref, l3_ref,
                       yb_ref, yc_ref, wa_ref, wb_ref, wc_ref, wo_ref, gffn_ref, wpq_ref,
                       xo_ref, h2_ref, qp_ref, *token_order_scr):
    tm = x_ref.shape[0]

    def token_order(ref, scr):
        d = ref.shape[1] // A_OUT_WIDTH
        if d == 1:
            return ref[...]
        for r in range(d):
            for c in range(A_PAIRS):
                cols = slice(r * A_OUT_WIDTH + c * LANES, r * A_OUT_WIDTH + (c + 1) * LANES)
                scr[c, pl.ds(r, tm // d, stride=d), :] = ref[:, cols]
        return jnp.concatenate([scr[c] for c in range(A_PAIRS)], axis=1)

    o1, o2, o3, l1, l2, l3 = [token_order(ref, scr) for ref, scr in
                              zip((o1_ref, o2_ref, o3_ref, l1_ref, l2_ref, l3_ref), token_order_scr)]
    m = jnp.maximum(jnp.maximum(l1, l2), l3)
    e1, e2, e3 = jnp.exp(l1 - m), jnp.exp(l2 - m), jnp.exp(l3 - m)
    den = e1 + e2 + e3
    ya = (e1 / den) * o1 + (e2 / den) * o2 + (e3 / den) * o3
    pa = jnp.dot(ya.astype(BF16), wa_ref[...], preferred_element_type=F32)
    pb = jnp.dot(yb_ref[...], wb_ref[...], preferred_element_type=F32)
    pc = jnp.dot(yc_ref[...], wc_ref[...], preferred_element_type=F32)
    mix = (jax.nn.sigmoid(ga_ref[...]) * pa + jax.nn.sigmoid(gb_ref[...]) * pb
           + jax.nn.sigmoid(gc_ref[...]) * pc)
    x_new = x_ref[...] + jnp.dot(mix.astype(BF16), wo_ref[...], preferred_element_type=F32)
    xo_ref[...] = x_new
    h2 = _rms(x_new, gffn_ref[...]).astype(BF16)
    h2_ref[...] = h2
    qp_ref[...] = jnp.dot(h2, wpq_ref[...], preferred_element_type=F32)


def merge_proj(x, gc_arr, a_outs, a_lses, y_b, y_c, w_a, w_b, w_c, w_o, g_ffn, w_pq):
    t, d = x.shape
    tm = min(TOKEN_TILE, t)
    row = lambda width, col=0: pl.BlockSpec((tm, width), lambda i: (i, col))
    full = lambda a: pl.BlockSpec(a.shape, lambda i: (0, 0))
    g_ffn = g_ffn.reshape(1, d)
    group = lambda a: pl.BlockSpec((tm // (a.shape[1] // A_OUT_WIDTH), a.shape[1]), lambda i: (i, 0))
    in_specs = ([row(d)] + [row(d, c) for c in range(3)]
                + [group(a) for a in (*a_outs, *a_lses)] + [row(y_b.shape[1]), row(y_c.shape[1])]
                + [full(w_a), full(w_b), full(w_c), full(w_o), full(g_ffn), full(w_pq)])
    return pl.pallas_call(
        _merge_proj_kernel,
        grid=(t // tm,),
        in_specs=in_specs,
        out_specs=[row(d), row(d), row(d)],
        out_shape=[jax.ShapeDtypeStruct((t, d), F32), jax.ShapeDtypeStruct((t, d), BF16),
                   jax.ShapeDtypeStruct((t, d), F32)],
        scratch_shapes=[pltpu.VMEM((A_PAIRS, tm, LANES), F32)] * 6,
        compiler_params=pltpu.CompilerParams(dimension_semantics=("parallel",), vmem_limit_bytes=VMEM_LIMIT),
        name="merge_proj",
    )(x, gc_arr, gc_arr, gc_arr, *a_outs, *a_lses, y_b, y_c, w_a, w_b, w_c, w_o, g_ffn, w_pq)


def _oddeven_merge_sort_pairs(n):
    pairs = []
    p = 1
    while p < n:
        k = p
        while k >= 1:
            for j in range(k % p, n - k, 2 * k):
                for i in range(min(k, n - j - k)):
                    if (i + j) // (2 * p) == (i + j + k) // (2 * p):
                        pairs.append((i + j, i + j + k))
            k //= 2
        p *= 2
    return pairs


def _bitonic_merge_pairs(n):
    pairs = []
    stride = n // 2
    while stride >= 1:
        pairs += [(i, i + stride) for i in range(n) if not i & stride]
        stride //= 2
    return pairs


def _compare_exchange(vals, pairs):
    for i, j in pairs:
        a, b = vals[i], vals[j]
        if b is None:
            continue
        if a is None:
            vals[i], vals[j] = b, None
        else:
            vals[i], vals[j] = jnp.maximum(a, b), jnp.minimum(a, b)
    return vals


_SORT16 = _oddeven_merge_sort_pairs(PEER_TOPK)
_MERGE16 = _bitonic_merge_pairs(PEER_TOPK)
_CAND_PAIRS = [(a, b) for a in range(PEER_TOPK) for b in range(PEER_TOPK) if (a + 1) * (b + 1) <= PEER_TOPK]
_SORT64 = _oddeven_merge_sort_pairs(64)


def _top16_sorted(scores_t):
    slabs = [scores_t[v * SUBLANES:(v + 1) * SUBLANES, :] for v in range(PEER_KEYS // SUBLANES)]
    slabs = _compare_exchange(slabs, _SORT16)
    shift = SUBLANES // 2
    while shift >= 1:
        other = [pltpu.roll(s, shift, 0) for s in slabs]
        slabs = [jnp.maximum(slabs[r], other[PEER_TOPK - 1 - r]) for r in range(PEER_TOPK)]
        slabs = _compare_exchange(slabs, _MERGE16)
        shift //= 2
    return slabs


def _peer_select_kernel(qp_ref, keys_ref, theta_ref, s2_ref, c1_ref, e2_ref, s1_scr):
    tm = qp_ref.shape[0]
    sub = lax.broadcasted_iota(jnp.int32, (SUBLANES, tm), 0)
    tops = [[None] * PEER_TOPK for _ in range(2)]
    for h in range(PEER_HEADS):
        qh = qp_ref[:, h * LANES:(h + 1) * LANES].astype(BF16)
        for p in range(2):
            s_t = _nt_dot(keys_ref[2 * h + p], qh)
            if p == 0:
                s1_scr[h] = s_t
            else:
                for c in range(tm // LANES):
                    s2_ref[h, c] = s_t[:, c * LANES:(c + 1) * LANES]
            top = _top16_sorted(s_t)
            for r in range(PEER_TOPK):
                tops[p][r] = top[r] if h == 0 else jnp.where(sub == h, top[r], tops[p][r])
    cand = [tops[0][a] + tops[1][b] for a, b in _CAND_PAIRS] + [None] * (64 - len(_CAND_PAIRS))
    best = _compare_exchange(cand, _SORT64)[:PEER_TOPK]
    mx = best[0]
    z = functools.reduce(lambda u, w_: u + w_, [jnp.exp(b - mx) for b in best])
    thr = best[PEER_TOPK - 1]
    m1, m2 = tops[0][0], tops[1][0]

    def smallest_selected(s1_vals, thr_vals, t2_of, n_b):
        theta = jnp.full(s1_vals.shape, -NEG_INF, F32)
        for b in range(n_b):
            theta = jnp.where(s1_vals + t2_of(b) >= thr_vals, t2_of(b), theta)
        return theta

    n_best = 3
    short = PEER_TOPK // (n_best + 1)
    theta_best = [smallest_selected(tops[0][a], thr, lambda b: tops[1][b], PEER_TOPK) for a in range(n_best)]
    for h in range(PEER_HEADS):
        row = lambda a: a[h:h + 1, :]
        s1 = s1_scr[h]
        theta = smallest_selected(s1, row(thr), lambda b: row(tops[1][b]), short)
        for a in range(n_best):
            theta = jnp.where(s1 == row(tops[0][a]), row(theta_best[a]), theta)
        theta_ref[h] = theta
        c1_ref[h] = jnp.exp(s1 - row(m1)) / row(z)
        for c in range(tm // LANES):
            e2_ref[h, c] = jnp.exp(s2_ref[h, c] - m2[h:h + 1, c * LANES:(c + 1) * LANES])


def peer_select(qp, keys_padded):
    t = qp.shape[0]
    tm = min(PEER_TOKEN_TILE, t)
    first = jax.ShapeDtypeStruct((PEER_HEADS, PEER_KEYS, t), F32)
    first_spec = pl.BlockSpec((PEER_HEADS, PEER_KEYS, tm), lambda i: (0, 0, i))
    second = jax.ShapeDtypeStruct((PEER_HEADS, t // LANES, PEER_KEYS, LANES), F32)
    second_spec = pl.BlockSpec((PEER_HEADS, tm // LANES, PEER_KEYS, LANES), lambda i: (0, i, 0, 0))
    return pl.pallas_call(
        _peer_select_kernel,
        grid=(t // tm,),
        in_specs=[pl.BlockSpec((tm, PEER_HEADS * PEER_KEY_DIM), lambda i: (i, 0)),
                  pl.BlockSpec(keys_padded.shape, lambda i: (0, 0, 0))],
        out_specs=[first_spec, second_spec, first_spec, second_spec],
        out_shape=[first, second, first, second],
        scratch_shapes=[pltpu.VMEM((PEER_HEADS, PEER_KEYS, tm), F32)],
        compiler_params=pltpu.CompilerParams(dimension_semantics=("parallel",), vmem_limit_bytes=VMEM_LIMIT),
        name="peer_select",
    )(qp, keys_padded)


def _peer_dense_kernel(h_ref, u_ref, vt_ref, theta_ref, theta_nxt_ref, s2_ref, c1_ref, c1_nxt_ref, e2_ref,
                       x_ref, g_ref, o_ref, acc_scr, gate_scr, *, final_norm):
    j = pl.program_id(1)
    tm = h_ref.shape[0]
    ec = u_ref.shape[0]
    rows_per_tile = ec // PEER_KEYS
    n_chunks = tm // LANES
    part = ec // n_chunks
    cur, nxt = j % 2, (j + 1) % 2

    def build(tc, th_ref, c_ref, slot):
        cols = pl.ds(pl.multiple_of(tc * LANES, LANES), LANES)
        for ii in range(rows_per_tile):
            rows = slice(ii * PEER_KEYS, (ii + 1) * PEER_KEYS)
            w = None
            for h in range(PEER_HEADS):
                gate = e2_ref[h, tc] * c_ref[h, ii:ii + 1, cols]
                term = jnp.where(s2_ref[h, tc] >= th_ref[h, ii:ii + 1, cols], gate, 0.0)
                w = term if w is None else w + term
            gate_scr[slot, tc, rows, :] = w.astype(BF16)

    @pl.when(j == 0)
    def _():
        acc_scr[...] = jnp.zeros(acc_scr.shape, F32)

        def first(tc, carry):
            build(tc, theta_ref, c1_ref, cur)
            return carry

        lax.fori_loop(0, n_chunks, first, 0)

    def body(q, carry):
        rows = pl.ds(pl.multiple_of(q * part, part), part)
        act = _gelu(_nt_dot(u_ref[rows, :], h_ref[...]))
        gate = jnp.concatenate([gate_scr[cur, tc, rows, :] for tc in range(n_chunks)], axis=1)
        wa = (act * gate.astype(F32)).astype(BF16)
        acc_scr[...] += jnp.dot(vt_ref[:, rows], wa, preferred_element_type=F32)
        build(q, theta_nxt_ref, c1_nxt_ref, nxt)
        return carry

    lax.fori_loop(0, n_chunks, body, 0)

    @pl.when(j == pl.num_programs(1) - 1)
    def _():
        y = x_ref[...] + acc_scr[...].T
        if final_norm:
            y = _rms(y, g_ref[...])
        o_ref[...] = y


def peer_dense(h2, u_bf, vt_bf, theta, s2, c1, e2, x, g_final, final_norm):
    t, d = x.shape
    n_exp = u_bf.shape[0]
    tm = min(PEER_TOKEN_TILE, t)
    ec = PEER_EXPERT_TILE
    n_tiles = n_exp // ec
    sel_spec = pl.BlockSpec((PEER_HEADS, tm // LANES, PEER_KEYS, LANES), lambda i, j: (0, i, 0, 0))
    key_spec = pl.BlockSpec((PEER_HEADS, ec // PEER_KEYS, tm), lambda i, j: (0, j, i))
    key_nxt_spec = pl.BlockSpec((PEER_HEADS, ec // PEER_KEYS, tm), lambda i, j: (0, jnp.minimum(j + 1, n_tiles - 1), i))
    return pl.pallas_call(
        functools.partial(_peer_dense_kernel, final_norm=final_norm),
        grid=(t // tm, n_tiles),
        in_specs=[pl.BlockSpec((tm, d), lambda i, j: (i, 0)),
                  pl.BlockSpec((ec, d), lambda i, j: (j, 0)),
                  pl.BlockSpec((d, ec), lambda i, j: (0, j)),
                  key_spec, key_nxt_spec, sel_spec, key_spec, key_nxt_spec, sel_spec,
                  pl.BlockSpec((tm, d), lambda i, j: (i, 0)),
                  pl.BlockSpec((1, d), lambda i, j: (0, 0))],
        out_specs=pl.BlockSpec((tm, d), lambda i, j: (i, 0)),
        out_shape=jax.ShapeDtypeStruct((t, d), F32),
        scratch_shapes=[pltpu.VMEM((d, tm), F32), pltpu.VMEM((2, tm // LANES, ec, LANES), BF16)],
        compiler_params=pltpu.CompilerParams(dimension_semantics=("parallel", "arbitrary"),
                                             vmem_limit_bytes=VMEM_LIMIT),
        name="peer_dense",
    )(h2, u_bf, vt_bf, theta, theta, s2, c1, c1, e2, x, g_final.reshape(1, d))


def _t5_bucket_table(max_dist):
    n = np.arange(max_dist + 1)
    max_exact = N_BUCKETS // 2
    nf = np.maximum(n, max_exact).astype(np.float32)
    large = max_exact + (np.log(nf / np.float32(max_exact)) / np.float32(math.log(MAX_DISTANCE / max_exact))
                         * np.float32(N_BUCKETS - max_exact)).astype(np.int32)
    large = np.minimum(large, N_BUCKETS - 1)
    return np.where(n < max_exact, n, large).astype(np.int32)


def _skew(vec, rows):
    h, length = vec.shape
    return jnp.tile(vec, (1, rows))[:, :rows * (length - 1)].reshape(h, rows, length - 1)


def _banded_bias(bias_heads, dilation):
    w = A_BLOCK
    heads = bias_heads.shape[1]
    by_rel = bias_heads[_t5_bucket_table(w * dilation)[np.arange(w + 1) * dilation]].T.astype(F32)
    vec = jnp.concatenate([by_rel, jnp.full((heads, 3 * w - (w + 1)), NEG_INF, F32)], axis=1)
    return jnp.transpose(_skew(vec, 2 * w)[:, :, w:2 * w], (0, 2, 1))


def _diagonal_tiles(by_dist, tq, tk, n_tiles):
    heads = by_dist.shape[0]
    n_cols = (n_tiles - 1) * tk + tq
    mask = jnp.full((heads, tk), NEG_INF, F32)
    vec = jnp.concatenate([mask, by_dist, mask], axis=1)
    assert vec.shape[1] == n_cols + tk
    skew = _skew(vec, tk)
    tiles = jnp.stack([skew[:, :, i * tk:i * tk + tq] for i in range(n_tiles)], axis=1)
    return jnp.transpose(tiles, (0, 1, 3, 2))


def _causal_bias_tiles(bias_heads, tq, tk):
    n_tiles = 2 * -(-(MAX_DISTANCE - 1 + tk + tk) // (2 * tk))
    far = bias_heads[N_BUCKETS - 1]
    n_dist = (n_tiles - 2) * tk + tq
    by_dist = (bias_heads - far[None, :])[_t5_bucket_table(n_dist - 1)].T.astype(F32) * LOG2E
    return _diagonal_tiles(by_dist, tq, tk, n_tiles)


def _causal_mask_tiles(tq, tk):
    return _diagonal_tiles(jnp.zeros((1, tq), F32), tq, tk, 2)


def _rope_tables(seq):
    half = C_ROPE_DIM // 2
    inv = ROPE_BASE ** (-jnp.arange(half, dtype=F32) / half)
    ang = jnp.arange(seq, dtype=F32)[:, None] * inv[None, :]
    cos, sin = jnp.cos(ang), jnp.sin(ang)
    ones = jnp.ones((seq, C_NOPE_DIM), F32)
    zeros_n = jnp.zeros((seq, C_NOPE_DIM), F32)
    spare = jnp.zeros((seq, LANES - C_NOPE_DIM - C_ROPE_DIM), F32)
    cos_t = jnp.concatenate([ones, cos, cos, spare], axis=1)
    sin_t = jnp.concatenate([zeros_n, -sin, sin, spare], axis=1)
    return cos_t, sin_t


def _swap_halves(w):
    half = w.shape[-1] // 2
    return jnp.concatenate([w[..., half:], w[..., :half]], axis=-1)


def _prep_in_proj(w_in):
    d = w_in.shape[0]
    bounds = np.cumsum([A_COLS, B_QK_COLS, B_QK_COLS, B_V_COLS, C_Q_RANK, C_KV_RANK, C_ROPE_DIM])
    wa, wbq, wbk, wbv, wcq, wckv, wckr, wg = jnp.split(w_in, bounds, axis=1)
    zeros = lambda n: jnp.zeros((d, n), w_in.dtype)
    wa = wa.reshape(d, 3, A_HEADS, A_HEAD_DIM)
    qa = wa[:, 0] * (1.0 / math.sqrt(A_HEAD_DIM))
    q_blocks = []
    for h in range(A_HEADS):
        pad = zeros(A_HEAD_DIM)
        q_blocks += [qa[:, h], pad] if h % 2 == 0 else [pad, qa[:, h]]
    w_a = jnp.concatenate(q_blocks + [wa[:, 1].reshape(d, -1), wa[:, 2].reshape(d, -1)], axis=1)
    qb = wbq.reshape(d, B_HEADS, 2, B_QK_DIM) * (LOG2E / math.sqrt(B_QK_DIM))
    q_blocks = []
    for h in range(B_HEADS):
        pad = zeros(B_QK_DIM)
        q_blocks += [qb[:, h, 0], pad, pad, qb[:, h, 1]]
    w_b = jnp.concatenate(q_blocks + [wbk, wbv], axis=1)
    w_gc = jnp.concatenate([wg, wcq, wckv, zeros(C_NOPE_DIM), wckr, _swap_halves(wckr)], axis=1)
    return w_a.astype(BF16), w_b.astype(BF16), w_gc.astype(BF16)


def _prep_mla_weights(w_uq, w_ukv):
    rq = w_uq.shape[0]
    wq = w_uq.reshape(rq, C_HEADS, C_NOPE_DIM + C_ROPE_DIM)
    q_rope = wq[..., C_NOPE_DIM:]
    wq = jnp.concatenate([wq, _swap_halves(q_rope)], axis=-1).reshape(rq, C_QK_WIDTH)
    rkv = w_ukv.shape[0]
    wkv = w_ukv.reshape(rkv, C_HEADS, C_NOPE_DIM + C_V_DIM)
    wk = jnp.concatenate([wkv[..., :C_NOPE_DIM], jnp.zeros((rkv, C_HEADS, LANES - C_NOPE_DIM), w_ukv.dtype)],
                         axis=-1).reshape(rkv, C_QK_WIDTH)
    wv = wkv[..., C_NOPE_DIM:].reshape(rkv, C_V_WIDTH)
    return wq.astype(BF16), wk.astype(BF16), wv.astype(BF16)


def _prep_peer_keys(sub_keys):
    half = PEER_KEY_DIM // 2
    z = jnp.zeros(sub_keys.shape[:1] + sub_keys.shape[2:], sub_keys.dtype)
    first = jnp.concatenate([sub_keys[:, 0], z], axis=-1)
    second = jnp.concatenate([z, sub_keys[:, 1]], axis=-1)
    assert first.shape[-1] == LANES and half * 2 == LANES
    return jnp.stack([first, second], axis=1).reshape(2 * PEER_HEADS, PEER_KEYS, LANES).astype(BF16)


def kernel(x, rel_bias, w_in, g_mix, w_uq, g_cq, w_ukv, g_ckv, lam_q1, lam_k1, lam_q2, lam_k2, g_subln,
           w_branch_a, w_branch_b, w_branch_c, w_out, g_ffn, w_peer_q, peer_sub_keys, peer_u, peer_v, g_final):
    bsz, seq, d = x.shape
    t = bsz * seq
    depth = w_in.shape[0]
    bias_a = rel_bias[:, :A_HEADS]
    bias_b = rel_bias[:, A_HEADS:]
    a_biases = [_banded_bias(bias_a, dil) for _, dil in A_CONFIGS]
    b_bias = _causal_bias_tiles(bias_b, FLASH_Q_TILE, FLASH_K_TILE)
    c_mask = _causal_mask_tiles(FLASH_Q_TILE, FLASH_K_TILE)
    cos_t, sin_t = _rope_tables(seq)
    gate_blocks = 3 * d // C_LATENT_WIDTH

    xf = x.reshape(t, d)
    for l in range(depth):
        w_a, w_b, w_gc = _prep_in_proj(w_in[l])
        dilations = tuple(dil for _, dil in A_CONFIGS)
        views_a = norm_matmul_dilated(xf, g_mix[l], w_a, dilations, A_Q_WIDTH)
        qkv_b = norm_matmul(xf, g_mix[l], w_b, BF16).reshape(bsz, seq, -1)
        gc = norm_matmul(xf, g_mix[l], w_gc, F32)

        a_outs, a_lses = [], []
        for dil, view, bias in zip(dilations, views_a, a_biases):
            o, lse = local_attn(view.reshape(bsz, seq // dil, -1), bias, dil)
            a_outs.append(o.reshape(t // dil, -1))
            a_lses.append(lse.reshape(t // dil, -1))

        lam_init = 0.8 - 0.6 * math.exp(-0.3 * l)
        pad = lambda v: jnp.pad(v.astype(F32), (0, LANES - v.shape[0]))
        lam_prm = jnp.stack([pad(lam_q1[l]), pad(lam_k1[l]), pad(lam_q2[l]), pad(lam_k2[l]),
                             jnp.full((LANES,), lam_init, F32)]
                            + [jnp.zeros((LANES,), F32)] * (SUBLANES - 5))
        y_b = diff_attn(qkv_b, b_bias, lam_prm, g_subln[l]).reshape(t, -1)

        wq, wk, wv = _prep_mla_weights(w_uq[l], w_ukv[l])
        q_c, k_c, v_c = mla_prep(gc, gate_blocks, g_cq[l], g_ckv[l], wq, wk, wv, cos_t, sin_t, seq)
        y_c = mla_attn(q_c.reshape(bsz, seq, -1), k_c.reshape(bsz, seq, -1), v_c.reshape(bsz, seq, -1),
                       c_mask).reshape(t, -1)

        xf, h2, qp = merge_proj(xf, gc, a_outs, a_lses, y_b, y_c,
                                w_branch_a[l].astype(BF16), w_branch_b[l].astype(BF16),
                                w_branch_c[l].astype(BF16), w_out[l].astype(BF16), g_ffn[l],
                                w_peer_q[l].astype(BF16))

        theta, s2, c1, e2 = peer_select(qp, _prep_peer_keys(peer_sub_keys[l]))
        xf = peer_dense(h2, peer_u[l].astype(BF16), peer_v[l].T.astype(BF16), theta, s2, c1, e2, xf,
                        g_final, final_norm=(l == depth - 1))
    return xf.reshape(bsz, seq, d)
```

```python
import functools
import math

import numpy as np
import jax
import jax.numpy as jnp
from jax import lax
from jax.experimental import pallas as pl
from jax.experimental.pallas import tpu as pltpu

D_MODEL = 1024
DEPTH = 4
A_HEADS = 6
A_HEAD_DIM = 64
A_CONFIGS = ((128, 1), (512, 4), (2048, 16))
B_HEADS = 4
B_QK_DIM = 64
B_V_DIM = 128
C_HEADS = 6
C_NOPE_DIM = 64
C_ROPE_DIM = 32
C_V_DIM = 64
C_Q_RANK = 256
C_KV_RANK = 128
ROPE_BASE = 10000.0
N_BUCKETS = 32
MAX_DISTANCE = 2048
PEER_HEADS = 8
PEER_KEYS = 128
PEER_EXPERTS = PEER_KEYS * PEER_KEYS
PEER_KEY_DIM = 128
PEER_TOPK = 16
RMS_EPS = 1e-6
NEG_INF = -1e30
LOG2E = math.log2(math.e)

A_COLS = 3 * A_HEADS * A_HEAD_DIM
B_QK_COLS = B_HEADS * 2 * B_QK_DIM
B_V_COLS = B_HEADS * B_V_DIM

LANES = 128
SUBLANES = 8
VMEM_LIMIT = 56 * 1024 * 1024

F32 = jnp.float32
BF16 = jnp.bfloat16

TOKEN_TILE = 512
PROJ_TOKEN_TILE = 2048
PROJ_COL_TILE = 512
A_BLOCK = 128
A_BLOCKS_PER_STEP = 4
FLASH_Q_TILE = 512
FLASH_K_TILE = 256
FLASH_ROW_BLOCK = 128
FLASH_UNROLLS = (8, 4, 2)
PEER_TOKEN_TILE = 512
PEER_EXPERT_TILE = 2048


def _nt_dot(a, b):
    return lax.dot_general(a, b, (((1,), (1,)), ((), ())), preferred_element_type=F32)


def _twice_gelu(x):
    return x * (1.0 + lax.erf(x * math.sqrt(0.5)))


def _rms(x, g):
    return x * lax.rsqrt(jnp.mean(x * x, axis=-1, keepdims=True) + RMS_EPS) * g


def _norm_matmul_kernel(x_ref, g_ref, w_ref, o_ref, h_scr):
    @pl.when(pl.program_id(1) == 0)
    def _():
        h_scr[...] = _rms(x_ref[...], g_ref[...]).astype(h_scr.dtype)

    o_ref[...] = jnp.dot(h_scr[...], w_ref[...], preferred_element_type=F32).astype(o_ref.dtype)


def norm_matmul(x, g, w, out_dtype):
    t, k = x.shape
    n = w.shape[1]
    tm, tn = min(PROJ_TOKEN_TILE, t), PROJ_COL_TILE
    return pl.pallas_call(
        _norm_matmul_kernel,
        grid=(t // tm, n // tn),
        in_specs=[pl.BlockSpec((tm, k), lambda i, j: (i, 0)),
                  pl.BlockSpec((1, k), lambda i, j: (0, 0)),
                  pl.BlockSpec((k, tn), lambda i, j: (0, j))],
        out_specs=pl.BlockSpec((tm, tn), lambda i, j: (i, j)),
        out_shape=jax.ShapeDtypeStruct((t, n), out_dtype),
        scratch_shapes=[pltpu.VMEM((tm, k), BF16)],
        compiler_params=pltpu.CompilerParams(dimension_semantics=("parallel", "arbitrary"),
                                             vmem_limit_bytes=VMEM_LIMIT),
        name="norm_matmul",
    )(x, g.reshape(1, k), w)


def _norm_matmul_dilated_kernel(x_ref, g_ref, w_ref, *rest, dilations):
    out_refs, (h_scr, res_scr) = rest[:len(dilations)], rest[len(dilations):]
    @pl.when(pl.program_id(1) == 0)
    def _():
        h_scr[...] = _rms(x_ref[...], g_ref[...]).astype(h_scr.dtype)

    res = jnp.dot(h_scr[...], w_ref[...], preferred_element_type=F32)
    n_lane_blocks, tm, _ = res_scr.shape
    tn = n_lane_blocks * LANES
    for c in range(n_lane_blocks):
        res_scr[c] = res[:, c * LANES:(c + 1) * LANES]
    for d, o_ref in zip(dilations, out_refs):
        for r in range(d):
            for c in range(n_lane_blocks):
                cols = slice(r * tn + c * LANES, r * tn + (c + 1) * LANES)
                o_ref[:, cols] = res_scr[c, pl.ds(r, tm // d, stride=d), :].astype(o_ref.dtype)


def norm_matmul_dilated(x, g, w, dilations, tn):
    t, k = x.shape
    n = w.shape[1]
    tm = min(PROJ_TOKEN_TILE, t)
    assert n % tn == 0 and all(tm % (d * SUBLANES) == 0 for d in dilations)
    return pl.pallas_call(
        functools.partial(_norm_matmul_dilated_kernel, dilations=dilations),
        grid=(t // tm, n // tn),
        in_specs=[pl.BlockSpec((tm, k), lambda i, j: (i, 0)),
                  pl.BlockSpec((1, k), lambda i, j: (0, 0)),
                  pl.BlockSpec((k, tn), lambda i, j: (0, j))],
        out_specs=[pl.BlockSpec((tm // d, d * tn), lambda i, j: (i, j)) for d in dilations],
        out_shape=[jax.ShapeDtypeStruct((t // d, d * n), BF16) for d in dilations],
        scratch_shapes=[pltpu.VMEM((tm, k), BF16), pltpu.VMEM((tn // LANES, tm, LANES), F32)],
        compiler_params=pltpu.CompilerParams(dimension_semantics=("parallel", "arbitrary"),
                                             vmem_limit_bytes=VMEM_LIMIT),
        name="norm_matmul_dilated",
    )(x, g.reshape(1, k), w)


A_Q_WIDTH = A_HEADS * LANES
A_PAIRS = A_HEADS // 2
A_KV_WIDTH = 2 * A_PAIRS * LANES
A_OUT_WIDTH = A_PAIRS * LANES


def _local_attn_kernel(q_ref, kvc_ref, kvp_ref, bias_ref, o_ref, lse_ref, kv_scr, *, nblk):
    w = A_BLOCK
    first_class_block = pl.program_id(2) == 0
    kv_scr[0:w, :] = kvp_ref[0]
    kv_scr[w:, :] = kvc_ref[0]
    lane = lax.broadcasted_iota(jnp.int32, (w, LANES), 1)
    col = lax.broadcasted_iota(jnp.int32, (w, 2 * w), 1)
    no_prev = jnp.where(col < w, jnp.where(first_class_block, NEG_INF, 0.0), 0.0)
    ones = jnp.ones((2 * w, LANES), BF16)
    for i in range(nblk):
        rows = slice(i * w, (i + 1) * w)
        for p in range(A_PAIRS):
            k_blk = kv_scr[i * w:(i + 2) * w, p * LANES:(p + 1) * LANES]
            v_blk = kv_scr[i * w:(i + 2) * w, (A_PAIRS + p) * LANES:(A_PAIRS + p + 1) * LANES]
            heads = (2 * p, 2 * p + 1)
            q = jnp.concatenate([q_ref[0, rows, hd * LANES:(hd + 1) * LANES] for hd in heads], axis=0)
            bias = jnp.concatenate([bias_ref[hd] for hd in heads], axis=0)
            if i == 0:
                bias = bias + jnp.concatenate([no_prev, no_prev], axis=0)
            s = _nt_dot(q, k_blk) + bias
            m = jnp.max(s, axis=-1, keepdims=True)
            pr = jnp.exp(s - m)
            pv = jnp.dot(pr.astype(BF16), jnp.concatenate([v_blk, ones], axis=1), preferred_element_type=F32)
            l = pv[:, LANES:]
            o = pv[:, :LANES] / l
            lse = m + jnp.log(l)
            o_ref[0, rows, p * LANES:(p + 1) * LANES] = jnp.where(lane < A_HEAD_DIM, o[:w], o[w:])
            lse_ref[0, rows, p * LANES:(p + 1) * LANES] = jnp.where(lane < A_HEAD_DIM, lse[:w], lse[w:])


def local_attn(view, bias, dilation):
    bsz, rows, width = view.shape
    d = dilation
    w = A_BLOCK
    nblk = min(A_BLOCKS_PER_STEP, rows // w)
    assert rows % (w * nblk) == 0 and width == d * (A_Q_WIDTH + A_KV_WIDTH) and A_Q_WIDTH == A_KV_WIDTH
    grid = (bsz, d, rows // (w * nblk))
    out_shape = jax.ShapeDtypeStruct((bsz, rows, d * A_OUT_WIDTH), F32)
    o, lse = pl.pallas_call(
        functools.partial(_local_attn_kernel, nblk=nblk),
        grid=grid,
        in_specs=[pl.BlockSpec((1, w * nblk, A_Q_WIDTH), lambda b, r, n: (b, n, r)),
                  pl.BlockSpec((1, w * nblk, A_KV_WIDTH), lambda b, r, n: (b, n, d + r)),
                  pl.BlockSpec((1, w, A_KV_WIDTH), lambda b, r, n: (b, jnp.maximum(n * nblk - 1, 0), d + r)),
                  pl.BlockSpec((A_HEADS, w, 2 * w), lambda b, r, n: (0, 0, 0))],
        out_specs=[pl.BlockSpec((1, w * nblk, A_OUT_WIDTH), lambda b, r, n: (b, n, r)),
                   pl.BlockSpec((1, w * nblk, A_OUT_WIDTH), lambda b, r, n: (b, n, r))],
        out_shape=[out_shape, out_shape],
        scratch_shapes=[pltpu.VMEM((w * (nblk + 1), A_KV_WIDTH), BF16)],
        compiler_params=pltpu.CompilerParams(dimension_semantics=("parallel", "parallel", "arbitrary")),
        name=f"local_attn_d{d}",
    )(view, view, view, bias)
    return o, lse


def _flash_pair_kernel(*refs, tq, tk, n_near, shared_k, mode):
    if shared_k:
        qa_ref, qb_ref, ka_ref, v_ref, bias_ref = refs[:5]
        kb_ref = ka_ref
        rest = refs[5:]
    else:
        qa_ref, qb_ref, ka_ref, kb_ref, v_ref, bias_ref = refs[:6]
        rest = refs[6:]
    if mode == "diff":
        prm_ref, gsub_ref, o_ref, m_scr, alpha_scr, acc_scr, s0_scr, s1_scr, p_scr = rest
    else:
        o_ref, m_scr, alpha_scr, acc_scr, s0_scr, s1_scr, p_scr = rest

    qi = pl.program_id(2)
    rb = FLASH_ROW_BLOCK
    m_scr[...] = jnp.full(m_scr.shape, NEG_INF, F32)
    acc_scr[...] = jnp.zeros(acc_scr.shape, F32)
    ones = jnp.ones((tk, LANES), BF16)
    last_kt = 2 * qi + 1

    def scores(kt, s_scr):
        koff = pl.multiple_of(jnp.minimum(kt, last_kt) * tk, tk)
        ka = ka_ref[0, pl.ds(koff, tk), :]
        if shared_k:
            s_scr[...] = _nt_dot(jnp.concatenate([qa_ref[0], qb_ref[0]], axis=0), ka)
        else:
            s_scr[0:tq, :] = _nt_dot(qa_ref[0], ka)
            s_scr[tq:, :] = _nt_dot(qb_ref[0], kb_ref[0, pl.ds(koff, tk), :])

    def accumulate(kt, s_scr, near):
        koff = pl.multiple_of(kt * tk, tk)
        for r in range(2 * tq // rb):
            rows = slice(r * rb, (r + 1) * rb)
            s = s_scr[rows, :]
            if near:
                b0 = (r * rb) % tq
                s = s + bias_ref[0, last_kt - kt, b0:b0 + rb, :]
            m_prev = m_scr[rows, :]
            m_new = jnp.maximum(m_prev, jnp.max(s, axis=-1, keepdims=True))
            alpha_scr[rows, :] = jnp.exp2(m_prev - m_new)
            m_scr[rows, :] = m_new
            p_scr[rows, :] = jnp.exp2(s - jnp.concatenate([m_new] * (tk // LANES), axis=1)).astype(BF16)
        v_ones = jnp.concatenate([v_ref[0, pl.ds(koff, tk), :], ones], axis=1)
        pv = jnp.dot(p_scr[...], v_ones, preferred_element_type=F32)
        alpha = alpha_scr[...]
        acc_scr[...] = jnp.concatenate([alpha, alpha], axis=1) * acc_scr[...] + pv

    n_tiles = 2 * qi + 2
    n_far = jnp.maximum(n_tiles - n_near, 0)

    def run(first, count, near):
        def unrolled(kt, steps):
            for step in range(steps):
                nxt, cur = (s1_scr, s0_scr) if step % 2 == 0 else (s0_scr, s1_scr)
                scores(kt + step + 1, nxt)
                accumulate(kt + step, cur, near)

        done = 0
        for steps in FLASH_UNROLLS:
            def body(i, carry, steps=steps, done=done):
                unrolled(first + done + steps * i, steps)
                return carry

            trips = (count - done) // steps
            lax.fori_loop(0, trips, body, 0)
            done = done + trips * steps

    scores(0, s0_scr)
    run(0, n_far, False)
    run(n_far, n_tiles - n_far, True)

    o = acc_scr[:, :LANES] / acc_scr[:, LANES:]
    oa, ob = o[:tq], o[tq:]
    if mode == "diff":
        prm = prm_ref[...]
        lam_init = prm[4:5, 0:1]
        lam = (jnp.exp(jnp.sum(prm[0:1] * prm[1:2], axis=-1, keepdims=True))
               - jnp.exp(jnp.sum(prm[2:3] * prm[3:4], axis=-1, keepdims=True)) + lam_init)
        d = oa - lam * ob
        o_ref[0] = (_rms(d, gsub_ref[...]) * (1.0 - lam_init)).astype(o_ref.dtype)
    else:
        lane = lax.broadcasted_iota(jnp.int32, oa.shape, 1)
        o_ref[0] = jnp.where(lane < C_V_DIM, oa, ob).astype(o_ref.dtype)


def _flash_pair_call(mode, q_arr, k_arr, v_arr, bias, groups, col_maps, out_cols, extra):
    bsz, seq, _ = q_arr.shape
    tq, tk = FLASH_Q_TILE, FLASH_K_TILE
    assert seq % tq == 0 and tq == 2 * tk
    n_near = bias.shape[1]
    assert n_near % 2 == 0
    qa_c, qb_c, ka_c, kb_c, v_c = col_maps
    shared_k = kb_c is None
    q_spec = lambda cm: pl.BlockSpec((1, tq, LANES), lambda b, g, i: (b, i, cm(g)))
    kv_spec = lambda cm: pl.BlockSpec((1, seq, LANES), lambda b, g, i: (b, 0, cm(g)))
    in_specs = [q_spec(qa_c), q_spec(qb_c), kv_spec(ka_c)]
    args = [q_arr, q_arr, k_arr]
    if not shared_k:
        in_specs.append(kv_spec(kb_c))
        args.append(k_arr)
    in_specs.append(kv_spec(v_c))
    args.append(v_arr)
    bias_group = (lambda g: g) if bias.shape[0] > 1 else (lambda g: 0)
    in_specs.append(pl.BlockSpec((1, n_near, tq, tk), lambda b, g, i: (bias_group(g), 0, 0, 0)))
    args.append(bias)
    for e in extra:
        in_specs.append(pl.BlockSpec(e.shape, lambda b, g, i: (0, 0)))
        args.append(e)
    return pl.pallas_call(
        functools.partial(_flash_pair_kernel, tq=tq, tk=tk, n_near=n_near, shared_k=shared_k, mode=mode),
        grid=(bsz, groups, seq // tq),
        in_specs=in_specs,
        out_specs=pl.BlockSpec((1, tq, LANES), lambda b, g, i: (b, i, g)),
        out_shape=jax.ShapeDtypeStruct((bsz, seq, out_cols), BF16),
        scratch_shapes=[pltpu.VMEM((2 * tq, LANES), F32), pltpu.VMEM((2 * tq, LANES), F32),
                        pltpu.VMEM((2 * tq, 2 * LANES), F32), pltpu.VMEM((2 * tq, tk), F32),
                        pltpu.VMEM((2 * tq, tk), F32), pltpu.VMEM((2 * tq, tk), BF16)],
        compiler_params=pltpu.CompilerParams(dimension_semantics=("parallel", "parallel", "arbitrary"),
                                             vmem_limit_bytes=VMEM_LIMIT),
        name=f"flash_pair_{mode}",
    )(*args)


B_Q_BLOCKS = 2 * B_HEADS
B_WIDTH_PADDED = (B_Q_BLOCKS + 2 * B_HEADS) * LANES


def diff_attn(qkv, bias, lam_prm, g_subln):
    return _flash_pair_call(
        "diff", qkv, qkv, qkv, bias, B_HEADS,
        (lambda g: 2 * g, lambda g: 2 * g + 1, lambda g: B_Q_BLOCKS + g, None,
         lambda g: B_Q_BLOCKS + B_HEADS + g),
        B_HEADS * B_V_DIM, [lam_prm, g_subln.reshape(1, B_V_DIM)])


def mla_attn(q, k, v, mask):
    return _flash_pair_call(
        "mla", q, k, v, mask, C_HEADS // 2,
        (lambda g: 2 * g, lambda g: 2 * g + 1, lambda g: 2 * g, lambda g: 2 * g + 1, lambda g: g),
        C_HEADS * C_V_DIM, [])


C_QK_WIDTH = C_HEADS * LANES
C_V_WIDTH = C_HEADS * C_V_DIM
C_LATENT_WIDTH = C_Q_RANK + C_KV_RANK + LANES
ROPE_SHIFT = LANES - C_ROPE_DIM


def _mla_prep_kernel(c_ref, gq_ref, gkv_ref, wq_ref, wk_ref, wv_ref, cos_ref, sin_ref, q_ref, k_ref, v_ref):
    c = c_ref[...]
    cos = cos_ref[...]
    sin = sin_ref[...]

    def rope(blk):
        return blk * cos + pltpu.roll(blk, ROPE_SHIFT, 1) * sin

    cq = _rms(c[:, :C_Q_RANK], gq_ref[...]).astype(BF16)
    ckv = _rms(c[:, C_Q_RANK:C_Q_RANK + C_KV_RANK], gkv_ref[...]).astype(BF16)
    q_all = jnp.dot(cq, wq_ref[...], preferred_element_type=F32)
    k_all = jnp.dot(ckv, wk_ref[...], preferred_element_type=F32)
    k_rope = rope(c[:, C_Q_RANK + C_KV_RANK:])
    scale = LOG2E / math.sqrt(C_NOPE_DIM + C_ROPE_DIM)
    for h in range(C_HEADS):
        cols = slice(h * LANES, (h + 1) * LANES)
        q_ref[:, cols] = (rope(q_all[:, cols]) * scale).astype(q_ref.dtype)
        k_ref[:, cols] = (k_all[:, cols] + k_rope).astype(k_ref.dtype)
    v_ref[...] = jnp.dot(ckv, wv_ref[...], preferred_element_type=F32).astype(v_ref.dtype)


def mla_prep(c_lat, c_col_block, g_cq, g_ckv, w_q, w_k, w_v, cos_t, sin_t, seq):
    t = c_lat.shape[0]
    tm = min(TOKEN_TILE, seq)
    per_seq = seq // tm
    full = lambda a: pl.BlockSpec(a.shape, lambda i: (0, 0))
    g_cq = g_cq.reshape(1, -1)
    g_ckv = g_ckv.reshape(1, -1)
    return pl.pallas_call(
        _mla_prep_kernel,
        grid=(t // tm,),
        in_specs=[pl.BlockSpec((tm, C_LATENT_WIDTH), lambda i: (i, c_col_block)),
                  full(g_cq), full(g_ckv), full(w_q), full(w_k), full(w_v),
                  pl.BlockSpec((tm, LANES), lambda i: (i % per_seq, 0)),
                  pl.BlockSpec((tm, LANES), lambda i: (i % per_seq, 0))],
        out_specs=[pl.BlockSpec((tm, C_QK_WIDTH), lambda i: (i, 0)),
                   pl.BlockSpec((tm, C_QK_WIDTH), lambda i: (i, 0)),
                   pl.BlockSpec((tm, C_V_WIDTH), lambda i: (i, 0))],
        out_shape=[jax.ShapeDtypeStruct((t, C_QK_WIDTH), BF16), jax.ShapeDtypeStruct((t, C_QK_WIDTH), BF16),
                   jax.ShapeDtypeStruct((t, C_V_WIDTH), BF16)],
        compiler_params=pltpu.CompilerParams(dimension_semantics=("parallel",)),
        name="mla_prep",
    )(c_lat, g_cq, g_ckv, w_q, w_k, w_v, cos_t, sin_t)


def _merge_proj_kernel(x_ref, ga_ref, gb_ref, gc_ref, o1_ref, o2_ref, o3_ref, l1_ref, l2_ref, l3_ref,
                       yb_ref, yc_ref, wa_ref, wb_ref, wc_ref, wo_ref, gffn_ref, wpq_ref,
                       xo_ref, h2_ref, qp_ref, *token_order_scr):
    tm = x_ref.shape[0]

    def token_order(ref, scr):
        d = ref.shape[1] // A_OUT_WIDTH
        if d == 1:
            return ref[...]
        for r in range(d):
            for c in range(A_PAIRS):
                cols = slice(r * A_OUT_WIDTH + c * LANES, r * A_OUT_WIDTH + (c + 1) * LANES)
                scr[c, pl.ds(r, tm // d, stride=d), :] = ref[:, cols]
        return jnp.concatenate([scr[c] for c in range(A_PAIRS)], axis=1)

    o1, o2, o3, l1, l2, l3 = [token_order(ref, scr) for ref, scr in
                              zip((o1_ref, o2_ref, o3_ref, l1_ref, l2_ref, l3_ref), token_order_scr)]
    m = jnp.maximum(jnp.maximum(l1, l2), l3)
    e1, e2, e3 = jnp.exp(l1 - m), jnp.exp(l2 - m), jnp.exp(l3 - m)
    den = e1 + e2 + e3
    ya = (e1 / den) * o1 + (e2 / den) * o2 + (e3 / den) * o3
    pa = jnp.dot(ya.astype(BF16), wa_ref[...], preferred_element_type=F32)
    pb = jnp.dot(yb_ref[...], wb_ref[...], preferred_element_type=F32)
    pc = jnp.dot(yc_ref[...], wc_ref[...], preferred_element_type=F32)
    mix = (jax.nn.sigmoid(ga_ref[...]) * pa + jax.nn.sigmoid(gb_ref[...]) * pb
           + jax.nn.sigmoid(gc_ref[...]) * pc)
    x_new = x_ref[...] + jnp.dot(mix.astype(BF16), wo_ref[...], preferred_element_type=F32)
    xo_ref[...] = x_new
    h2 = _rms(x_new, gffn_ref[...]).astype(BF16)
    h2_ref[...] = h2
    qp_ref[...] = jnp.dot(h2, wpq_ref[...], preferred_element_type=F32)


def merge_proj(x, gc_arr, a_outs, a_lses, y_b, y_c, w_a, w_b, w_c, w_o, g_ffn, w_pq):
    t, d = x.shape
    tm = min(TOKEN_TILE, t)
    row = lambda width, col=0: pl.BlockSpec((tm, width), lambda i: (i, col))
    full = lambda a: pl.BlockSpec(a.shape, lambda i: (0, 0))
    g_ffn = g_ffn.reshape(1, d)
    group = lambda a: pl.BlockSpec((tm // (a.shape[1] // A_OUT_WIDTH), a.shape[1]), lambda i: (i, 0))
    in_specs = ([row(d)] + [row(d, c) for c in range(3)]
                + [group(a) for a in (*a_outs, *a_lses)] + [row(y_b.shape[1]), row(y_c.shape[1])]
                + [full(w_a), full(w_b), full(w_c), full(w_o), full(g_ffn), full(w_pq)])
    return pl.pallas_call(
        _merge_proj_kernel,
        grid=(t // tm,),
        in_specs=in_specs,
        out_specs=[row(d), row(d), row(d)],
        out_shape=[jax.ShapeDtypeStruct((t, d), F32), jax.ShapeDtypeStruct((t, d), BF16),
                   jax.ShapeDtypeStruct((t, d), F32)],
        scratch_shapes=[pltpu.VMEM((A_PAIRS, tm, LANES), F32)] * 6,
        compiler_params=pltpu.CompilerParams(dimension_semantics=("parallel",), vmem_limit_bytes=VMEM_LIMIT),
        name="merge_proj",
    )(x, gc_arr, gc_arr, gc_arr, *a_outs, *a_lses, y_b, y_c, w_a, w_b, w_c, w_o, g_ffn, w_pq)


def _oddeven_merge_sort_pairs(n):
    pairs = []
    p = 1
    while p < n:
        k = p
        while k >= 1:
            for j in range(k % p, n - k, 2 * k):
                for i in range(min(k, n - j - k)):
                    if (i + j) // (2 * p) == (i + j + k) // (2 * p):
                        pairs.append((i + j, i + j + k))
            k //= 2
        p *= 2
    return pairs


def _bitonic_merge_pairs(n):
    pairs = []
    stride = n // 2
    while stride >= 1:
        pairs += [(i, i + stride) for i in range(n) if not i & stride]
        stride //= 2
    return pairs


def _compare_exchange(vals, pairs):
    for i, j in pairs:
        a, b = vals[i], vals[j]
        if b is None:
            continue
        if a is None:
            vals[i], vals[j] = b, None
        else:
            vals[i], vals[j] = jnp.maximum(a, b), jnp.minimum(a, b)
    return vals


_SORT16 = _oddeven_merge_sort_pairs(PEER_TOPK)
_MERGE16 = _bitonic_merge_pairs(PEER_TOPK)
_CAND_PAIRS = [(a, b) for a in range(PEER_TOPK) for b in range(PEER_TOPK) if (a + 1) * (b + 1) <= PEER_TOPK]
_SORT64 = _oddeven_merge_sort_pairs(64)


def _top16_sorted(scores_t):
    slabs = [scores_t[v * SUBLANES:(v + 1) * SUBLANES, :] for v in range(PEER_KEYS // SUBLANES)]
    slabs = _compare_exchange(slabs, _SORT16)
    shift = SUBLANES // 2
    while shift >= 1:
        other = [pltpu.roll(s, shift, 0) for s in slabs]
        slabs = [jnp.maximum(slabs[r], other[PEER_TOPK - 1 - r]) for r in range(PEER_TOPK)]
        slabs = _compare_exchange(slabs, _MERGE16)
        shift //= 2
    return slabs


def _peer_select_kernel(qp_ref, keys_ref, theta_ref, s2_ref, c1_ref, e2_ref, s1_scr):
    tm = qp_ref.shape[0]
    sub = lax.broadcasted_iota(jnp.int32, (SUBLANES, tm), 0)
    tops = [[None] * PEER_TOPK for _ in range(2)]
    for h in range(PEER_HEADS):
        qh = qp_ref[:, h * LANES:(h + 1) * LANES].astype(BF16)
        for p in range(2):
            s_t = _nt_dot(keys_ref[2 * h + p], qh)
            if p == 0:
                s1_scr[h] = s_t
            else:
                for c in range(tm // LANES):
                    s2_ref[h, c] = s_t[:, c * LANES:(c + 1) * LANES]
            top = _top16_sorted(s_t)
            for r in range(PEER_TOPK):
                tops[p][r] = top[r] if h == 0 else jnp.where(sub == h, top[r], tops[p][r])
    cand = [tops[0][a] + tops[1][b] for a, b in _CAND_PAIRS] + [None] * (64 - len(_CAND_PAIRS))
    best = _compare_exchange(cand, _SORT64)[:PEER_TOPK]
    mx = best[0]
    z = functools.reduce(lambda u, w_: u + w_, [jnp.exp(b - mx) for b in best])
    thr = best[PEER_TOPK - 1]
    m1, m2 = tops[0][0], tops[1][0]

    def smallest_selected(s1_vals, thr_vals, t2_of, n_b):
        theta = jnp.full(s1_vals.shape, -NEG_INF, F32)
        for b in range(n_b):
            theta = jnp.where(s1_vals + t2_of(b) >= thr_vals, t2_of(b), theta)
        return theta

    n_best = 3
    short = PEER_TOPK // (n_best + 1)
    theta_best = [smallest_selected(tops[0][a], thr, lambda b: tops[1][b], PEER_TOPK) for a in range(n_best)]
    for h in range(PEER_HEADS):
        row = lambda a: a[h:h + 1, :]
        s1 = s1_scr[h]
        theta = smallest_selected(s1, row(thr), lambda b: row(tops[1][b]), short)
        for a in range(n_best):
            theta = jnp.where(s1 == row(tops[0][a]), row(theta_best[a]), theta)
        theta_ref[h] = theta
        c1_ref[h] = jnp.exp(s1 - row(m1)) * (0.5 / row(z))
        for c in range(tm // LANES):
            e2_ref[h, c] = jnp.exp(s2_ref[h, c] - m2[h:h + 1, c * LANES:(c + 1) * LANES])


def peer_select(qp, keys_padded):
    t = qp.shape[0]
    tm = min(PEER_TOKEN_TILE, t)
    first = jax.ShapeDtypeStruct((PEER_HEADS, PEER_KEYS, t), F32)
    first_spec = pl.BlockSpec((PEER_HEADS, PEER_KEYS, tm), lambda i: (0, 0, i))
    second = jax.ShapeDtypeStruct((PEER_HEADS, t // LANES, PEER_KEYS, LANES), F32)
    second_spec = pl.BlockSpec((PEER_HEADS, tm // LANES, PEER_KEYS, LANES), lambda i: (0, i, 0, 0))
    return pl.pallas_call(
        _peer_select_kernel,
        grid=(t // tm,),
        in_specs=[pl.BlockSpec((tm, PEER_HEADS * PEER_KEY_DIM), lambda i: (i, 0)),
                  pl.BlockSpec(keys_padded.shape, lambda i: (0, 0, 0))],
        out_specs=[first_spec, second_spec, first_spec, second_spec],
        out_shape=[first, second, first, second],
        scratch_shapes=[pltpu.VMEM((PEER_HEADS, PEER_KEYS, tm), F32)],
        compiler_params=pltpu.CompilerParams(dimension_semantics=("parallel",), vmem_limit_bytes=VMEM_LIMIT),
        name="peer_select",
    )(qp, keys_padded)


def _peer_dense_kernel(h_ref, u_ref, vt_ref, theta_ref, s2_ref, c1_ref, e2_ref, x_ref, g_ref,
                       o_ref, acc_scr, gate_scr, wa_scr, *, final_norm):
    j = pl.program_id(1)
    tm = h_ref.shape[0]
    ec = u_ref.shape[0]
    rows_per_tile = ec // PEER_KEYS

    @pl.when(j == 0)
    def _():
        acc_scr[...] = jnp.zeros(acc_scr.shape, F32)

    n_chunks = tm // LANES

    def build(tc, carry):
        cols = pl.ds(pl.multiple_of(tc * LANES, LANES), LANES)
        for ii in range(rows_per_tile):
            rows = slice(ii * PEER_KEYS, (ii + 1) * PEER_KEYS)
            w = None
            for h in range(PEER_HEADS):
                gate = e2_ref[h, tc] * c1_ref[h, ii:ii + 1, cols]
                term = jnp.where(s2_ref[h, tc] >= theta_ref[h, ii:ii + 1, cols], gate, 0.0)
                w = term if w is None else w + term
            gate_scr[tc, rows, :] = w.astype(BF16)
        return carry

    lax.fori_loop(0, n_chunks, build, 0)
    halves = [range(0, n_chunks // 2), range(n_chunks // 2, n_chunks)]
    for half in halves:
        rows = slice(half[0] * LANES, (half[-1] + 1) * LANES)
        act = _twice_gelu(_nt_dot(u_ref[...], h_ref[rows, :]))
        for n, tc in enumerate(half):
            wa_scr[tc] = act[:, n * LANES:(n + 1) * LANES].astype(BF16) * gate_scr[tc]
    for half in halves:
        cols = slice(half[0] * LANES, (half[-1] + 1) * LANES)
        wa = jnp.concatenate([wa_scr[tc] for tc in half], axis=1)
        acc_scr[:, cols] += jnp.dot(vt_ref[...], wa, preferred_element_type=F32)

    @pl.when(j == pl.num_programs(1) - 1)
    def _():
        y = x_ref[...] + acc_scr[...].T
        if final_norm:
            y = _rms(y, g_ref[...])
        o_ref[...] = y


def peer_dense(h2, u_bf, vt_bf, theta, s2, c1, e2, x, g_final, final_norm):
    t, d = x.shape
    n_exp = u_bf.shape[0]
    tm = min(PEER_TOKEN_TILE, t)
    ec = PEER_EXPERT_TILE
    sel_spec = pl.BlockSpec((PEER_HEADS, tm // LANES, PEER_KEYS, LANES), lambda i, j: (0, i, 0, 0))
    key_spec = pl.BlockSpec((PEER_HEADS, ec // PEER_KEYS, tm), lambda i, j: (0, j, i))
    return pl.pallas_call(
        functools.partial(_peer_dense_kernel, final_norm=final_norm),
        grid=(t // tm, n_exp // ec),
        in_specs=[pl.BlockSpec((tm, d), lambda i, j: (i, 0)),
                  pl.BlockSpec((ec, d), lambda i, j: (j, 0)),
                  pl.BlockSpec((d, ec), lambda i, j: (0, j)),
                  key_spec, sel_spec, key_spec, sel_spec,
                  pl.BlockSpec((tm, d), lambda i, j: (i, 0)),
                  pl.BlockSpec((1, d), lambda i, j: (0, 0))],
        out_specs=pl.BlockSpec((tm, d), lambda i, j: (i, 0)),
        out_shape=jax.ShapeDtypeStruct((t, d), F32),
        scratch_shapes=[pltpu.VMEM((d, tm), F32), pltpu.VMEM((tm // LANES, ec, LANES), BF16),
                        pltpu.VMEM((tm // LANES, ec, LANES), BF16)],
        compiler_params=pltpu.CompilerParams(dimension_semantics=("parallel", "arbitrary"),
                                             vmem_limit_bytes=VMEM_LIMIT),
        name="peer_dense",
    )(h2, u_bf, vt_bf, theta, s2, c1, e2, x, g_final.reshape(1, d))


def _t5_bucket_table(max_dist):
    n = np.arange(max_dist + 1)
    max_exact = N_BUCKETS // 2
    nf = np.maximum(n, max_exact).astype(np.float32)
    large = max_exact + (np.log(nf / np.float32(max_exact)) / np.float32(math.log(MAX_DISTANCE / max_exact))
                         * np.float32(N_BUCKETS - max_exact)).astype(np.int32)
    large = np.minimum(large, N_BUCKETS - 1)
    return np.where(n < max_exact, n, large).astype(np.int32)


def _skew(vec, rows):
    h, length = vec.shape
    return jnp.tile(vec, (1, rows))[:, :rows * (length - 1)].reshape(h, rows, length - 1)


def _banded_bias(bias_heads, dilation):
    w = A_BLOCK
    heads = bias_heads.shape[1]
    by_rel = bias_heads[_t5_bucket_table(w * dilation)[np.arange(w + 1) * dilation]].T.astype(F32)
    vec = jnp.concatenate([by_rel, jnp.full((heads, 3 * w - (w + 1)), NEG_INF, F32)], axis=1)
    return jnp.transpose(_skew(vec, 2 * w)[:, :, w:2 * w], (0, 2, 1))


def _diagonal_tiles(by_dist, tq, tk, n_tiles):
    heads = by_dist.shape[0]
    n_cols = (n_tiles - 1) * tk + tq
    mask = jnp.full((heads, tk), NEG_INF, F32)
    vec = jnp.concatenate([mask, by_dist, mask], axis=1)
    assert vec.shape[1] == n_cols + tk
    skew = _skew(vec, tk)
    tiles = jnp.stack([skew[:, :, i * tk:i * tk + tq] for i in range(n_tiles)], axis=1)
    return jnp.transpose(tiles, (0, 1, 3, 2))


def _causal_bias_tiles(bias_heads, tq, tk):
    n_tiles = 2 * -(-(MAX_DISTANCE - 1 + tk + tk) // (2 * tk))
    far = bias_heads[N_BUCKETS - 1]
    n_dist = (n_tiles - 2) * tk + tq
    by_dist = (bias_heads - far[None, :])[_t5_bucket_table(n_dist - 1)].T.astype(F32) * LOG2E
    return _diagonal_tiles(by_dist, tq, tk, n_tiles)


def _causal_mask_tiles(tq, tk):
    return _diagonal_tiles(jnp.zeros((1, tq), F32), tq, tk, 2)


def _rope_tables(seq):
    half = C_ROPE_DIM // 2
    inv = ROPE_BASE ** (-jnp.arange(half, dtype=F32) / half)
    ang = jnp.arange(seq, dtype=F32)[:, None] * inv[None, :]
    cos, sin = jnp.cos(ang), jnp.sin(ang)
    ones = jnp.ones((seq, C_NOPE_DIM), F32)
    zeros_n = jnp.zeros((seq, C_NOPE_DIM), F32)
    spare = jnp.zeros((seq, LANES - C_NOPE_DIM - C_ROPE_DIM), F32)
    cos_t = jnp.concatenate([ones, cos, cos, spare], axis=1)
    sin_t = jnp.concatenate([zeros_n, -sin, sin, spare], axis=1)
    return cos_t, sin_t


def _swap_halves(w):
    half = w.shape[-1] // 2
    return jnp.concatenate([w[..., half:], w[..., :half]], axis=-1)


def _prep_in_proj(w_in):
    d = w_in.shape[0]
    bounds = np.cumsum([A_COLS, B_QK_COLS, B_QK_COLS, B_V_COLS, C_Q_RANK, C_KV_RANK, C_ROPE_DIM])
    wa, wbq, wbk, wbv, wcq, wckv, wckr, wg = jnp.split(w_in, bounds, axis=1)
    zeros = lambda n: jnp.zeros((d, n), w_in.dtype)
    wa = wa.reshape(d, 3, A_HEADS, A_HEAD_DIM)
    qa = wa[:, 0] * (1.0 / math.sqrt(A_HEAD_DIM))
    q_blocks = []
    for h in range(A_HEADS):
        pad = zeros(A_HEAD_DIM)
        q_blocks += [qa[:, h], pad] if h % 2 == 0 else [pad, qa[:, h]]
    w_a = jnp.concatenate(q_blocks + [wa[:, 1].reshape(d, -1), wa[:, 2].reshape(d, -1)], axis=1)
    qb = wbq.reshape(d, B_HEADS, 2, B_QK_DIM) * (LOG2E / math.sqrt(B_QK_DIM))
    q_blocks = []
    for h in range(B_HEADS):
        pad = zeros(B_QK_DIM)
        q_blocks += [qb[:, h, 0], pad, pad, qb[:, h, 1]]
    w_b = jnp.concatenate(q_blocks + [wbk, wbv], axis=1)
    w_gc = jnp.concatenate([wg, wcq, wckv, zeros(C_NOPE_DIM), wckr, _swap_halves(wckr)], axis=1)
    return w_a.astype(BF16), w_b.astype(BF16), w_gc.astype(BF16)


def _prep_mla_weights(w_uq, w_ukv):
    rq = w_uq.shape[0]
    wq = w_uq.reshape(rq, C_HEADS, C_NOPE_DIM + C_ROPE_DIM)
    q_rope = wq[..., C_NOPE_DIM:]
    wq = jnp.concatenate([wq, _swap_halves(q_rope)], axis=-1).reshape(rq, C_QK_WIDTH)
    rkv = w_ukv.shape[0]
    wkv = w_ukv.reshape(rkv, C_HEADS, C_NOPE_DIM + C_V_DIM)
    wk = jnp.concatenate([wkv[..., :C_NOPE_DIM], jnp.zeros((rkv, C_HEADS, LANES - C_NOPE_DIM), w_ukv.dtype)],
                         axis=-1).reshape(rkv, C_QK_WIDTH)
    wv = wkv[..., C_NOPE_DIM:].reshape(rkv, C_V_WIDTH)
    return wq.astype(BF16), wk.astype(BF16), wv.astype(BF16)


def _prep_peer_keys(sub_keys):
    half = PEER_KEY_DIM // 2
    z = jnp.zeros(sub_keys.shape[:1] + sub_keys.shape[2:], sub_keys.dtype)
    first = jnp.concatenate([sub_keys[:, 0], z], axis=-1)
    second = jnp.concatenate([z, sub_keys[:, 1]], axis=-1)
    assert first.shape[-1] == LANES and half * 2 == LANES
    return jnp.stack([first, second], axis=1).reshape(2 * PEER_HEADS, PEER_KEYS, LANES).astype(BF16)


def kernel(x, rel_bias, w_in, g_mix, w_uq, g_cq, w_ukv, g_ckv, lam_q1, lam_k1, lam_q2, lam_k2, g_subln,
           w_branch_a, w_branch_b, w_branch_c, w_out, g_ffn, w_peer_q, peer_sub_keys, peer_u, peer_v, g_final):
    bsz, seq, d = x.shape
    t = bsz * seq
    depth = w_in.shape[0]
    bias_a = rel_bias[:, :A_HEADS]
    bias_b = rel_bias[:, A_HEADS:]
    a_biases = [_banded_bias(bias_a, dil) for _, dil in A_CONFIGS]
    b_bias = _causal_bias_tiles(bias_b, FLASH_Q_TILE, FLASH_K_TILE)
    c_mask = _causal_mask_tiles(FLASH_Q_TILE, FLASH_K_TILE)
    cos_t, sin_t = _rope_tables(seq)
    gate_blocks = 3 * d // C_LATENT_WIDTH

    xf = x.reshape(t, d)
    for l in range(depth):
        w_a, w_b, w_gc = _prep_in_proj(w_in[l])
        dilations = tuple(dil for _, dil in A_CONFIGS)
        views_a = norm_matmul_dilated(xf, g_mix[l], w_a, dilations, A_Q_WIDTH)
        qkv_b = norm_matmul(xf, g_mix[l], w_b, BF16).reshape(bsz, seq, -1)
        gc = norm_matmul(xf, g_mix[l], w_gc, F32)

        a_outs, a_lses = [], []
        for dil, view, bias in zip(dilations, views_a, a_biases):
            o, lse = local_attn(view.reshape(bsz, seq // dil, -1), bias, dil)
            a_outs.append(o.reshape(t // dil, -1))
            a_lses.append(lse.reshape(t // dil, -1))

        lam_init = 0.8 - 0.6 * math.exp(-0.3 * l)
        pad = lambda v: jnp.pad(v.astype(F32), (0, LANES - v.shape[0]))
        lam_prm = jnp.stack([pad(lam_q1[l]), pad(lam_k1[l]), pad(lam_q2[l]), pad(lam_k2[l]),
                             jnp.full((LANES,), lam_init, F32)]
                            + [jnp.zeros((LANES,), F32)] * (SUBLANES - 5))
        y_b = diff_attn(qkv_b, b_bias, lam_prm, g_subln[l]).reshape(t, -1)

        wq, wk, wv = _prep_mla_weights(w_uq[l], w_ukv[l])
        q_c, k_c, v_c = mla_prep(gc, gate_blocks, g_cq[l], g_ckv[l], wq, wk, wv, cos_t, sin_t, seq)
        y_c = mla_attn(q_c.reshape(bsz, seq, -1), k_c.reshape(bsz, seq, -1), v_c.reshape(bsz, seq, -1),
                       c_mask).reshape(t, -1)

        xf, h2, qp = merge_proj(xf, gc, a_outs, a_lses, y_b, y_c,
                                w_branch_a[l].astype(BF16), w_branch_b[l].astype(BF16),
                                w_branch_c[l].astype(BF16), w_out[l].astype(BF16), g_ffn[l],
                                w_peer_q[l].astype(BF16))

        theta, s2, c1, e2 = peer_select(qp, _prep_peer_keys(peer_sub_keys[l]))
        xf = peer_dense(h2, peer_u[l].astype(BF16), peer_v[l].T.astype(BF16), theta, s2, c1, e2, xf,
                        g_final, final_norm=(l == depth - 1))
    return xf.reshape(bsz, seq, d)
```
